```python
import jax
import jax.numpy as jnp
from jax import lax
import numpy as np

D_MODEL = 1024
BATCH = 16
SEQ = 2048
DEPTH = 1
DEC_BATCH = 16
DEC_SEQ = 32
PAST_LEN = 4096

CHUNK = 64
LEFT_CHUNKS = 8
ATT_WINDOW = LEFT_CHUNKS * CHUNK
A_HEADS = 8
A_HEAD_DIM = 64
A_WIDTH = A_HEADS * A_HEAD_DIM
REL_CLIP = 128
R_HEADS = 4
R_KEY_DIM = 128
R_VAL_DIM = 128
R_WIDTH = R_HEADS * R_VAL_DIM
ROPE_BASE = 10000.0
D_FF = 4 * D_MODEL
NORM_EPS = 1e-6
GN_EPS = 1e-6

IN_SPLITS = (A_WIDTH, A_WIDTH, A_WIDTH, R_HEADS * R_KEY_DIM, R_HEADS * R_KEY_DIM, R_WIDTH, R_WIDTH, D_MODEL, D_MODEL)
IN_COLS = sum(IN_SPLITS)
IN_OFFSETS = tuple(int(o) for o in np.cumsum(IN_SPLITS)[:-1])

kernel_name = 'chunkband_relbias_attn_retention_hybrid_step'


def rms_norm(x, g):
    xf = x.astype(jnp.float32)
    y = xf * lax.rsqrt(jnp.mean(xf * xf, axis=-1, keepdims=True) + NORM_EPS)
    return (y * g.astype(jnp.float32)).astype(x.dtype)


def rotary(x, pos):
    half = x.shape[-1] // 2
    inv = ROPE_BASE ** (-jnp.arange(half, dtype=jnp.float32) / half)
    ang = pos[:, None] * inv[None, :]
    cos = jnp.cos(ang)[:, None, :]
    sin = jnp.sin(ang)[:, None, :]
    xf = x.astype(jnp.float32)
    x1, x2 = xf[..., :half], xf[..., half:]
    return jnp.concatenate([x1 * cos - x2 * sin, x1 * sin + x2 * cos], axis=-1).astype(x.dtype)


def retention_log_decay():
    return jnp.log1p(-jnp.exp2(-5.0 - jnp.arange(R_HEADS, dtype=jnp.float32)))


def mixer_inputs(h, pos, w_in):
    B, T, _ = h.shape
    z = h @ w_in
    qa, ka, va, qr, kr, vr, gr, ga, gb = jnp.split(z, IN_OFFSETS, axis=-1)
    qa = qa.reshape(B, T, A_HEADS, A_HEAD_DIM)
    ka = ka.reshape(B, T, A_HEADS, A_HEAD_DIM)
    va = va.reshape(B, T, A_HEADS, A_HEAD_DIM)
    qr = rotary(qr.reshape(B, T, R_HEADS, R_KEY_DIM), pos)
    kr = rotary(kr.reshape(B, T, R_HEADS, R_KEY_DIM), pos) * (R_KEY_DIM ** -0.5)
    vr = vr.reshape(B, T, R_HEADS, R_VAL_DIM)
    return qa, ka, va, qr, kr, vr, gr, ga, gb


def chunk_attention_prompt(q, k, v, table):
    B, S, H, dh = q.shape
    nc = S // CHUNK
    band = LEFT_CHUNKS + 1
    pad = ((0, 0), (ATT_WINDOW, 0), (0, 0), (0, 0))
    kc = jnp.pad(k, pad).reshape(B, nc + LEFT_CHUNKS, CHUNK, H, dh)
    vc = jnp.pad(v, pad).reshape(B, nc + LEFT_CHUNKS, CHUNK, H, dh)
    kb = jnp.concatenate([kc[:, i:i + nc] for i in range(band)], axis=2)
    vb = jnp.concatenate([vc[:, i:i + nc] for i in range(band)], axis=2)
    qc = q.reshape(B, nc, CHUNK, H, dh)
    s = jnp.einsum('bnqhd,bnkhd->bnhqk', qc, kb, preferred_element_type=jnp.float32) * (dh ** -0.5)
    r = jnp.arange(band * CHUNK)
    dist = jnp.arange(CHUNK)[:, None] - r[None, :] + ATT_WINDOW
    bias = table[:, jnp.clip(dist, -REL_CLIP, REL_CLIP) + REL_CLIP].astype(jnp.float32)
    valid = (jnp.arange(nc)[:, None] * CHUNK - ATT_WINDOW + r[None, :]) >= 0
    s = jnp.where(valid[None, :, None, None, :], s + bias[None, None], -jnp.inf)
    p = jax.nn.softmax(s, axis=-1).astype(v.dtype)
    o = jnp.einsum('bnhqk,bnkhd->bnqhd', p, vb)
    return o.reshape(B, S, H, dh)


def chunk_attention_sample(q, k_new, v_new, k_cache, v_cache, table):
    L = k_cache.shape[1]
    T = q.shape[1]
    k = jnp.concatenate([k_cache.astype(k_new.dtype), k_new], axis=1)
    v = jnp.concatenate([v_cache.astype(v_new.dtype), v_new], axis=1)
    s = jnp.einsum('bqhd,bkhd->bhqk', q, k, preferred_element_type=jnp.float32) * (A_HEAD_DIM ** -0.5)
    dist = jnp.arange(T)[:, None] - (jnp.arange(L + T)[None, :] - L)
    bias = table[:, jnp.clip(dist, -REL_CLIP, REL_CLIP) + REL_CLIP].astype(jnp.float32)
    p = jax.nn.softmax(s + bias[None], axis=-1).astype(v.dtype)
    return jnp.einsum('bhqk,bkhd->bqhd', p, v)


def retention_block(q, k, v, state, lg):
    T = q.shape[1]
    qf = q.astype(jnp.float32)
    kf = k.astype(jnp.float32)
    vf = v.astype(jnp.float32)
    idx = jnp.arange(T, dtype=jnp.float32)
    diff = idx[:, None] - idx[None, :]
    decay = jnp.where(diff[None] >= 0, jnp.exp(jnp.maximum(diff, 0.0)[None] * lg[:, None, None]), 0.0)
    s = jnp.einsum('bthd,bshd->bhts', qf, kf) * decay[None]
    intra = jnp.einsum('bhts,bshe->bthe', s, vf)
    cross = jnp.einsum('bthd,bhde->bthe', qf, state) * jnp.exp((idx + 1.0)[:, None] * lg[None, :])[None, :, :, None]
    kdec = kf * jnp.exp((T - 1.0 - idx)[:, None] * lg[None, :])[None, :, :, None]
    new_state = state * jnp.exp(T * lg)[None, :, None, None] + jnp.einsum('bthd,bthe->bhde', kdec, vf)
    return intra + cross, new_state


def retention_prompt(q, k, v, lg):
    B, S, H, dk = q.shape
    nc = S // CHUNK

    def to_chunks(t):
        return jnp.moveaxis(t.reshape(B, nc, CHUNK, H, t.shape[-1]), 1, 0)

    def step(state, qkv):
        o, state = retention_block(qkv[0], qkv[1], qkv[2], state, lg)
        return state, o

    s0 = jnp.zeros((B, H, dk, R_VAL_DIM), jnp.float32)
    s_fin, o = lax.scan(step, s0, (to_chunks(q), to_chunks(k), to_chunks(v)))
    return jnp.moveaxis(o, 0, 1).reshape(B, S, H, R_VAL_DIM), s_fin


def head_group_norm(o):
    mu = jnp.mean(o, axis=-1, keepdims=True)
    var = jnp.mean(jnp.square(o - mu), axis=-1, keepdims=True)
    return (o - mu) * lax.rsqrt(var + GN_EPS)


def mixer_output(att, ret, gr, ga, gb, w_attn_out, w_ret_out, w_o):
    B, T = att.shape[:2]
    dt = att.dtype
    ya = att.reshape(B, T, A_WIDTH) @ w_attn_out
    rn = head_group_norm(ret) * jax.nn.silu(gr.astype(jnp.float32)).reshape(B, T, R_HEADS, R_VAL_DIM)
    yr = rn.astype(dt).reshape(B, T, R_WIDTH) @ w_ret_out
    m = jax.nn.sigmoid(ga) * ya + jax.nn.sigmoid(gb) * yr
    return m @ w_o


def block_tail(x, mix, g_mix_post, g_ffn_pre, g_ffn_post, w_up, w_down):
    x = x + rms_norm(mix, g_mix_post)
    h = rms_norm(x, g_ffn_pre)
    f = jnp.square(jax.nn.relu(h @ w_up)) @ w_down
    return x + rms_norm(f, g_ffn_post)


def setup_inputs(seed: int = 0) -> dict:
    key = jax.random.key(seed)
    ks = jax.random.split(key, 18)

    def nrm(k, shape, scale):
        return scale * jax.random.normal(k, shape, jnp.float32)

    cache_len = min(ATT_WINDOW, PAST_LEN)
    return {
        'x_prompt': nrm(ks[0], (BATCH, SEQ, D_MODEL), 1.0),
        'x_sample': nrm(ks[1], (DEC_BATCH, DEC_SEQ, D_MODEL), 1.0),
        'cache_attn_k': nrm(ks[2], (DEPTH, DEC_BATCH, cache_len, A_HEADS, A_HEAD_DIM), 1.0),
        'cache_attn_v': nrm(ks[3], (DEPTH, DEC_BATCH, cache_len, A_HEADS, A_HEAD_DIM), 1.0),
        'state_retention': nrm(ks[4], (DEPTH, DEC_BATCH, R_HEADS, R_KEY_DIM, R_VAL_DIM), 1.0),
        'norm_mix_pre': 1.0 + nrm(ks[5], (DEPTH, D_MODEL), 0.05),
        'w_in': nrm(ks[6], (DEPTH, D_MODEL, IN_COLS), D_MODEL ** -0.5),
        'rel_bias_table': nrm(ks[7], (DEPTH, A_HEADS, 2 * REL_CLIP + 1), 0.5),
        'w_attn_out': nrm(ks[8], (DEPTH, A_WIDTH, D_MODEL), A_WIDTH ** -0.5),
        'w_ret_out': nrm(ks[9], (DEPTH, R_WIDTH, D_MODEL), R_WIDTH ** -0.5),
        'w_o': nrm(ks[10], (DEPTH, D_MODEL, D_MODEL), D_MODEL ** -0.5),
        'norm_mix_post': 1.0 + nrm(ks[11], (DEPTH, D_MODEL), 0.05),
        'norm_ffn_pre': 1.0 + nrm(ks[12], (DEPTH, D_MODEL), 0.05),
        'w_ff_up': nrm(ks[13], (DEPTH, D_MODEL, D_FF), D_MODEL ** -0.5),
        'w_ff_down': nrm(ks[14], (DEPTH, D_FF, D_MODEL), D_FF ** -0.5),
        'norm_ffn_post': 1.0 + nrm(ks[15], (DEPTH, D_MODEL), 0.05),
    }


def reference(x_prompt, x_sample, cache_attn_k, cache_attn_v, state_retention,
              norm_mix_pre, w_in, rel_bias_table, w_attn_out, w_ret_out, w_o,
              norm_mix_post, norm_ffn_pre, w_ff_up, w_ff_down, norm_ffn_post):
    xp = x_prompt
    xs = x_sample
    S = xp.shape[1]
    T = xs.shape[1]
    pos_p = jnp.arange(S, dtype=jnp.float32)
    pos_s = PAST_LEN + jnp.arange(T, dtype=jnp.float32)
    lg = retention_log_decay()
    kp_l, vp_l, sp_l, ks_l, vs_l, ss_l = [], [], [], [], [], []
    for l in range(DEPTH):
        h = rms_norm(xp, norm_mix_pre[l])
        qa, ka, va, qr, kr, vr, gr, ga, gb = mixer_inputs(h, pos_p, w_in[l])
        att = chunk_attention_prompt(qa, ka, va, rel_bias_table[l])
        ret, s_p = retention_prompt(qr, kr, vr, lg)
        mix = mixer_output(att, ret, gr, ga, gb, w_attn_out[l], w_ret_out[l], w_o[l])
        xp = block_tail(xp, mix, norm_mix_post[l], norm_ffn_pre[l], norm_ffn_post[l], w_ff_up[l], w_ff_down[l])
        kp_l.append(ka[:, -ATT_WINDOW:])
        vp_l.append(va[:, -ATT_WINDOW:])
        sp_l.append(s_p)
        h = rms_norm(xs, norm_mix_pre[l])
        qa, ka, va, qr, kr, vr, gr, ga, gb = mixer_inputs(h, pos_s, w_in[l])
        att = chunk_attention_sample(qa, ka, va, cache_attn_k[l], cache_attn_v[l], rel_bias_table[l])
        ret, s_s = retention_block(qr, kr, vr, state_retention[l].astype(jnp.float32), lg)
        mix = mixer_output(att, ret, gr, ga, gb, w_attn_out[l], w_ret_out[l], w_o[l])
        xs = block_tail(xs, mix, norm_mix_post[l], norm_ffn_pre[l], norm_ffn_post[l], w_ff_up[l], w_ff_down[l])
        ks_l.append(ka.astype(cache_attn_k.dtype))
        vs_l.append(va.astype(cache_attn_v.dtype))
        ss_l.append(s_s.astype(state_retention.dtype))
    return (xp, xs, jnp.stack(kp_l), jnp.stack(vp_l), jnp.stack(sp_l), jnp.stack(ks_l), jnp.stack(vs_l), jnp.stack(ss_l))
```

```python
import functools
import math

import jax
import jax.numpy as jnp
import numpy as np
from jax import lax
from jax.experimental import pallas as pl
from jax.experimental.pallas import tpu as pltpu

F32 = jnp.float32
BF16 = jnp.bfloat16

D_MODEL = 1024
CHUNK = 64
LEFT_CHUNKS = 8
ATT_WINDOW = LEFT_CHUNKS * CHUNK
A_HEADS = 8
A_HEAD_DIM = 64
A_WIDTH = A_HEADS * A_HEAD_DIM
REL_CLIP = 128
R_HEADS = 4
R_DIM = 128
R_WIDTH = R_HEADS * R_DIM
ROPE_BASE = 10000.0
D_FF = 4 * D_MODEL
NORM_EPS = 1e-6
GN_EPS = 1e-6
PAST_LEN = 4096

IN_SPLITS = (A_WIDTH, A_WIDTH, A_WIDTH, R_WIDTH, R_WIDTH, R_WIDTH, R_WIDTH, D_MODEL, D_MODEL)
IN_COLS = sum(IN_SPLITS)
IN_OFFS = tuple(int(o) for o in np.cumsum((0,) + IN_SPLITS))

LANES = 128
MXU_N = 256
ROW_TILE = 512
ATT_Q_TILE = 256
ATT_KEYS = ATT_WINDOW + ATT_Q_TILE
RET_TILE = 256
FF_CHUNK = 1024
VMEM_LIMIT = 56 * 1024 * 1024
NEG_BIG = -1e30
LOG_DECAY = tuple(math.log1p(-(2.0 ** (-5 - h))) for h in range(R_HEADS))


def _resident(shape):
    nd = len(shape)
    return pl.BlockSpec(shape, lambda *_: (0,) * nd, pipeline_mode=pl.Buffered(1))


def _rms(x, g):
    ms = jnp.mean(x * x, axis=-1, keepdims=True)
    return x * lax.rsqrt(ms + NORM_EPS) * g


def _inproj_kernel(x_ref, g_ref, w_ref, cs_ref, sn_ref,
                   qa_ref, ka_ref, va_ref, qr_ref, kr_ref, vr_ref, sgr_ref, sga_ref, sgb_ref,
                   kt_ref, vt_ref, *, tail_period):
    i = pl.program_id(0)
    h = _rms(x_ref[...], g_ref[...]).astype(BF16)

    def seg(s):
        return jnp.dot(h, w_ref[:, IN_OFFS[s]:IN_OFFS[s + 1]], preferred_element_type=F32)

    is_tail = (i % tail_period) == (tail_period - 1)

    qa_ref[...] = (seg(0) * (A_HEAD_DIM ** -0.5)).astype(BF16)
    ka = seg(1)
    ka_ref[...] = ka.astype(BF16)

    @pl.when(is_tail)
    def _():
        kt_ref[...] = ka

    va = seg(2)
    va_ref[...] = va.astype(BF16)

    @pl.when(is_tail)
    def _():
        vt_ref[...] = va

    cs = cs_ref[...]
    sn = sn_ref[...]

    def rope(z, scale):
        parts = []
        for hh in range(R_HEADS):
            zh = z[:, hh * R_DIM:(hh + 1) * R_DIM]
            r = zh * cs + pltpu.roll(zh, R_DIM // 2, 1) * sn
            if scale != 1.0:
                r = r * scale
            parts.append(r.astype(BF16))
        return jnp.concatenate(parts, axis=1)

    qr_ref[...] = rope(seg(3), 1.0)
    kr_ref[...] = rope(seg(4), R_DIM ** -0.5)
    vr_ref[...] = seg(5).astype(BF16)
    gr = seg(6)
    sgr_ref[...] = (gr * jax.nn.sigmoid(gr)).astype(BF16)
    sga_ref[...] = jax.nn.sigmoid(seg(7)).astype(BF16)
    sgb_ref[...] = jax.nn.sigmoid(seg(8)).astype(BF16)


def _in_proj(x2, g, w_bf, cs, sn, tail_period):
    rows = x2.shape[0]
    n = rows // ROW_TILE
    n_pos = cs.shape[0] // ROW_TILE
    row = lambda w: pl.BlockSpec((ROW_TILE, w), lambda i: (i, 0))
    tail = pl.BlockSpec((ROW_TILE, A_WIDTH), lambda i: (i // tail_period, 0))
    pos = pl.BlockSpec((ROW_TILE, R_DIM), lambda i: (i % n_pos, 0))
    bf = lambda w: jax.ShapeDtypeStruct((rows, w), BF16)
    n_tail = rows // tail_period
    return pl.pallas_call(
        functools.partial(_inproj_kernel, tail_period=tail_period),
        grid=(n,),
        in_specs=[row(D_MODEL), _resident((1, D_MODEL)), _resident((D_MODEL, IN_COLS)), pos, pos],
        out_specs=[row(A_WIDTH)] * 3 + [row(R_WIDTH)] * 4 + [row(D_MODEL)] * 2 + [tail, tail],
        out_shape=[bf(A_WIDTH)] * 3 + [bf(R_WIDTH)] * 4 + [bf(D_MODEL)] * 2
        + [jax.ShapeDtypeStruct((n_tail, A_WIDTH), F32)] * 2,
        compiler_params=pltpu.CompilerParams(
            dimension_semantics=("arbitrary",), vmem_limit_bytes=VMEM_LIMIT),
        name="in_proj",
    )(x2, g, w_bf, cs, sn)


def _attend_heads(q_all, k_all, v_all, bias_of_head):
    tq = q_all.shape[0]
    lane_q = lax.broadcasted_iota(jnp.int32, (tq, LANES), 1)
    lane_o = lax.broadcasted_iota(jnp.int32, (tq, MXU_N), 1)
    heads_per_group = MXU_N // A_HEAD_DIM
    outs = []
    for grp in range(A_HEADS // heads_per_group):
        v4 = v_all[:, grp * MXU_N:(grp + 1) * MXU_N]
        acc = jnp.zeros((tq, MXU_N), F32)
        for hg in range(heads_per_group):
            hd = grp * heads_per_group + hg
            pair = hd // 2
            q2 = q_all[:, pair * LANES:(pair + 1) * LANES]
            k2 = k_all[:, pair * LANES:(pair + 1) * LANES]
            lo = (hd % 2) * A_HEAD_DIM
            qm = jnp.where((lane_q >= lo) & (lane_q < lo + A_HEAD_DIM), q2, jnp.zeros_like(q2))
            s = lax.dot_general(qm, k2, (((1,), (1,)), ((), ())), preferred_element_type=F32)
            s = s + bias_of_head(hd)
            m = jnp.max(s, axis=-1, keepdims=True)
            e = jnp.exp(s - m)
            denom = jnp.sum(e, axis=-1, keepdims=True)
            o = jnp.dot(e.astype(BF16), v4, preferred_element_type=F32) * (1.0 / denom)
            olo = hg * A_HEAD_DIM
            acc = jnp.where((lane_o >= olo) & (lane_o < olo + A_HEAD_DIM), o, acc)
        outs.append(acc.astype(BF16))
    return jnp.concatenate(outs, axis=1)


def _attn_prompt_kernel(q_ref, k_ref, v_ref, bm_ref, o_ref):
    j = pl.program_id(1)
    jc = jnp.minimum(j, ATT_WINDOW // ATT_Q_TILE)
    kstart = pl.multiple_of((j - jc) * ATT_Q_TILE, ATT_Q_TILE)
    c0 = ATT_WINDOW // ATT_Q_TILE - jc
    k_all = k_ref[0, pl.ds(kstart, ATT_KEYS), :]
    v_all = v_ref[0, pl.ds(kstart, ATT_KEYS), :]
    n_c = ATT_KEYS // ATT_Q_TILE

    def bias_of_head(hd):
        return jnp.concatenate([bm_ref[hd, c0 + c] for c in range(n_c)], axis=1)

    o_ref[0] = _attend_heads(q_ref[0], k_all, v_all, bias_of_head)


def _attn_prompt(q, k, v, bm):
    b, s, _ = q.shape
    blk = pl.BlockSpec((1, ATT_Q_TILE, A_WIDTH), lambda bi, j: (bi, j, 0))
    full = pl.BlockSpec((1, s, A_WIDTH), lambda bi, j: (bi, 0, 0))
    return pl.pallas_call(
        _attn_prompt_kernel,
        grid=(b, s // ATT_Q_TILE),
        in_specs=[blk, full, full, _resident(bm.shape)],
        out_specs=blk,
        out_shape=jax.ShapeDtypeStruct((b, s, A_WIDTH), BF16),
        compiler_params=pltpu.CompilerParams(
            dimension_semantics=("arbitrary", "arbitrary"), vmem_limit_bytes=VMEM_LIMIT),
        name="attn_prompt",
    )(q, k, v, bm)


def _attn_sample_kernel(q_ref, kn_ref, vn_ref, kc_ref, vc_ref, bias_ref, o_ref, *, n_pad):
    zpad = jnp.zeros((n_pad, A_WIDTH), BF16)
    k_all = jnp.concatenate([kc_ref[0].astype(BF16), kn_ref[0], zpad], axis=0)
    v_all = jnp.concatenate([vc_ref[0].astype(BF16), vn_ref[0], zpad], axis=0)
    o_ref[0] = _attend_heads(q_ref[0], k_all, v_all, lambda hd: bias_ref[hd])


def _attn_sample(q, kn, vn, kc, vc, bias):
    b, t, _ = q.shape
    l = kc.shape[1]
    nk = bias.shape[-1]
    new = pl.BlockSpec((1, t, A_WIDTH), lambda bi: (bi, 0, 0))
    cache = pl.BlockSpec((1, l, A_WIDTH), lambda bi: (bi, 0, 0))
    return pl.pallas_call(
        functools.partial(_attn_sample_kernel, n_pad=nk - l - t),
        grid=(b,),
        in_specs=[new, new, new, cache, cache, _resident(bias.shape)],
        out_specs=new,
        out_shape=jax.ShapeDtypeStruct((b, t, A_WIDTH), BF16),
        compiler_params=pltpu.CompilerParams(
            dimension_semantics=("arbitrary",), vmem_limit_bytes=VMEM_LIMIT),
        name="attn_sample",
    )(q, kn, vn, kc, vc, bias)


def _retention_kernel(q_ref, k_ref, v_ref, sg_ref, s0_ref, rn_ref, sout_ref,
                      state, dmat, rdec, kdec):
    bi = pl.program_id(0)
    j = pl.program_id(1)
    t = q_ref.shape[1]

    @pl.when((bi == 0) & (j == 0))
    def _():
        row = lax.broadcasted_iota(jnp.int32, (t, t), 0)
        col = lax.broadcasted_iota(jnp.int32, (t, t), 1)
        diff = (row - col).astype(F32)
        rowl = lax.broadcasted_iota(jnp.int32, (t, R_DIM), 0).astype(F32)
        for hh in range(R_HEADS):
            lg = LOG_DECAY[hh]
            dmat[hh] = jnp.where(diff >= 0, jnp.exp(jnp.maximum(diff, 0.0) * lg), 0.0)
            rdec[hh] = jnp.exp((rowl + 1.0) * lg)
            kdec[hh] = jnp.exp((t - 1.0 - rowl) * lg)

    @pl.when(j == 0)
    def _():
        state[...] = s0_ref[0]

    for hh in range(R_HEADS):
        sl = slice(hh * R_DIM, (hh + 1) * R_DIM)
        q = q_ref[0, :, sl]
        k = k_ref[0, :, sl]
        v = v_ref[0, :, sl]
        s_prev = state[hh]
        a = lax.dot_general(q, k, (((1,), (1,)), ((), ())), preferred_element_type=F32) * dmat[hh]
        intra = jnp.dot(a.astype(BF16), v, preferred_element_type=F32)
        cross = jnp.dot(q, s_prev.astype(BF16), preferred_element_type=F32) * rdec[hh]
        o = intra + cross
        kd = (k.astype(F32) * kdec[hh]).astype(BF16)
        upd = lax.dot_general(kd, v, (((0,), (0,)), ((), ())), preferred_element_type=F32)
        state[hh] = s_prev * math.exp(t * LOG_DECAY[hh]) + upd
        mu = jnp.mean(o, axis=-1, keepdims=True)
        d = o - mu
        var = jnp.mean(d * d, axis=-1, keepdims=True)
        rn = d * lax.rsqrt(var + GN_EPS) * sg_ref[0, :, sl].astype(F32)
        rn_ref[0, :, sl] = rn.astype(BF16)

    @pl.when(j == pl.num_programs(1) - 1)
    def _():
        sout_ref[0] = state[...]


def _retention(q, k, v, sg, s0, tile):
    b, s, _ = q.shape
    blk = pl.BlockSpec((1, tile, R_WIDTH), lambda bi, j: (bi, j, 0))
    st = pl.BlockSpec((1, R_HEADS, R_DIM, R_DIM), lambda bi, j: (bi, 0, 0, 0))
    return pl.pallas_call(
        _retention_kernel,
        grid=(b, s // tile),
        in_specs=[blk, blk, blk, blk, st],
        out_specs=[blk, st],
        out_shape=[jax.ShapeDtypeStruct((b, s, R_WIDTH), BF16),
                   jax.ShapeDtypeStruct((b, R_HEADS, R_DIM, R_DIM), F32)],
        scratch_shapes=[pltpu.VMEM((R_HEADS, R_DIM, R_DIM), F32),
                        pltpu.VMEM((R_HEADS, tile, tile), F32),
                        pltpu.VMEM((R_HEADS, tile, R_DIM), F32),
                        pltpu.VMEM((R_HEADS, tile, R_DIM), F32)],
        compiler_params=pltpu.CompilerParams(
            dimension_semantics=("arbitrary", "arbitrary"), vmem_limit_bytes=VMEM_LIMIT),
        name="retention",
    )(q, k, v, sg, s0)


def _tail_kernel(x_ref, att_ref, rn_ref, sga_ref, sgb_ref, wao_ref, wro_ref, wo_ref,
                 g1_ref, g2_ref, g3_ref, wup_ref, wdn_ref, y_ref):
    ya = jnp.dot(att_ref[...], wao_ref[...], preferred_element_type=F32)
    yr = jnp.dot(rn_ref[...], wro_ref[...], preferred_element_type=F32)
    m = sga_ref[...].astype(F32) * ya + sgb_ref[...].astype(F32) * yr
    mix = jnp.dot(m.astype(BF16), wo_ref[...], preferred_element_type=F32)
    x1 = x_ref[...] + _rms(mix, g1_ref[...])
    h2 = _rms(x1, g2_ref[...]).astype(BF16)
    f = jnp.zeros_like(x1)
    for c in range(D_FF // FF_CHUNK):
        u = jnp.dot(h2, wup_ref[:, c * FF_CHUNK:(c + 1) * FF_CHUNK], preferred_element_type=F32)
        u = jnp.square(jnp.maximum(u, 0.0)).astype(BF16)
        f = f + jnp.dot(u, wdn_ref[c * FF_CHUNK:(c + 1) * FF_CHUNK, :], preferred_element_type=F32)
    y_ref[...] = x1 + _rms(f, g3_ref[...])


def _tail(x2, att, rn, sga, sgb, wao, wro, wo, g1, g2, g3, wup, wdn):
    rows = x2.shape[0]
    row = lambda w: pl.BlockSpec((ROW_TILE, w), lambda i: (i, 0))
    gspec = _resident((1, D_MODEL))
    return pl.pallas_call(
        _tail_kernel,
        grid=(rows // ROW_TILE,),
        in_specs=[row(D_MODEL), row(A_WIDTH), row(R_WIDTH), row(D_MODEL), row(D_MODEL),
                  _resident(wao.shape), _resident(wro.shape), _resident(wo.shape),
                  gspec, gspec, gspec, _resident(wup.shape), _resident(wdn.shape)],
        out_specs=row(D_MODEL),
        out_shape=jax.ShapeDtypeStruct((rows, D_MODEL), F32),
        compiler_params=pltpu.CompilerParams(
            dimension_semantics=("arbitrary",), vmem_limit_bytes=VMEM_LIMIT),
        name="tail",
    )(x2, att, rn, sga, sgb, wao, wro, wo, g1, g2, g3, wup, wdn)


def _rope_tables(pos):
    half = R_DIM // 2
    inv = ROPE_BASE ** (-jnp.arange(half, dtype=F32) / half)
    ang = pos[:, None] * inv[None, :]
    cos, sin = jnp.cos(ang), jnp.sin(ang)
    return jnp.concatenate([cos, cos], axis=1), jnp.concatenate([-sin, sin], axis=1)


def _prompt_bias(table):
    q = np.arange(ATT_Q_TILE)[:, None]
    r = np.arange(ATT_KEYS)[None, :]
    dist = q - (r - ATT_WINDOW)
    idx = np.clip(dist, -REL_CLIP, REL_CLIP) + REL_CLIP
    qc = q // CHUNK
    kc = r // CHUNK - LEFT_CHUNKS
    valid = (kc <= qc) & (kc >= qc - LEFT_CHUNKS)
    bias = jnp.where(valid[None], table[:, idx].astype(F32), NEG_BIG)
    pad = jnp.full((A_HEADS, ATT_Q_TILE, ATT_WINDOW), NEG_BIG, F32)
    full = jnp.concatenate([bias, pad], axis=2)
    n_c = full.shape[2] // ATT_Q_TILE
    return full.reshape(A_HEADS, ATT_Q_TILE, n_c, ATT_Q_TILE).transpose(0, 2, 1, 3)


def _sample_bias(table, t, l, nk):
    dist = np.arange(t)[:, None] - (np.arange(l + t)[None, :] - l)
    idx = np.clip(dist, -REL_CLIP, REL_CLIP) + REL_CLIP
    bias = table[:, idx].astype(F32)
    pad = jnp.full((A_HEADS, t, nk - l - t), NEG_BIG, F32)
    return jnp.concatenate([bias, pad], axis=2)


def kernel(x_prompt, x_sample, cache_attn_k, cache_attn_v, state_retention, norm_mix_pre, w_in,
           rel_bias_table, w_attn_out, w_ret_out, w_o, norm_mix_post, norm_ffn_pre, w_ff_up,
           w_ff_down, norm_ffn_post):
    depth = w_in.shape[0]
    bp, sp, _ = x_prompt.shape
    bs, ts, _ = x_sample.shape
    lc = cache_attn_k.shape[2]
    n_keys_s = -(-(lc + ts) // LANES) * LANES

    cs_p, sn_p = _rope_tables(jnp.arange(sp, dtype=F32))
    pos_s = PAST_LEN + jnp.arange(ts, dtype=F32)
    cs_s, sn_s = _rope_tables(jnp.tile(pos_s, bs))

    xp = x_prompt.reshape(bp * sp, D_MODEL)
    xs = x_sample.reshape(bs * ts, D_MODEL)
    outs = [[] for _ in range(6)]
    for l in range(depth):
        w_in_b = w_in[l].astype(BF16)
        wao, wro, wo = w_attn_out[l].astype(BF16), w_ret_out[l].astype(BF16), w_o[l].astype(BF16)
        wup, wdn = w_ff_up[l].astype(BF16), w_ff_down[l].astype(BF16)
        g0 = norm_mix_pre[l].reshape(1, D_MODEL)
        g1 = norm_mix_post[l].reshape(1, D_MODEL)
        g2 = norm_ffn_pre[l].reshape(1, D_MODEL)
        g3 = norm_ffn_post[l].reshape(1, D_MODEL)
        tail_w = (wao, wro, wo, g1, g2, g3, wup, wdn)

        qa, ka, va, qr, kr, vr, sgr, sga, sgb, kt, vt = _in_proj(
            xp, g0, w_in_b, cs_p, sn_p, tail_period=sp // ROW_TILE)
        r3 = lambda a: a.reshape(bp, sp, a.shape[-1])
        att = _attn_prompt(r3(qa), r3(ka), r3(va), _prompt_bias(rel_bias_table[l]))
        rn, s_p = _retention(r3(qr), r3(kr), r3(vr), r3(sgr),
                             jnp.zeros((bp, R_HEADS, R_DIM, R_DIM), F32), RET_TILE)
        xp = _tail(xp, att.reshape(bp * sp, A_WIDTH), rn.reshape(bp * sp, R_WIDTH), sga, sgb, *tail_w)
        outs[0].append(kt.reshape(bp, ATT_WINDOW, A_HEADS, A_HEAD_DIM))
        outs[1].append(vt.reshape(bp, ATT_WINDOW, A_HEADS, A_HEAD_DIM))
        outs[2].append(s_p)

        qa, ka, va, qr, kr, vr, sgr, sga, sgb, kt, vt = _in_proj(
            xs, g0, w_in_b, cs_s, sn_s, tail_period=1)
        r3 = lambda a: a.reshape(bs, ts, a.shape[-1])
        att = _attn_sample(r3(qa), r3(ka), r3(va),
                           cache_attn_k[l].reshape(bs, lc, A_WIDTH),
                           cache_attn_v[l].reshape(bs, lc, A_WIDTH),
                           _sample_bias(rel_bias_table[l], ts, lc, n_keys_s))
        rn, s_s = _retention(r3(qr), r3(kr), r3(vr), r3(sgr),
                             state_retention[l].astype(F32), ts)
        xs = _tail(xs, att.reshape(bs * ts, A_WIDTH), rn.reshape(bs * ts, R_WIDTH), sga, sgb, *tail_w)
        outs[3].append(kt.reshape(bs, ts, A_HEADS, A_HEAD_DIM).astype(cache_attn_k.dtype))
        outs[4].append(vt.reshape(bs, ts, A_HEADS, A_HEAD_DIM).astype(cache_attn_v.dtype))
        outs[5].append(s_s.astype(state_retention.dtype))

    return (xp.reshape(bp, sp, D_MODEL), xs.reshape(bs, ts, D_MODEL),
            jnp.stack(outs[0]), jnp.stack(outs[1]), jnp.stack(outs[2]),
            jnp.stack(outs[3]), jnp.stack(outs[4]), jnp.stack(outs[5]))
```

```python
import functools
import math

import jax
import jax.numpy as jnp
import numpy as np
from jax import lax
from jax.experimental import pallas as pl
from jax.experimental.pallas import tpu as pltpu

F32 = jnp.float32
BF16 = jnp.bfloat16

D_MODEL = 1024
CHUNK = 64
LEFT_CHUNKS = 8
ATT_WINDOW = LEFT_CHUNKS * CHUNK
A_HEADS = 8
A_HEAD_DIM = 64
A_WIDTH = A_HEADS * A_HEAD_DIM
REL_CLIP = 128
R_HEADS = 4
R_DIM = 128
R_WIDTH = R_HEADS * R_DIM
ROPE_BASE = 10000.0
D_FF = 4 * D_MODEL
NORM_EPS = 1e-6
GN_EPS = 1e-6
PAST_LEN = 4096

IN_SPLITS = (A_WIDTH, A_WIDTH, A_WIDTH, R_WIDTH, R_WIDTH, R_WIDTH, R_WIDTH, D_MODEL, D_MODEL)
IN_COLS = sum(IN_SPLITS)
IN_OFFS = tuple(int(o) for o in np.cumsum((0,) + IN_SPLITS))

LANES = 128
MXU_N = 256
ROW_TILE = 512
SUB_ROWS = 256
ATT_Q_TILE = 256
ATT_KEYS = ATT_WINDOW + ATT_Q_TILE
RET_TILE = 256
FF_CHUNK = 1024
VMEM_LIMIT = 56 * 1024 * 1024
NEG_BIG = -1e30
LOG2E = math.log2(math.e)
ATT_Q_SCALE = (A_HEAD_DIM ** -0.5) * LOG2E
LOG_DECAY = tuple(math.log1p(-(2.0 ** (-5 - h))) for h in range(R_HEADS))


def _resident(shape):
    nd = len(shape)
    return pl.BlockSpec(shape, lambda *_: (0,) * nd, pipeline_mode=pl.Buffered(1))


def _rms(x, g):
    ms = jnp.mean(x * x, axis=-1, keepdims=True)
    return x * lax.rsqrt(ms + NORM_EPS) * g


def _inproj_kernel(x_ref, g_ref, w_ref, cs_ref, sn_ref,
                   qa_ref, ka_ref, va_ref, qr_ref, kr_ref, vr_ref, sgr_ref, sga_ref, sgb_ref,
                   kt_ref, vt_ref):
    for r0 in range(0, ROW_TILE, SUB_ROWS):
        rows = slice(r0, r0 + SUB_ROWS)
        h = _rms(x_ref[rows, :], g_ref[...]).astype(BF16)

        def seg(s):
            return jnp.dot(h, w_ref[:, IN_OFFS[s]:IN_OFFS[s + 1]], preferred_element_type=F32)

        cs = cs_ref[rows, :]
        sn = sn_ref[rows, :]

        def rope(z, scale):
            parts = []
            for hh in range(R_HEADS):
                zh = z[:, hh * R_DIM:(hh + 1) * R_DIM]
                r = zh * cs + pltpu.roll(zh, R_DIM // 2, 1) * sn
                if scale != 1.0:
                    r = r * scale
                parts.append(r.astype(BF16))
            return jnp.concatenate(parts, axis=1)

        qa_ref[rows, :] = (seg(0) * ATT_Q_SCALE).astype(BF16)
        ka = seg(1)
        ka_ref[rows, :] = ka.astype(BF16)
        kt_ref[rows, :] = ka
        va = seg(2)
        va_ref[rows, :] = va.astype(BF16)
        vt_ref[rows, :] = va
        qr_ref[rows, :] = rope(seg(3), 1.0)
        kr_ref[rows, :] = rope(seg(4), R_DIM ** -0.5)
        vr_ref[rows, :] = seg(5).astype(BF16)
        gr = seg(6)
        sgr_ref[rows, :] = (gr * jax.nn.sigmoid(gr)).astype(BF16)
        sga_ref[rows, :] = jax.nn.sigmoid(seg(7)).astype(BF16)
        sgb_ref[rows, :] = jax.nn.sigmoid(seg(8)).astype(BF16)


def _in_proj(x2, g, w_bf, cs, sn, tail_period):
    rows = x2.shape[0]
    n = rows // ROW_TILE
    n_pos = cs.shape[0] // ROW_TILE
    row = lambda w: pl.BlockSpec((ROW_TILE, w), lambda i: (i, 0))
    tail = pl.BlockSpec((ROW_TILE, A_WIDTH), lambda i: (i // tail_period, 0))
    pos = pl.BlockSpec((ROW_TILE, R_DIM), lambda i: (i % n_pos, 0))
    bf = lambda w: jax.ShapeDtypeStruct((rows, w), BF16)
    n_tail = rows // tail_period
    return pl.pallas_call(
        _inproj_kernel,
        grid=(n,),
        in_specs=[row(D_MODEL), _resident((1, D_MODEL)), _resident((D_MODEL, IN_COLS)), pos, pos],
        out_specs=[row(A_WIDTH)] * 3 + [row(R_WIDTH)] * 4 + [row(D_MODEL)] * 2 + [tail, tail],
        out_shape=[bf(A_WIDTH)] * 3 + [bf(R_WIDTH)] * 4 + [bf(D_MODEL)] * 2
        + [jax.ShapeDtypeStruct((n_tail, A_WIDTH), F32)] * 2,
        compiler_params=pltpu.CompilerParams(
            dimension_semantics=("arbitrary",), vmem_limit_bytes=VMEM_LIMIT),
        name="in_proj",
    )(x2, g, w_bf, cs, sn)


def _attend_heads(q_all, k_all, v_all, bias_of_head):
    tq = q_all.shape[0]
    lane_q = lax.broadcasted_iota(jnp.int32, (tq, LANES), 1)
    lane_o = lax.broadcasted_iota(jnp.int32, (tq, MXU_N), 1)
    heads_per_group = MXU_N // A_HEAD_DIM
    outs = []
    for grp in range(A_HEADS // heads_per_group):
        v4 = v_all[:, grp * MXU_N:(grp + 1) * MXU_N]
        acc = jnp.zeros((tq, MXU_N), F32)
        for hg in range(heads_per_group):
            hd = grp * heads_per_group + hg
            pair = hd // 2
            q2 = q_all[:, pair * LANES:(pair + 1) * LANES]
            k2 = k_all[:, pair * LANES:(pair + 1) * LANES]
            lo = (hd % 2) * A_HEAD_DIM
            qm = jnp.where((lane_q >= lo) & (lane_q < lo + A_HEAD_DIM), q2, jnp.zeros_like(q2))
            s = lax.dot_general(qm, k2, (((1,), (1,)), ((), ())), preferred_element_type=F32)
            s = s + bias_of_head(hd)
            m = jnp.max(s, axis=-1, keepdims=True)
            e = jnp.exp2(s - m)
            denom = jnp.sum(e, axis=-1, keepdims=True)
            o = jnp.dot(e.astype(BF16), v4, preferred_element_type=F32) * (1.0 / denom)
            olo = hg * A_HEAD_DIM
            acc = jnp.where((lane_o >= olo) & (lane_o < olo + A_HEAD_DIM), o, acc)
        outs.append(acc.astype(BF16))
    return jnp.concatenate(outs, axis=1)


def _attn_prompt_kernel(q_ref, k_ref, v_ref, bm_ref, o_ref):
    j = pl.program_id(1)
    jc = jnp.minimum(j, ATT_WINDOW // ATT_Q_TILE)
    kstart = pl.multiple_of((j - jc) * ATT_Q_TILE, ATT_Q_TILE)
    c0 = ATT_WINDOW // ATT_Q_TILE - jc
    k_all = k_ref[0, pl.ds(kstart, ATT_KEYS), :]
    v_all = v_ref[0, pl.ds(kstart, ATT_KEYS), :]
    n_c = ATT_KEYS // ATT_Q_TILE

    def bias_of_head(hd):
        return jnp.concatenate([bm_ref[hd, c0 + c] for c in range(n_c)], axis=1)

    o_ref[0] = _attend_heads(q_ref[0], k_all, v_all, bias_of_head)


def _attn_prompt(q, k, v, bm):
    b, s, _ = q.shape
    blk = pl.BlockSpec((1, ATT_Q_TILE, A_WIDTH), lambda bi, j: (bi, j, 0))
    full = pl.BlockSpec((1, s, A_WIDTH), lambda bi, j: (bi, 0, 0))
    return pl.pallas_call(
        _attn_prompt_kernel,
        grid=(b, s // ATT_Q_TILE),
        in_specs=[blk, full, full, _resident(bm.shape)],
        out_specs=blk,
        out_shape=jax.ShapeDtypeStruct((b, s, A_WIDTH), BF16),
        compiler_params=pltpu.CompilerParams(
            dimension_semantics=("arbitrary", "arbitrary"), vmem_limit_bytes=VMEM_LIMIT),
        name="attn_prompt",
    )(q, k, v, bm)


def _attn_sample_kernel(q_ref, kn_ref, vn_ref, kc_ref, vc_ref, bias_ref, o_ref, *, n_pad):
    zpad = jnp.zeros((n_pad, A_WIDTH), BF16)
    k_all = jnp.concatenate([kc_ref[0].astype(BF16), kn_ref[0], zpad], axis=0)
    v_all = jnp.concatenate([vc_ref[0].astype(BF16), vn_ref[0], zpad], axis=0)
    o_ref[0] = _attend_heads(q_ref[0], k_all, v_all, lambda hd: bias_ref[hd])


def _attn_sample(q, kn, vn, kc, vc, bias):
    b, t, _ = q.shape
    l = kc.shape[1]
    nk = bias.shape[-1]
    new = pl.BlockSpec((1, t, A_WIDTH), lambda bi: (bi, 0, 0))
    cache = pl.BlockSpec((1, l, A_WIDTH), lambda bi: (bi, 0, 0))
    return pl.pallas_call(
        functools.partial(_attn_sample_kernel, n_pad=nk - l - t),
        grid=(b,),
        in_specs=[new, new, new, cache, cache, _resident(bias.shape)],
        out_specs=new,
        out_shape=jax.ShapeDtypeStruct((b, t, A_WIDTH), BF16),
        compiler_params=pltpu.CompilerParams(
            dimension_semantics=("arbitrary",), vmem_limit_bytes=VMEM_LIMIT),
        name="attn_sample",
    )(q, kn, vn, kc, vc, bias)


def _retention_kernel(q_ref, k_ref, v_ref, sg_ref, s0_ref, rn_ref, sout_ref,
                      state, dmat, rdec, kdec):
    bi = pl.program_id(0)
    j = pl.program_id(1)
    t = q_ref.shape[1]

    @pl.when((bi == 0) & (j == 0))
    def _():
        row = lax.broadcasted_iota(jnp.int32, (t, t), 0)
        col = lax.broadcasted_iota(jnp.int32, (t, t), 1)
        diff = (row - col).astype(F32)
        rowl = lax.broadcasted_iota(jnp.int32, (t, R_DIM), 0).astype(F32)
        for hh in range(R_HEADS):
            lg = LOG_DECAY[hh]
            dmat[hh] = jnp.where(diff >= 0, jnp.exp(jnp.maximum(diff, 0.0) * lg), 0.0)
            rdec[hh] = jnp.exp((rowl + 1.0) * lg)
            kdec[hh] = jnp.exp((t - 1.0 - rowl) * lg)

    @pl.when(j == 0)
    def _():
        state[...] = s0_ref[0]

    for hh in range(R_HEADS):
        sl = slice(hh * R_DIM, (hh + 1) * R_DIM)
        q = q_ref[0, :, sl]
        k = k_ref[0, :, sl]
        v = v_ref[0, :, sl]
        s_prev = state[hh]
        a = lax.dot_general(q, k, (((1,), (1,)), ((), ())), preferred_element_type=F32) * dmat[hh]
        intra = jnp.dot(a.astype(BF16), v, preferred_element_type=F32)
        cross = jnp.dot(q, s_prev.astype(BF16), preferred_element_type=F32) * rdec[hh]
        o = intra + cross
        kd = (k.astype(F32) * kdec[hh]).astype(BF16)
        upd = lax.dot_general(kd, v, (((0,), (0,)), ((), ())), preferred_element_type=F32)
        state[hh] = s_prev * math.exp(t * LOG_DECAY[hh]) + upd
        mu = jnp.mean(o, axis=-1, keepdims=True)
        d = o - mu
        var = jnp.mean(d * d, axis=-1, keepdims=True)
        rn = d * lax.rsqrt(var + GN_EPS) * sg_ref[0, :, sl].astype(F32)
        rn_ref[0, :, sl] = rn.astype(BF16)

    @pl.when(j == pl.num_programs(1) - 1)
    def _():
        sout_ref[0] = state[...]


def _retention(q, k, v, sg, s0, tile):
    b, s, _ = q.shape
    blk = pl.BlockSpec((1, tile, R_WIDTH), lambda bi, j: (bi, j, 0))
    st = pl.BlockSpec((1, R_HEADS, R_DIM, R_DIM), lambda bi, j: (bi, 0, 0, 0))
    return pl.pallas_call(
        _retention_kernel,
        grid=(b, s // tile),
        in_specs=[blk, blk, blk, blk, st],
        out_specs=[blk, st],
        out_shape=[jax.ShapeDtypeStruct((b, s, R_WIDTH), BF16),
                   jax.ShapeDtypeStruct((b, R_HEADS, R_DIM, R_DIM), F32)],
        scratch_shapes=[pltpu.VMEM((R_HEADS, R_DIM, R_DIM), F32),
                        pltpu.VMEM((R_HEADS, tile, tile), F32),
                        pltpu.VMEM((R_HEADS, tile, R_DIM), F32),
                        pltpu.VMEM((R_HEADS, tile, R_DIM), F32)],
        compiler_params=pltpu.CompilerParams(
            dimension_semantics=("arbitrary", "arbitrary"), vmem_limit_bytes=VMEM_LIMIT),
        name="retention",
    )(q, k, v, sg, s0)


def _tail_kernel(x_ref, att_ref, rn_ref, sga_ref, sgb_ref, wao_ref, wro_ref, wo_ref,
                 g1_ref, g2_ref, g3_ref, wup_ref, wdn_ref, y_ref):
    n_sub = ROW_TILE // SUB_ROWS
    rows = [slice(r * SUB_ROWS, (r + 1) * SUB_ROWS) for r in range(n_sub)]

    def merged(r):
        ya = jnp.dot(att_ref[r, :], wao_ref[...], preferred_element_type=F32)
        yr = jnp.dot(rn_ref[r, :], wro_ref[...], preferred_element_type=F32)
        return (sga_ref[r, :].astype(F32) * ya + sgb_ref[r, :].astype(F32) * yr).astype(BF16)

    def ffn_chunk(h2, f, c):
        u = jnp.dot(h2, wup_ref[:, c * FF_CHUNK:(c + 1) * FF_CHUNK], preferred_element_type=F32)
        u = jnp.square(jnp.maximum(u, 0.0)).astype(BF16)
        return f + jnp.dot(u, wdn_ref[c * FF_CHUNK:(c + 1) * FF_CHUNK, :], preferred_element_type=F32)

    n_ff = D_FF // FF_CHUNK
    mix = [jnp.dot(merged(r), wo_ref[...], preferred_element_type=F32) for r in rows]
    x1, h2, f = [None] * n_sub, [None] * n_sub, [None] * n_sub
    for s in range(n_sub):
        x1[s] = x_ref[rows[s], :] + _rms(mix[s], g1_ref[...])
        h2[s] = _rms(x1[s], g2_ref[...]).astype(BF16)
        f[s] = jnp.zeros_like(x1[s])
        if s > 0:
            for c in range(1, n_ff):
                f[s - 1] = ffn_chunk(h2[s - 1], f[s - 1], c)
            y_ref[rows[s - 1], :] = x1[s - 1] + _rms(f[s - 1], g3_ref[...])
        f[s] = ffn_chunk(h2[s], f[s], 0)
    for c in range(1, n_ff):
        f[-1] = ffn_chunk(h2[-1], f[-1], c)
    y_ref[rows[-1], :] = x1[-1] + _rms(f[-1], g3_ref[...])


def _tail(x2, att, rn, sga, sgb, wao, wro, wo, g1, g2, g3, wup, wdn):
    rows = x2.shape[0]
    row = lambda w: pl.BlockSpec((ROW_TILE, w), lambda i: (i, 0))
    gspec = _resident((1, D_MODEL))
    return pl.pallas_call(
        _tail_kernel,
        grid=(rows // ROW_TILE,),
        in_specs=[row(D_MODEL), row(A_WIDTH), row(R_WIDTH), row(D_MODEL), row(D_MODEL),
                  _resident(wao.shape), _resident(wro.shape), _resident(wo.shape),
                  gspec, gspec, gspec, _resident(wup.shape), _resident(wdn.shape)],
        out_specs=row(D_MODEL),
        out_shape=jax.ShapeDtypeStruct((rows, D_MODEL), F32),
        compiler_params=pltpu.CompilerParams(
            dimension_semantics=("arbitrary",), vmem_limit_bytes=VMEM_LIMIT),
        name="tail",
    )(x2, att, rn, sga, sgb, wao, wro, wo, g1, g2, g3, wup, wdn)


def _rope_tables(pos):
    half = R_DIM // 2
    inv = ROPE_BASE ** (-jnp.arange(half, dtype=F32) / half)
    ang = pos[:, None] * inv[None, :]
    cos, sin = jnp.cos(ang), jnp.sin(ang)
    return jnp.concatenate([cos, cos], axis=1), jnp.concatenate([-sin, sin], axis=1)


BIAS_ROW = 1024
BIAS_TPAD = 384


def _bias_kernel(t_ref, bp_ref, bs_ref, *, t_s, nk_valid_s):
    tq = ATT_Q_TILE
    t8 = jnp.broadcast_to(t_ref[0], (8, BIAS_TPAD))
    k_idx = lax.broadcasted_iota(jnp.int32, (BIAS_TPAD, BIAS_ROW), 0)
    x_idx = lax.broadcasted_iota(jnp.int32, (BIAS_TPAD, BIAS_ROW), 1)
    src = jnp.clip(ATT_WINDOW + tq + REL_CLIP - x_idx, 0, 2 * REL_CLIP)
    sel = jnp.where(k_idx == src, 1.0, 0.0).astype(BF16)
    t1 = t8.astype(BF16)
    r1 = t8 - t1.astype(F32)
    t2 = r1.astype(BF16)
    t3 = (r1 - t2.astype(F32)).astype(BF16)
    gen = (jnp.dot(t1, sel, preferred_element_type=F32) + jnp.dot(t2, sel, preferred_element_type=F32)
           + jnp.dot(t3, sel, preferred_element_type=F32))
    rolled = pltpu.roll(jnp.broadcast_to(gen[0:1] * LOG2E, (tq, BIAS_ROW)), BIAS_ROW - tq, 1,
                        stride=1, stride_axis=0)
    qc = lax.shift_right_logical(lax.broadcasted_iota(jnp.int32, (tq, tq), 0), 6)
    col = lax.broadcasted_iota(jnp.int32, (tq, tq), 1)
    n_c = bp_ref.shape[1]
    for c in range(n_c):
        if c * tq < ATT_KEYS:
            kc = lax.shift_right_logical(col + c * tq, 6)
            valid = (kc >= qc) & (kc <= qc + LEFT_CHUNKS)
            bp_ref[0, c] = jnp.where(valid, rolled[:, c * tq:(c + 1) * tq], NEG_BIG)
        else:
            bp_ref[0, c] = jnp.full((tq, tq), NEG_BIG, F32)
    nk_s = bs_ref.shape[2]
    lane = lax.broadcasted_iota(jnp.int32, (t_s, nk_s), 1)
    bs_ref[0] = jnp.where(lane < nk_valid_s, rolled[0:t_s, 0:nk_s], NEG_BIG)


def _build_bias(table, t_s, l_s, nk_s):
    assert CHUNK == 64 and l_s == ATT_WINDOW and t_s <= ATT_Q_TILE and nk_s <= ATT_KEYS
    hn, tl = table.shape
    tpad = jnp.pad(table.astype(F32), ((0, 0), (0, BIAS_TPAD - tl))).reshape(hn, 1, BIAS_TPAD)
    n_c = (ATT_KEYS + ATT_WINDOW) // ATT_Q_TILE
    return pl.pallas_call(
        functools.partial(_bias_kernel, t_s=t_s, nk_valid_s=l_s + t_s),
        grid=(hn,),
        in_specs=[pl.BlockSpec((1, 1, BIAS_TPAD), lambda h: (h, 0, 0))],
        out_specs=[pl.BlockSpec((1, n_c, ATT_Q_TILE, ATT_Q_TILE), lambda h: (h, 0, 0, 0)),
                   pl.BlockSpec((1, t_s, nk_s), lambda h: (h, 0, 0))],
        out_shape=[jax.ShapeDtypeStruct((hn, n_c, ATT_Q_TILE, ATT_Q_TILE), F32),
                   jax.ShapeDtypeStruct((hn, t_s, nk_s), F32)],
        compiler_params=pltpu.CompilerParams(
            dimension_semantics=("arbitrary",), vmem_limit_bytes=VMEM_LIMIT),
        name="rel_bias",
    )(tpad)


def kernel(x_prompt, x_sample, cache_attn_k, cache_attn_v, state_retention, norm_mix_pre, w_in,
           rel_bias_table, w_attn_out, w_ret_out, w_o, norm_mix_post, norm_ffn_pre, w_ff_up,
           w_ff_down, norm_ffn_post):
    depth = w_in.shape[0]
    bp, sp, _ = x_prompt.shape
    bs, ts, _ = x_sample.shape
    lc = cache_attn_k.shape[2]
    n_keys_s = -(-(lc + ts) // LANES) * LANES

    cs_p, sn_p = _rope_tables(jnp.arange(sp, dtype=F32))
    pos_s = PAST_LEN + jnp.arange(ts, dtype=F32)
    cs_s, sn_s = _rope_tables(jnp.tile(pos_s, bs))

    xp = x_prompt.reshape(bp * sp, D_MODEL)
    xs = x_sample.reshape(bs * ts, D_MODEL)
    outs = [[] for _ in range(6)]
    for l in range(depth):
        w_in_b = w_in[l].astype(BF16)
        wao, wro, wo = w_attn_out[l].astype(BF16), w_ret_out[l].astype(BF16), w_o[l].astype(BF16)
        wup, wdn = w_ff_up[l].astype(BF16), w_ff_down[l].astype(BF16)
        g0 = norm_mix_pre[l].reshape(1, D_MODEL)
        g1 = norm_mix_post[l].reshape(1, D_MODEL)
        g2 = norm_ffn_pre[l].reshape(1, D_MODEL)
        g3 = norm_ffn_post[l].reshape(1, D_MODEL)
        tail_w = (wao, wro, wo, g1, g2, g3, wup, wdn)
        bias_p, bias_s = _build_bias(rel_bias_table[l], ts, lc, n_keys_s)

        qa, ka, va, qr, kr, vr, sgr, sga, sgb, kt, vt = _in_proj(
            xp, g0, w_in_b, cs_p, sn_p, tail_period=sp // ROW_TILE)
        r3 = lambda a: a.reshape(bp, sp, a.shape[-1])
        att = _attn_prompt(r3(qa), r3(ka), r3(va), bias_p)
        rn, s_p = _retention(r3(qr), r3(kr), r3(vr), r3(sgr),
                             jnp.zeros((bp, R_HEADS, R_DIM, R_DIM), F32), RET_TILE)
        xp = _tail(xp, att.reshape(bp * sp, A_WIDTH), rn.reshape(bp * sp, R_WIDTH), sga, sgb, *tail_w)
        outs[0].append(kt.reshape(bp, ATT_WINDOW, A_HEADS, A_HEAD_DIM))
        outs[1].append(vt.reshape(bp, ATT_WINDOW, A_HEADS, A_HEAD_DIM))
        outs[2].append(s_p)

        qa, ka, va, qr, kr, vr, sgr, sga, sgb, kt, vt = _in_proj(
            xs, g0, w_in_b, cs_s, sn_s, tail_period=1)
        r3 = lambda a: a.reshape(bs, ts, a.shape[-1])
        att = _attn_sample(r3(qa), r3(ka), r3(va),
                           cache_attn_k[l].reshape(bs, lc, A_WIDTH),
                           cache_attn_v[l].reshape(bs, lc, A_WIDTH),
                           bias_s)
        rn, s_s = _retention(r3(qr), r3(kr), r3(vr), r3(sgr),
                             state_retention[l].astype(F32), ts)
        xs = _tail(xs, att.reshape(bs * ts, A_WIDTH), rn.reshape(bs * ts, R_WIDTH), sga, sgb, *tail_w)
        outs[3].append(kt.reshape(bs, ts, A_HEADS, A_HEAD_DIM).astype(cache_attn_k.dtype))
        outs[4].append(vt.reshape(bs, ts, A_HEADS, A_HEAD_DIM).astype(cache_attn_v.dtype))
        outs[5].append(s_s.astype(state_retention.dtype))

    return (xp.reshape(bp, sp, D_MODEL), xs.reshape(bs, ts, D_MODEL),
            jnp.stack(outs[0]), jnp.stack(outs[1]), jnp.stack(outs[2]),
            jnp.stack(outs[3]), jnp.stack(outs[4]), jnp.stack(outs[5]))
```

```python
import functools
import math

import jax
import jax.numpy as jnp
import numpy as np
from jax import lax
from jax.experimental import pallas as pl
from jax.experimental.pallas import tpu as pltpu

F32 = jnp.float32
BF16 = jnp.bfloat16

D_MODEL = 1024
CHUNK = 64
LEFT_CHUNKS = 8
ATT_WINDOW = LEFT_CHUNKS * CHUNK
A_HEADS = 8
A_HEAD_DIM = 64
A_WIDTH = A_HEADS * A_HEAD_DIM
REL_CLIP = 128
R_HEADS = 4
R_DIM = 128
R_WIDTH = R_HEADS * R_DIM
ROPE_BASE = 10000.0
D_FF = 4 * D_MODEL
NORM_EPS = 1e-6
GN_EPS = 1e-6
PAST_LEN = 4096

IN_SPLITS = (A_WIDTH, A_WIDTH, A_WIDTH, R_WIDTH, R_WIDTH, R_WIDTH, R_WIDTH, D_MODEL, D_MODEL)
IN_COLS = sum(IN_SPLITS)
IN_OFFS = tuple(int(o) for o in np.cumsum((0,) + IN_SPLITS))

LANES = 128
MXU_N = 256
ROW_TILE = 512
SUB_ROWS = 256
ATT_Q_TILE = 256
ATT_KEYS = ATT_WINDOW + ATT_Q_TILE
ATT_QB = 4
ATT_AHEAD = 3
PV_EXTRA_ROWS = 16
RET_TILE = 256
FF_CHUNK = 1024
VMEM_LIMIT = 56 * 1024 * 1024
NEG_BIG = -1e30
LOG2E = math.log2(math.e)
ATT_Q_SCALE = (A_HEAD_DIM ** -0.5) * LOG2E
LOG_DECAY = tuple(math.log1p(-(2.0 ** (-5 - h))) for h in range(R_HEADS))


def _resident(shape):
    nd = len(shape)
    return pl.BlockSpec(shape, lambda *_: (0,) * nd, pipeline_mode=pl.Buffered(1))


def _rms(x, g):
    ms = jnp.mean(x * x, axis=-1, keepdims=True)
    return x * lax.rsqrt(ms + NORM_EPS) * g


def _inproj_kernel(x_ref, g_ref, w_ref, wvt_ref, cs_ref, sn_ref,
                   qa_ref, ka_ref, va_ref, qr_ref, kr_ref, vr_ref, sgr_ref, sga_ref, sgb_ref,
                   kt_ref, vt_ref, *, v_transposed):
    for r0 in range(0, ROW_TILE, SUB_ROWS):
        rows = slice(r0, r0 + SUB_ROWS)
        h = _rms(x_ref[rows, :], g_ref[...]).astype(BF16)

        def seg(s):
            return jnp.dot(h, w_ref[:, IN_OFFS[s]:IN_OFFS[s + 1]], preferred_element_type=F32)

        cs = cs_ref[rows, :]
        sn = sn_ref[rows, :]

        def rope(z, scale):
            parts = []
            for hh in range(R_HEADS):
                zh = z[:, hh * R_DIM:(hh + 1) * R_DIM]
                r = zh * cs + pltpu.roll(zh, R_DIM // 2, 1) * sn
                if scale != 1.0:
                    r = r * scale
                parts.append(r.astype(BF16))
            return jnp.concatenate(parts, axis=1)

        qa_ref[rows, :] = (seg(0) * ATT_Q_SCALE).astype(BF16)
        ka = seg(1)
        ka_ref[rows, :] = ka.astype(BF16)
        kt_ref[rows, :] = ka
        if v_transposed:
            va_t = lax.dot_general(wvt_ref[...], h, (((1,), (1,)), ((), ())), preferred_element_type=F32)
            va_ref[r0 // SUB_ROWS] = va_t.astype(BF16)
            vt_ref[0, :, rows] = va_t
        else:
            va = seg(2)
            va_ref[rows, :] = va.astype(BF16)
            vt_ref[rows, :] = va
        qr_ref[rows, :] = rope(seg(3), 1.0)
        kr_ref[rows, :] = rope(seg(4), R_DIM ** -0.5)
        vr_ref[rows, :] = seg(5).astype(BF16)
        gr = seg(6)
        sgr_ref[rows, :] = (gr * jax.nn.sigmoid(gr)).astype(BF16)
        sga_ref[rows, :] = jax.nn.sigmoid(seg(7)).astype(BF16)
        sgb_ref[rows, :] = jax.nn.sigmoid(seg(8)).astype(BF16)


def _in_proj(x2, g, w_bf, wvt_bf, cs, sn, tail_period, v_transposed):
    rows = x2.shape[0]
    n = rows // ROW_TILE
    n_pos = cs.shape[0] // ROW_TILE
    n_sub = ROW_TILE // SUB_ROWS
    row = lambda w: pl.BlockSpec((ROW_TILE, w), lambda i: (i, 0))
    tail = pl.BlockSpec((ROW_TILE, A_WIDTH), lambda i: (i // tail_period, 0))
    pos = pl.BlockSpec((ROW_TILE, R_DIM), lambda i: (i % n_pos, 0))
    bf = lambda w: jax.ShapeDtypeStruct((rows, w), BF16)
    n_tail = rows // tail_period
    tail_shape = jax.ShapeDtypeStruct((n_tail, A_WIDTH), F32)
    if v_transposed:
        v_spec = pl.BlockSpec((n_sub, A_WIDTH, SUB_ROWS), lambda i: (i, 0, 0))
        v_shape = jax.ShapeDtypeStruct((rows // SUB_ROWS, A_WIDTH, SUB_ROWS), BF16)
        vt_spec = pl.BlockSpec((1, A_WIDTH, ROW_TILE), lambda i: (i // tail_period, 0, 0))
        vt_shape = jax.ShapeDtypeStruct((n_tail // ROW_TILE, A_WIDTH, ROW_TILE), F32)
    else:
        v_spec, v_shape, vt_spec, vt_shape = row(A_WIDTH), bf(A_WIDTH), tail, tail_shape
    return pl.pallas_call(
        functools.partial(_inproj_kernel, v_transposed=v_transposed),
        grid=(n,),
        in_specs=[row(D_MODEL), _resident((1, D_MODEL)), _resident((D_MODEL, IN_COLS)),
                  _resident((A_WIDTH, D_MODEL)), pos, pos],
        out_specs=[row(A_WIDTH), row(A_WIDTH), v_spec] + [row(R_WIDTH)] * 4 + [row(D_MODEL)] * 2
        + [tail, vt_spec],
        out_shape=[bf(A_WIDTH), bf(A_WIDTH), v_shape] + [bf(R_WIDTH)] * 4 + [bf(D_MODEL)] * 2
        + [tail_shape, vt_shape],
        compiler_params=pltpu.CompilerParams(
            dimension_semantics=("arbitrary",), vmem_limit_bytes=VMEM_LIMIT),
        name="in_proj",
    )(x2, g, w_bf, wvt_bf, cs, sn)


def _attend_heads(q_all, k_all, v_all, bias_of_head):
    tq = q_all.shape[0]
    lane_q = lax.broadcasted_iota(jnp.int32, (tq, LANES), 1)
    lane_o = lax.broadcasted_iota(jnp.int32, (tq, MXU_N), 1)
    heads_per_group = MXU_N // A_HEAD_DIM
    outs = []
    for grp in range(A_HEADS // heads_per_group):
        v4 = v_all[:, grp * MXU_N:(grp + 1) * MXU_N]
        acc = jnp.zeros((tq, MXU_N), F32)
        for hg in range(heads_per_group):
            hd = grp * heads_per_group + hg
            pair = hd // 2
            q2 = q_all[:, pair * LANES:(pair + 1) * LANES]
            k2 = k_all[:, pair * LANES:(pair + 1) * LANES]
            lo = (hd % 2) * A_HEAD_DIM
            qm = jnp.where((lane_q >= lo) & (lane_q < lo + A_HEAD_DIM), q2, jnp.zeros_like(q2))
            s = lax.dot_general(qm, k2, (((1,), (1,)), ((), ())), preferred_element_type=F32)
            s = s + bias_of_head(hd)
            m = jnp.max(s, axis=-1, keepdims=True)
            e = jnp.exp2(s - m)
            denom = jnp.sum(e, axis=-1, keepdims=True)
            o = jnp.dot(e.astype(BF16), v4, preferred_element_type=F32) * (1.0 / denom)
            olo = hg * A_HEAD_DIM
            acc = jnp.where((lane_o >= olo) & (lane_o < olo + A_HEAD_DIM), o, acc)
        outs.append(acc.astype(BF16))
    return jnp.concatenate(outs, axis=1)


def _reduce_rows(x, op):
    parts = [x[r:r + 8] for r in range(0, x.shape[0], 8)]
    while len(parts) > 1:
        paired = [op(parts[i], parts[i + 1]) for i in range(0, len(parts) - 1, 2)]
        parts = paired + parts[len(parts) - len(parts) % 2:]
    red = jnp.max if op is jnp.maximum else jnp.sum
    return red(parts[0], axis=0, keepdims=True)


def _attn_prompt_kernel(q_ref, k_ref, vt_ref, bmt_ref, o_ref):
    n_c = ATT_KEYS // ATT_Q_TILE
    n_left = n_c - 1
    lane_q = lax.broadcasted_iota(jnp.int32, (ATT_Q_TILE, LANES), 1)
    ones_rows = jnp.ones((PV_EXTRA_ROWS, ATT_Q_TILE), BF16)
    band_keys = (LEFT_CHUNKS + 2) * CHUNK
    n_lt = ATT_Q_TILE // LANES

    def pieces(t):
        lo, hi = LANES * t, LANES * t + band_keys
        out = []
        for c in range(n_c):
            a, b = max(lo, c * ATT_Q_TILE), min(hi, (c + 1) * ATT_Q_TILE)
            if a < b:
                out.append((c, a - c * ATT_Q_TILE, b - c * ATT_Q_TILE))
        return out

    def window(qb):
        j = pl.program_id(1) * ATT_QB + qb
        kc = [jnp.maximum(j - n_left + c, 0) for c in range(n_c)]
        bc = [jnp.where(j - n_left + c >= 0, c, n_c) for c in range(n_c)]
        return kc, bc

    def scores(qb, hd):
        kc, bc = window(qb)
        pair = hd // 2
        q2 = q_ref[0, qb * ATT_Q_TILE:(qb + 1) * ATT_Q_TILE, pair * LANES:(pair + 1) * LANES]
        k2 = jnp.concatenate(
            [k_ref[0, pl.ds(pl.multiple_of(kc[c] * ATT_Q_TILE, ATT_Q_TILE), ATT_Q_TILE),
                   pair * LANES:(pair + 1) * LANES] for c in range(n_c)], axis=0)
        lo = (hd % 2) * A_HEAD_DIM
        qm = jnp.where((lane_q >= lo) & (lane_q < lo + A_HEAD_DIM), q2, jnp.zeros_like(q2))
        st = lax.dot_general(k2, qm, (((1,), (1,)), ((), ())), preferred_element_type=F32)
        tiles = []
        for t in range(n_lt):
            lanes = slice(t * LANES, (t + 1) * LANES)
            tiles.append([st[c * ATT_Q_TILE + a:c * ATT_Q_TILE + b, lanes] + bmt_ref[hd, bc[c], a:b, lanes]
                          for c, a, b in pieces(t)])
        return tiles

    def softmax(tiles):
        cols = []
        for t in range(n_lt):
            st = jnp.concatenate(tiles[t], axis=0)
            m = _reduce_rows(st, jnp.maximum)
            e = jnp.exp2(st - m).astype(BF16)
            zero = lambda n: [jnp.zeros((n, LANES), BF16)] if n else []
            cols.append(jnp.concatenate(zero(LANES * t) + [e] + zero(ATT_KEYS - band_keys - LANES * t), axis=0))
        return jnp.concatenate(cols, axis=1)

    def pv(qb, hd, pt):
        kc, _ = window(qb)
        ot = jnp.zeros((A_HEAD_DIM + PV_EXTRA_ROWS, ATT_Q_TILE), F32)
        for c in range(n_c):
            vt_c = vt_ref[0, kc[c], hd * A_HEAD_DIM:(hd + 1) * A_HEAD_DIM, :]
            ot = ot + jnp.dot(jnp.concatenate([vt_c, ones_rows], axis=0),
                              pt[c * ATT_Q_TILE:(c + 1) * ATT_Q_TILE, :], preferred_element_type=F32)
        return ot[0:A_HEAD_DIM] * (1.0 / ot[A_HEAD_DIM:A_HEAD_DIM + 1])

    items = [(qb, hd) for qb in range(ATT_QB) for hd in range(A_HEADS)]
    tiles = {it: scores(*it) for it in items[:ATT_AHEAD]}
    prev = None
    for n, (qb, hd) in enumerate(items):
        pt = softmax(tiles.pop((qb, hd)))
        if n + ATT_AHEAD < len(items):
            nxt = items[n + ATT_AHEAD]
            tiles[nxt] = scores(*nxt)
        ot = pv(qb, hd, pt)
        if hd % 2 == 0:
            prev = ot
        else:
            pair_t = jnp.concatenate([prev, ot], axis=0).T.astype(BF16)
            o_ref[0, qb * ATT_Q_TILE:(qb + 1) * ATT_Q_TILE, (hd // 2) * LANES:(hd // 2 + 1) * LANES] = pair_t


def _attn_prompt(q, k, vt, bmt):
    b, s, _ = q.shape
    assert vt.shape == (b, s // ATT_Q_TILE, A_WIDTH, ATT_Q_TILE)
    blk = pl.BlockSpec((1, ATT_QB * ATT_Q_TILE, A_WIDTH), lambda bi, j: (bi, j, 0))
    full = pl.BlockSpec((1, s, A_WIDTH), lambda bi, j: (bi, 0, 0))
    full_t = pl.BlockSpec((1,) + vt.shape[1:], lambda bi, j: (bi, 0, 0, 0))
    return pl.pallas_call(
        _attn_prompt_kernel,
        grid=(b, s // (ATT_QB * ATT_Q_TILE)),
        in_specs=[blk, full, full_t, _resident(bmt.shape)],
        out_specs=blk,
        out_shape=jax.ShapeDtypeStruct((b, s, A_WIDTH), BF16),
        compiler_params=pltpu.CompilerParams(
            dimension_semantics=("arbitrary", "arbitrary"), vmem_limit_bytes=VMEM_LIMIT),
        name="attn_prompt",
    )(q, k, vt, bmt)


def _attn_sample_kernel(q_ref, kn_ref, vn_ref, kc_ref, vc_ref, bias_ref, o_ref, *, n_pad):
    zpad = jnp.zeros((n_pad, A_WIDTH), BF16)
    k_all = jnp.concatenate([kc_ref[0].astype(BF16), kn_ref[0], zpad], axis=0)
    v_all = jnp.concatenate([vc_ref[0].astype(BF16), vn_ref[0], zpad], axis=0)
    o_ref[0] = _attend_heads(q_ref[0], k_all, v_all, lambda hd: bias_ref[hd])


def _attn_sample(q, kn, vn, kc, vc, bias):
    b, t, _ = q.shape
    l = kc.shape[1]
    nk = bias.shape[-1]
    new = pl.BlockSpec((1, t, A_WIDTH), lambda bi: (bi, 0, 0))
    cache = pl.BlockSpec((1, l, A_WIDTH), lambda bi: (bi, 0, 0))
    return pl.pallas_call(
        functools.partial(_attn_sample_kernel, n_pad=nk - l - t),
        grid=(b,),
        in_specs=[new, new, new, cache, cache, _resident(bias.shape)],
        out_specs=new,
        out_shape=jax.ShapeDtypeStruct((b, t, A_WIDTH), BF16),
        compiler_params=pltpu.CompilerParams(
            dimension_semantics=("arbitrary",), vmem_limit_bytes=VMEM_LIMIT),
        name="attn_sample",
    )(q, kn, vn, kc, vc, bias)


def _retention_kernel(q_ref, k_ref, v_ref, sg_ref, s0_ref, rn_ref, sout_ref,
                      state, dmat, rdec, kdec):
    bi = pl.program_id(0)
    j = pl.program_id(1)
    t = q_ref.shape[1]

    @pl.when((bi == 0) & (j == 0))
    def _():
        row = lax.broadcasted_iota(jnp.int32, (t, t), 0)
        col = lax.broadcasted_iota(jnp.int32, (t, t), 1)
        diff = (row - col).astype(F32)
        rowl = lax.broadcasted_iota(jnp.int32, (t, R_DIM), 0).astype(F32)
        for hh in range(R_HEADS):
            lg = LOG_DECAY[hh]
            dmat[hh] = jnp.where(diff >= 0, jnp.exp(jnp.maximum(diff, 0.0) * lg), 0.0)
            rdec[hh] = jnp.exp((rowl + 1.0) * lg)
            kdec[hh] = jnp.exp((t - 1.0 - rowl) * lg)

    @pl.when(j == 0)
    def _():
        state[...] = s0_ref[0]

    for hh in range(R_HEADS):
        sl = slice(hh * R_DIM, (hh + 1) * R_DIM)
        q = q_ref[0, :, sl]
        k = k_ref[0, :, sl]
        v = v_ref[0, :, sl]
        s_prev = state[hh]
        a = lax.dot_general(q, k, (((1,), (1,)), ((), ())), preferred_element_type=F32) * dmat[hh]
        intra = jnp.dot(a.astype(BF16), v, preferred_element_type=F32)
        cross = jnp.dot(q, s_prev.astype(BF16), preferred_element_type=F32) * rdec[hh]
        o = intra + cross
        kd = (k.astype(F32) * kdec[hh]).astype(BF16)
        upd = lax.dot_general(kd, v, (((0,), (0,)), ((), ())), preferred_element_type=F32)
        state[hh] = s_prev * math.exp(t * LOG_DECAY[hh]) + upd
        mu = jnp.mean(o, axis=-1, keepdims=True)
        d = o - mu
        var = jnp.mean(d * d, axis=-1, keepdims=True)
        rn = d * lax.rsqrt(var + GN_EPS) * sg_ref[0, :, sl].astype(F32)
        rn_ref[0, :, sl] = rn.astype(BF16)

    @pl.when(j == pl.num_programs(1) - 1)
    def _():
        sout_ref[0] = state[...]


def _retention(q, k, v, sg, s0, tile):
    b, s, _ = q.shape
    blk = pl.BlockSpec((1, tile, R_WIDTH), lambda bi, j: (bi, j, 0))
    st = pl.BlockSpec((1, R_HEADS, R_DIM, R_DIM), lambda bi, j: (bi, 0, 0, 0))
    return pl.pallas_call(
        _retention_kernel,
        grid=(b, s // tile),
        in_specs=[blk, blk, blk, blk, st],
        out_specs=[blk, st],
        out_shape=[jax.ShapeDtypeStruct((b, s, R_WIDTH), BF16),
                   jax.ShapeDtypeStruct((b, R_HEADS, R_DIM, R_DIM), F32)],
        scratch_shapes=[pltpu.VMEM((R_HEADS, R_DIM, R_DIM), F32),
                        pltpu.VMEM((R_HEADS, tile, tile), F32),
                        pltpu.VMEM((R_HEADS, tile, R_DIM), F32),
                        pltpu.VMEM((R_HEADS, tile, R_DIM), F32)],
        compiler_params=pltpu.CompilerParams(
            dimension_semantics=("arbitrary", "arbitrary"), vmem_limit_bytes=VMEM_LIMIT),
        name="retention",
    )(q, k, v, sg, s0)


def _tail_kernel(x_ref, att_ref, rn_ref, sga_ref, sgb_ref, wao_ref, wro_ref, wo_ref,
                 g1_ref, g2_ref, g3_ref, wup_ref, wdn_ref, y_ref):
    n_sub = ROW_TILE // SUB_ROWS
    rows = [slice(r * SUB_ROWS, (r + 1) * SUB_ROWS) for r in range(n_sub)]

    def merged(r):
        ya = jnp.dot(att_ref[r, :], wao_ref[...], preferred_element_type=F32)
        yr = jnp.dot(rn_ref[r, :], wro_ref[...], preferred_element_type=F32)
        return (sga_ref[r, :].astype(F32) * ya + sgb_ref[r, :].astype(F32) * yr).astype(BF16)

    def ffn_chunk(h2, f, c):
        u = jnp.dot(h2, wup_ref[:, c * FF_CHUNK:(c + 1) * FF_CHUNK], preferred_element_type=F32)
        u = jnp.square(jnp.maximum(u, 0.0)).astype(BF16)
        return f + jnp.dot(u, wdn_ref[c * FF_CHUNK:(c + 1) * FF_CHUNK, :], preferred_element_type=F32)

    n_ff = D_FF // FF_CHUNK
    mix = [jnp.dot(merged(r), wo_ref[...], preferred_element_type=F32) for r in rows]
    x1, h2, f = [None] * n_sub, [None] * n_sub, [None] * n_sub
    for s in range(n_sub):
        x1[s] = x_ref[rows[s], :] + _rms(mix[s], g1_ref[...])
        h2[s] = _rms(x1[s], g2_ref[...]).astype(BF16)
        f[s] = jnp.zeros_like(x1[s])
        if s > 0:
            for c in range(1, n_ff):
                f[s - 1] = ffn_chunk(h2[s - 1], f[s - 1], c)
            y_ref[rows[s - 1], :] = x1[s - 1] + _rms(f[s - 1], g3_ref[...])
        f[s] = ffn_chunk(h2[s], f[s], 0)
    for c in range(1, n_ff):
        f[-1] = ffn_chunk(h2[-1], f[-1], c)
    y_ref[rows[-1], :] = x1[-1] + _rms(f[-1], g3_ref[...])


def _tail(x2, att, rn, sga, sgb, wao, wro, wo, g1, g2, g3, wup, wdn):
    rows = x2.shape[0]
    row = lambda w: pl.BlockSpec((ROW_TILE, w), lambda i: (i, 0))
    gspec = _resident((1, D_MODEL))
    return pl.pallas_call(
        _tail_kernel,
        grid=(rows // ROW_TILE,),
        in_specs=[row(D_MODEL), row(A_WIDTH), row(R_WIDTH), row(D_MODEL), row(D_MODEL),
                  _resident(wao.shape), _resident(wro.shape), _resident(wo.shape),
                  gspec, gspec, gspec, _resident(wup.shape), _resident(wdn.shape)],
        out_specs=row(D_MODEL),
        out_shape=jax.ShapeDtypeStruct((rows, D_MODEL), F32),
        compiler_params=pltpu.CompilerParams(
            dimension_semantics=("arbitrary",), vmem_limit_bytes=VMEM_LIMIT),
        name="tail",
    )(x2, att, rn, sga, sgb, wao, wro, wo, g1, g2, g3, wup, wdn)


def _rope_tables(pos):
    half = R_DIM // 2
    inv = ROPE_BASE ** (-jnp.arange(half, dtype=F32) / half)
    ang = pos[:, None] * inv[None, :]
    cos, sin = jnp.cos(ang), jnp.sin(ang)
    return jnp.concatenate([cos, cos], axis=1), jnp.concatenate([-sin, sin], axis=1)


BIAS_ROW = 1024
BIAS_TPAD = 384


def _bias_kernel(t_ref, bpt_ref, bs_ref, *, t_s, nk_valid_s):
    tq = ATT_Q_TILE
    t8 = jnp.broadcast_to(t_ref[0], (8, BIAS_TPAD))
    t1 = t8.astype(BF16)
    r1 = t8 - t1.astype(F32)
    t2 = r1.astype(BF16)
    t3 = (r1 - t2.astype(F32)).astype(BF16)

    def generator(width, offset, sign):
        k_idx = lax.broadcasted_iota(jnp.int32, (BIAS_TPAD, width), 0)
        x_idx = lax.broadcasted_iota(jnp.int32, (BIAS_TPAD, width), 1)
        sel = jnp.where(k_idx == jnp.clip(offset + sign * x_idx, 0, 2 * REL_CLIP), 1.0, 0.0).astype(BF16)
        gen = (jnp.dot(t1, sel, preferred_element_type=F32) + jnp.dot(t2, sel, preferred_element_type=F32)
               + jnp.dot(t3, sel, preferred_element_type=F32))
        return gen[0:1] * LOG2E

    def toeplitz(gen, rows):
        width = gen.shape[1]
        return pltpu.roll(jnp.broadcast_to(gen, (rows, width)), width - tq, 1, stride=1, stride_axis=0)

    qc = lax.shift_right_logical(lax.broadcasted_iota(jnp.int32, (tq, tq), 1), 6)
    key = lax.broadcasted_iota(jnp.int32, (tq, tq), 0)
    for c in range(bpt_ref.shape[1]):
        if c * tq < ATT_KEYS:
            gen = generator(2 * tq, ATT_WINDOW + REL_CLIP - tq - tq * c, 1)
            kc = lax.shift_right_logical(key + c * tq, 6)
            valid = (kc >= qc) & (kc <= qc + LEFT_CHUNKS)
            bpt_ref[0, c] = jnp.where(valid, toeplitz(gen, tq)[:, 0:tq], NEG_BIG)
        else:
            bpt_ref[0, c] = jnp.full((tq, tq), NEG_BIG, F32)
    nk_s = bs_ref.shape[2]
    gen_s = generator(BIAS_ROW, ATT_WINDOW + tq + REL_CLIP, -1)
    lane = lax.broadcasted_iota(jnp.int32, (t_s, nk_s), 1)
    bs_ref[0] = jnp.where(lane < nk_valid_s, toeplitz(gen_s, t_s)[:, 0:nk_s], NEG_BIG)


def _build_bias(table, t_s, l_s, nk_s):
    assert CHUNK == 64 and l_s == ATT_WINDOW and t_s <= ATT_Q_TILE and nk_s <= ATT_KEYS
    hn, tl = table.shape
    tpad = jnp.pad(table.astype(F32), ((0, 0), (0, BIAS_TPAD - tl))).reshape(hn, 1, BIAS_TPAD)
    n_c = ATT_KEYS // ATT_Q_TILE + 1
    return pl.pallas_call(
        functools.partial(_bias_kernel, t_s=t_s, nk_valid_s=l_s + t_s),
        grid=(hn,),
        in_specs=[pl.BlockSpec((1, 1, BIAS_TPAD), lambda h: (h, 0, 0))],
        out_specs=[pl.BlockSpec((1, n_c, ATT_Q_TILE, ATT_Q_TILE), lambda h: (h, 0, 0, 0)),
                   pl.BlockSpec((1, t_s, nk_s), lambda h: (h, 0, 0))],
        out_shape=[jax.ShapeDtypeStruct((hn, n_c, ATT_Q_TILE, ATT_Q_TILE), F32),
                   jax.ShapeDtypeStruct((hn, t_s, nk_s), F32)],
        compiler_params=pltpu.CompilerParams(
            dimension_semantics=("arbitrary",), vmem_limit_bytes=VMEM_LIMIT),
        name="rel_bias",
    )(tpad)


def kernel(x_prompt, x_sample, cache_attn_k, cache_attn_v, state_retention, norm_mix_pre, w_in,
           rel_bias_table, w_attn_out, w_ret_out, w_o, norm_mix_post, norm_ffn_pre, w_ff_up,
           w_ff_down, norm_ffn_post):
    depth = w_in.shape[0]
    bp, sp, _ = x_prompt.shape
    bs, ts, _ = x_sample.shape
    lc = cache_attn_k.shape[2]
    n_keys_s = -(-(lc + ts) // LANES) * LANES

    cs_p, sn_p = _rope_tables(jnp.arange(sp, dtype=F32))
    pos_s = PAST_LEN + jnp.arange(ts, dtype=F32)
    cs_s, sn_s = _rope_tables(jnp.tile(pos_s, bs))

    xp = x_prompt.reshape(bp * sp, D_MODEL)
    xs = x_sample.reshape(bs * ts, D_MODEL)
    outs = [[] for _ in range(6)]
    for l in range(depth):
        w_in_b = w_in[l].astype(BF16)
        w_v_t = w_in_b[:, IN_OFFS[2]:IN_OFFS[3]].T
        wao, wro, wo = w_attn_out[l].astype(BF16), w_ret_out[l].astype(BF16), w_o[l].astype(BF16)
        wup, wdn = w_ff_up[l].astype(BF16), w_ff_down[l].astype(BF16)
        g0 = norm_mix_pre[l].reshape(1, D_MODEL)
        g1 = norm_mix_post[l].reshape(1, D_MODEL)
        g2 = norm_ffn_pre[l].reshape(1, D_MODEL)
        g3 = norm_ffn_post[l].reshape(1, D_MODEL)
        tail_w = (wao, wro, wo, g1, g2, g3, wup, wdn)
        bias_p, bias_s = _build_bias(rel_bias_table[l], ts, lc, n_keys_s)

        qa, ka, va_t, qr, kr, vr, sgr, sga, sgb, kt, vt_t = _in_proj(
            xp, g0, w_in_b, w_v_t, cs_p, sn_p, tail_period=sp // ROW_TILE, v_transposed=True)
        r3 = lambda a: a.reshape(bp, sp, a.shape[-1])
        att = _attn_prompt(r3(qa), r3(ka), va_t.reshape(bp, sp // SUB_ROWS, A_WIDTH, SUB_ROWS), bias_p)
        rn, s_p = _retention(r3(qr), r3(kr), r3(vr), r3(sgr),
                             jnp.zeros((bp, R_HEADS, R_DIM, R_DIM), F32), RET_TILE)
        xp = _tail(xp, att.reshape(bp * sp, A_WIDTH), rn.reshape(bp * sp, R_WIDTH), sga, sgb, *tail_w)
        outs[0].append(kt.reshape(bp, ATT_WINDOW, A_HEADS, A_HEAD_DIM))
        outs[1].append(vt_t.transpose(0, 2, 1).reshape(bp, ATT_WINDOW, A_HEADS, A_HEAD_DIM))
        outs[2].append(s_p)

        qa, ka, va, qr, kr, vr, sgr, sga, sgb, kt, vt = _in_proj(
            xs, g0, w_in_b, w_v_t, cs_s, sn_s, tail_period=1, v_transposed=False)
        r3 = lambda a: a.reshape(bs, ts, a.shape[-1])
        att = _attn_sample(r3(qa), r3(ka), r3(va),
                           cache_attn_k[l].reshape(bs, lc, A_WIDTH),
                           cache_attn_v[l].reshape(bs, lc, A_WIDTH),
                           bias_s)
        rn, s_s = _retention(r3(qr), r3(kr), r3(vr), r3(sgr),
                             state_retention[l].astype(F32), ts)
        xs = _tail(xs, att.reshape(bs * ts, A_WIDTH), rn.reshape(bs * ts, R_WIDTH), sga, sgb, *tail_w)
        outs[3].append(kt.reshape(bs, ts, A_HEADS, A_HEAD_DIM).astype(cache_attn_k.dtype))
        outs[4].append(vt.reshape(bs, ts, A_HEADS, A_HEAD_DIM).astype(cache_attn_v.dtype))
        outs[5].append(s_s.astype(state_retention.dtype))

    return (xp.reshape(bp, sp, D_MODEL), xs.reshape(bs, ts, D_MODEL),
            jnp.stack(outs[0]), jnp.stack(outs[1]), jnp.stack(outs[2]),
            jnp.stack(outs[3]), jnp.stack(outs[4]), jnp.stack(outs[5]))
```

```python
import functools
import math

import jax
import jax.numpy as jnp
import numpy as np
from jax import lax
from jax.experimental import pallas as pl
from jax.experimental.pallas import tpu as pltpu

F32 = jnp.float32
BF16 = jnp.bfloat16

D_MODEL = 1024
CHUNK = 64
LEFT_CHUNKS = 8
ATT_WINDOW = LEFT_CHUNKS * CHUNK
A_HEADS = 8
A_HEAD_DIM = 64
A_WIDTH = A_HEADS * A_HEAD_DIM
REL_CLIP = 128
R_HEADS = 4
R_DIM = 128
R_WIDTH = R_HEADS * R_DIM
ROPE_BASE = 10000.0
D_FF = 4 * D_MODEL
NORM_EPS = 1e-6
GN_EPS = 1e-6
PAST_LEN = 4096

IN_SPLITS = (A_WIDTH, A_WIDTH, A_WIDTH, R_WIDTH, R_WIDTH, R_WIDTH, R_WIDTH, D_MODEL, D_MODEL)
IN_COLS = sum(IN_SPLITS)
IN_OFFS = tuple(int(o) for o in np.cumsum((0,) + IN_SPLITS))

LANES = 128
MXU_N = 256
ROW_TILE = 512
SUB_ROWS = 256
ATT_Q_TILE = 256
ATT_KEYS = ATT_WINDOW + ATT_Q_TILE
ATT_QB = 4
ATT_AHEAD = 3
PV_EXTRA_ROWS = 16
RET_TILE = 256
RET_BLOCKS = 4
FF_CHUNK = 1024
VMEM_LIMIT = 56 * 1024 * 1024
NEG_BIG = -1e30
LOG2E = math.log2(math.e)
ATT_Q_SCALE = (A_HEAD_DIM ** -0.5) * LOG2E
LOG_DECAY = tuple(math.log1p(-(2.0 ** (-5 - h))) for h in range(R_HEADS))


def _resident(shape):
    nd = len(shape)
    return pl.BlockSpec(shape, lambda *_: (0,) * nd, pipeline_mode=pl.Buffered(1))


def _rms(x, g):
    ms = jnp.mean(x * x, axis=-1, keepdims=True)
    return x * lax.rsqrt(ms + NORM_EPS) * g


def _inproj_kernel(x_ref, g_ref, w_ref, wvt_ref, cs_ref, sn_ref,
                   qa_ref, ka_ref, va_ref, qr_ref, kr_ref, vr_ref, sgr_ref, sga_ref, sgb_ref,
                   kt_ref, vt_ref, *, v_transposed):
    for r0 in range(0, ROW_TILE, SUB_ROWS):
        rows = slice(r0, r0 + SUB_ROWS)
        h = _rms(x_ref[rows, :], g_ref[...]).astype(BF16)

        def seg(s):
            return jnp.dot(h, w_ref[:, IN_OFFS[s]:IN_OFFS[s + 1]], preferred_element_type=F32)

        cs = cs_ref[rows, :]
        sn = sn_ref[rows, :]

        def rope(z, scale):
            parts = []
            for hh in range(R_HEADS):
                zh = z[:, hh * R_DIM:(hh + 1) * R_DIM]
                r = zh * cs + pltpu.roll(zh, R_DIM // 2, 1) * sn
                if scale != 1.0:
                    r = r * scale
                parts.append(r.astype(BF16))
            return jnp.concatenate(parts, axis=1)

        qa_ref[rows, :] = (seg(0) * ATT_Q_SCALE).astype(BF16)
        ka = seg(1)
        ka_ref[rows, :] = ka.astype(BF16)
        kt_ref[rows, :] = ka
        if v_transposed:
            va_t = lax.dot_general(wvt_ref[...], h, (((1,), (1,)), ((), ())), preferred_element_type=F32)
            va_ref[r0 // SUB_ROWS] = va_t.astype(BF16)
            vt_ref[0, :, rows] = va_t
        else:
            va = seg(2)
            va_ref[rows, :] = va.astype(BF16)
            vt_ref[rows, :] = va
        qr_ref[rows, :] = rope(seg(3), 1.0)
        kr_ref[rows, :] = rope(seg(4), R_DIM ** -0.5)
        vr_ref[rows, :] = seg(5).astype(BF16)
        gr = seg(6)
        sgr_ref[rows, :] = (gr * jax.nn.sigmoid(gr)).astype(BF16)
        sga_ref[rows, :] = jax.nn.sigmoid(seg(7)).astype(BF16)
        sgb_ref[rows, :] = jax.nn.sigmoid(seg(8)).astype(BF16)


def _w_v_transposed(w_in, l):
    assert IN_OFFS[2] % A_WIDTH == 0

    def body(w_ref, o_ref):
        o_ref[...] = w_ref[0].T.astype(BF16)

    return pl.pallas_call(
        body,
        grid=(1,),
        in_specs=[pl.BlockSpec((1, D_MODEL, A_WIDTH), lambda i: (l, 0, IN_OFFS[2] // A_WIDTH))],
        out_specs=pl.BlockSpec((A_WIDTH, D_MODEL), lambda i: (0, 0)),
        out_shape=jax.ShapeDtypeStruct((A_WIDTH, D_MODEL), BF16),
        name="w_v_transpose",
    )(w_in)


def _in_proj(x2, g, w_bf, wvt_bf, cs, sn, tail_period, v_transposed):
    rows = x2.shape[0]
    n = rows // ROW_TILE
    n_pos = cs.shape[0] // ROW_TILE
    n_sub = ROW_TILE // SUB_ROWS
    row = lambda w: pl.BlockSpec((ROW_TILE, w), lambda i: (i, 0))
    tail = pl.BlockSpec((ROW_TILE, A_WIDTH), lambda i: (i // tail_period, 0))
    pos = pl.BlockSpec((ROW_TILE, R_DIM), lambda i: (i % n_pos, 0))
    bf = lambda w: jax.ShapeDtypeStruct((rows, w), BF16)
    n_tail = rows // tail_period
    tail_shape = jax.ShapeDtypeStruct((n_tail, A_WIDTH), F32)
    if v_transposed:
        v_spec = pl.BlockSpec((n_sub, A_WIDTH, SUB_ROWS), lambda i: (i, 0, 0))
        v_shape = jax.ShapeDtypeStruct((rows // SUB_ROWS, A_WIDTH, SUB_ROWS), BF16)
        vt_spec = pl.BlockSpec((1, A_WIDTH, ROW_TILE), lambda i: (i // tail_period, 0, 0))
        vt_shape = jax.ShapeDtypeStruct((n_tail // ROW_TILE, A_WIDTH, ROW_TILE), F32)
    else:
        v_spec, v_shape, vt_spec, vt_shape = row(A_WIDTH), bf(A_WIDTH), tail, tail_shape
    return pl.pallas_call(
        functools.partial(_inproj_kernel, v_transposed=v_transposed),
        grid=(n,),
        in_specs=[row(D_MODEL), _resident((1, D_MODEL)), _resident((D_MODEL, IN_COLS)),
                  _resident((A_WIDTH, D_MODEL)), pos, pos],
        out_specs=[row(A_WIDTH), row(A_WIDTH), v_spec] + [row(R_WIDTH)] * 4 + [row(D_MODEL)] * 2
        + [tail, vt_spec],
        out_shape=[bf(A_WIDTH), bf(A_WIDTH), v_shape] + [bf(R_WIDTH)] * 4 + [bf(D_MODEL)] * 2
        + [tail_shape, vt_shape],
        compiler_params=pltpu.CompilerParams(
            dimension_semantics=("arbitrary",), vmem_limit_bytes=VMEM_LIMIT),
        name="in_proj",
    )(x2, g, w_bf, wvt_bf, cs, sn)


def _attend_heads(q_all, k_all, v_all, bias_of_head):
    tq = q_all.shape[0]
    lane_q = lax.broadcasted_iota(jnp.int32, (tq, LANES), 1)
    lane_o = lax.broadcasted_iota(jnp.int32, (tq, MXU_N), 1)
    heads_per_group = MXU_N // A_HEAD_DIM
    outs = []
    for grp in range(A_HEADS // heads_per_group):
        v4 = v_all[:, grp * MXU_N:(grp + 1) * MXU_N]
        acc = jnp.zeros((tq, MXU_N), F32)
        for hg in range(heads_per_group):
            hd = grp * heads_per_group + hg
            pair = hd // 2
            q2 = q_all[:, pair * LANES:(pair + 1) * LANES]
            k2 = k_all[:, pair * LANES:(pair + 1) * LANES]
            lo = (hd % 2) * A_HEAD_DIM
            qm = jnp.where((lane_q >= lo) & (lane_q < lo + A_HEAD_DIM), q2, jnp.zeros_like(q2))
            s = lax.dot_general(qm, k2, (((1,), (1,)), ((), ())), preferred_element_type=F32)
            s = s + bias_of_head(hd)
            m = jnp.max(s, axis=-1, keepdims=True)
            e = jnp.exp2(s - m)
            denom = jnp.sum(e, axis=-1, keepdims=True)
            o = jnp.dot(e.astype(BF16), v4, preferred_element_type=F32) * (1.0 / denom)
            olo = hg * A_HEAD_DIM
            acc = jnp.where((lane_o >= olo) & (lane_o < olo + A_HEAD_DIM), o, acc)
        outs.append(acc.astype(BF16))
    return jnp.concatenate(outs, axis=1)


def _reduce_rows(x, op):
    parts = [x[r:r + 8] for r in range(0, x.shape[0], 8)]
    while len(parts) > 1:
        paired = [op(parts[i], parts[i + 1]) for i in range(0, len(parts) - 1, 2)]
        parts = paired + parts[len(parts) - len(parts) % 2:]
    red = jnp.max if op is jnp.maximum else jnp.sum
    return red(parts[0], axis=0, keepdims=True)


def _attn_prompt_kernel(q_ref, k_ref, vt_ref, bmt_ref, o_ref):
    n_c = ATT_KEYS // ATT_Q_TILE
    n_left = n_c - 1
    lane_q = lax.broadcasted_iota(jnp.int32, (ATT_Q_TILE, LANES), 1)
    ones_rows = jnp.ones((PV_EXTRA_ROWS, ATT_Q_TILE), BF16)
    band_keys = (LEFT_CHUNKS + 2) * CHUNK
    n_lt = ATT_Q_TILE // LANES

    def pieces(t):
        lo, hi = LANES * t, LANES * t + band_keys
        out = []
        for c in range(n_c):
            a, b = max(lo, c * ATT_Q_TILE), min(hi, (c + 1) * ATT_Q_TILE)
            if a < b:
                out.append((c, a - c * ATT_Q_TILE, b - c * ATT_Q_TILE))
        return out

    def window(qb):
        j = pl.program_id(1) * ATT_QB + qb
        kc = [jnp.maximum(j - n_left + c, 0) for c in range(n_c)]
        bc = [jnp.where(j - n_left + c >= 0, c, n_c) for c in range(n_c)]
        return kc, bc

    def scores(qb, hd):
        kc, bc = window(qb)
        pair = hd // 2
        q2 = q_ref[0, qb * ATT_Q_TILE:(qb + 1) * ATT_Q_TILE, pair * LANES:(pair + 1) * LANES]
        k2 = jnp.concatenate(
            [k_ref[0, pl.ds(pl.multiple_of(kc[c] * ATT_Q_TILE, ATT_Q_TILE), ATT_Q_TILE),
                   pair * LANES:(pair + 1) * LANES] for c in range(n_c)], axis=0)
        lo = (hd % 2) * A_HEAD_DIM
        qm = jnp.where((lane_q >= lo) & (lane_q < lo + A_HEAD_DIM), q2, jnp.zeros_like(q2))
        st = lax.dot_general(k2, qm, (((1,), (1,)), ((), ())), preferred_element_type=F32)
        tiles = []
        for t in range(n_lt):
            lanes = slice(t * LANES, (t + 1) * LANES)
            tiles.append([st[c * ATT_Q_TILE + a:c * ATT_Q_TILE + b, lanes] + bmt_ref[hd, bc[c], a:b, lanes]
                          for c, a, b in pieces(t)])
        return tiles

    def softmax(tiles):
        cols = []
        for t in range(n_lt):
            st = jnp.concatenate(tiles[t], axis=0)
            m = _reduce_rows(st, jnp.maximum)
            e = jnp.exp2(st - m).astype(BF16)
            zero = lambda n: [jnp.zeros((n, LANES), BF16)] if n else []
            cols.append(jnp.concatenate(zero(LANES * t) + [e] + zero(ATT_KEYS - band_keys - LANES * t), axis=0))
        return jnp.concatenate(cols, axis=1)

    def pv(qb, hd, pt):
        kc, _ = window(qb)
        ot = jnp.zeros((A_HEAD_DIM + PV_EXTRA_ROWS, ATT_Q_TILE), F32)
        for c in range(n_c):
            vt_c = vt_ref[0, kc[c], hd * A_HEAD_DIM:(hd + 1) * A_HEAD_DIM, :]
            ot = ot + jnp.dot(jnp.concatenate([vt_c, ones_rows], axis=0),
                              pt[c * ATT_Q_TILE:(c + 1) * ATT_Q_TILE, :], preferred_element_type=F32)
        return ot[0:A_HEAD_DIM] * (1.0 / ot[A_HEAD_DIM:A_HEAD_DIM + 1])

    items = [(qb, hd) for qb in range(ATT_QB) for hd in range(A_HEADS)]
    tiles = {it: scores(*it) for it in items[:ATT_AHEAD]}
    prev = None
    for n, (qb, hd) in enumerate(items):
        pt = softmax(tiles.pop((qb, hd)))
        if n + ATT_AHEAD < len(items):
            nxt = items[n + ATT_AHEAD]
            tiles[nxt] = scores(*nxt)
        ot = pv(qb, hd, pt)
        if hd % 2 == 0:
            prev = ot
        else:
            pair_t = jnp.concatenate([prev, ot], axis=0).T.astype(BF16)
            o_ref[0, qb * ATT_Q_TILE:(qb + 1) * ATT_Q_TILE, (hd // 2) * LANES:(hd // 2 + 1) * LANES] = pair_t


def _attn_prompt(q, k, vt, bmt):
    b, s, _ = q.shape
    assert vt.shape == (b, s // ATT_Q_TILE, A_WIDTH, ATT_Q_TILE)
    blk = pl.BlockSpec((1, ATT_QB * ATT_Q_TILE, A_WIDTH), lambda bi, j: (bi, j, 0))
    full = pl.BlockSpec((1, s, A_WIDTH), lambda bi, j: (bi, 0, 0))
    full_t = pl.BlockSpec((1,) + vt.shape[1:], lambda bi, j: (bi, 0, 0, 0))
    return pl.pallas_call(
        _attn_prompt_kernel,
        grid=(b, s // (ATT_QB * ATT_Q_TILE)),
        in_specs=[blk, full, full_t, _resident(bmt.shape)],
        out_specs=blk,
        out_shape=jax.ShapeDtypeStruct((b, s, A_WIDTH), BF16),
        compiler_params=pltpu.CompilerParams(
            dimension_semantics=("arbitrary", "arbitrary"), vmem_limit_bytes=VMEM_LIMIT),
        name="attn_prompt",
    )(q, k, vt, bmt)


def _attn_sample_kernel(q_ref, kn_ref, vn_ref, kc_ref, vc_ref, bias_ref, o_ref, *, n_pad):
    zpad = jnp.zeros((n_pad, A_WIDTH), BF16)
    k_all = jnp.concatenate([kc_ref[0].astype(BF16), kn_ref[0], zpad], axis=0)
    v_all = jnp.concatenate([vc_ref[0].astype(BF16), vn_ref[0], zpad], axis=0)
    o_ref[0] = _attend_heads(q_ref[0], k_all, v_all, lambda hd: bias_ref[hd])


def _attn_sample(q, kn, vn, kc, vc, bias):
    b, t, _ = q.shape
    l = kc.shape[1]
    nk = bias.shape[-1]
    new = pl.BlockSpec((1, t, A_WIDTH), lambda bi: (bi, 0, 0))
    cache = pl.BlockSpec((1, l, A_WIDTH), lambda bi: (bi, 0, 0))
    return pl.pallas_call(
        functools.partial(_attn_sample_kernel, n_pad=nk - l - t),
        grid=(b,),
        in_specs=[new, new, new, cache, cache, _resident(bias.shape)],
        out_specs=new,
        out_shape=jax.ShapeDtypeStruct((b, t, A_WIDTH), BF16),
        compiler_params=pltpu.CompilerParams(
            dimension_semantics=("arbitrary",), vmem_limit_bytes=VMEM_LIMIT),
        name="attn_sample",
    )(q, kn, vn, kc, vc, bias)


def _retention_kernel(q_ref, k_ref, v_ref, sg_ref, s0_ref, rn_ref, sout_ref,
                      state, dmat, rdec, kdec):
    bi = pl.program_id(0)
    j = pl.program_id(1)
    t = dmat.shape[1]
    n_blk = q_ref.shape[1] // t

    @pl.when((bi == 0) & (j == 0))
    def _():
        row = lax.broadcasted_iota(jnp.int32, (t, t), 0)
        col = lax.broadcasted_iota(jnp.int32, (t, t), 1)
        diff = (row - col).astype(F32)
        rowl = lax.broadcasted_iota(jnp.int32, (t, R_DIM), 0).astype(F32)
        for hh in range(R_HEADS):
            lg = LOG_DECAY[hh]
            dmat[hh] = jnp.where(diff >= 0, jnp.exp(jnp.maximum(diff, 0.0) * lg), 0.0)
            rdec[hh] = jnp.exp((rowl + 1.0) * lg)
            kdec[hh] = jnp.exp((t - 1.0 - rowl) * lg)

    @pl.when(j == 0)
    def _():
        state[...] = s0_ref[0]

    s_cur = [state[hh] for hh in range(R_HEADS)]

    def recur(blk, hh):
        rows, sl = slice(blk * t, (blk + 1) * t), slice(hh * R_DIM, (hh + 1) * R_DIM)
        q = q_ref[0, rows, sl]
        k = k_ref[0, rows, sl]
        v = v_ref[0, rows, sl]
        s_prev = s_cur[hh]
        a = lax.dot_general(q, k, (((1,), (1,)), ((), ())), preferred_element_type=F32) * dmat[hh]
        intra = jnp.dot(a.astype(BF16), v, preferred_element_type=F32)
        cross = jnp.dot(q, s_prev.astype(BF16), preferred_element_type=F32) * rdec[hh]
        kd = (k.astype(F32) * kdec[hh]).astype(BF16)
        upd = lax.dot_general(kd, v, (((0,), (0,)), ((), ())), preferred_element_type=F32)
        s_cur[hh] = s_prev * math.exp(t * LOG_DECAY[hh]) + upd
        return intra + cross

    def norm_gate(blk, hh, o):
        rows, sl = slice(blk * t, (blk + 1) * t), slice(hh * R_DIM, (hh + 1) * R_DIM)
        mu = jnp.mean(o, axis=-1, keepdims=True)
        d = o - mu
        var = jnp.mean(d * d, axis=-1, keepdims=True)
        rn = d * lax.rsqrt(var + GN_EPS) * sg_ref[0, rows, sl].astype(F32)
        rn_ref[0, rows, sl] = rn.astype(BF16)

    items = [(blk, hh) for blk in range(n_blk) for hh in range(R_HEADS)]
    o_next = recur(*items[0])
    for n, it in enumerate(items):
        o = o_next
        if n + 1 < len(items):
            o_next = recur(*items[n + 1])
        norm_gate(*it, o)

    for hh in range(R_HEADS):
        state[hh] = s_cur[hh]
        sout_ref[0, hh] = s_cur[hh]


def _retention(q, k, v, sg, s0, tile, n_blk):
    b, s, _ = q.shape
    blk = pl.BlockSpec((1, n_blk * tile, R_WIDTH), lambda bi, j: (bi, j, 0))
    st = pl.BlockSpec((1, R_HEADS, R_DIM, R_DIM), lambda bi, j: (bi, 0, 0, 0))
    return pl.pallas_call(
        _retention_kernel,
        grid=(b, s // (n_blk * tile)),
        in_specs=[blk, blk, blk, blk, st],
        out_specs=[blk, st],
        out_shape=[jax.ShapeDtypeStruct((b, s, R_WIDTH), BF16),
                   jax.ShapeDtypeStruct((b, R_HEADS, R_DIM, R_DIM), F32)],
        scratch_shapes=[pltpu.VMEM((R_HEADS, R_DIM, R_DIM), F32),
                        pltpu.VMEM((R_HEADS, tile, tile), F32),
                        pltpu.VMEM((R_HEADS, tile, R_DIM), F32),
                        pltpu.VMEM((R_HEADS, tile, R_DIM), F32)],
        compiler_params=pltpu.CompilerParams(
            dimension_semantics=("arbitrary", "arbitrary"), vmem_limit_bytes=VMEM_LIMIT),
        name="retention",
    )(q, k, v, sg, s0)


def _tail_kernel(x_ref, att_ref, rn_ref, sga_ref, sgb_ref, wao_ref, wro_ref, wo_ref,
                 g1_ref, g2_ref, g3_ref, wup_ref, wdn_ref, y_ref):
    n_sub = ROW_TILE // SUB_ROWS
    rows = [slice(r * SUB_ROWS, (r + 1) * SUB_ROWS) for r in range(n_sub)]

    def merged(r):
        ya = jnp.dot(att_ref[r, :], wao_ref[...], preferred_element_type=F32)
        yr = jnp.dot(rn_ref[r, :], wro_ref[...], preferred_element_type=F32)
        return (sga_ref[r, :].astype(F32) * ya + sgb_ref[r, :].astype(F32) * yr).astype(BF16)

    def ffn_chunk(h2, f, c):
        u = jnp.dot(h2, wup_ref[:, c * FF_CHUNK:(c + 1) * FF_CHUNK], preferred_element_type=F32)
        u = jnp.square(jnp.maximum(u, 0.0)).astype(BF16)
        return f + jnp.dot(u, wdn_ref[c * FF_CHUNK:(c + 1) * FF_CHUNK, :], preferred_element_type=F32)

    n_ff = D_FF // FF_CHUNK
    mix = [jnp.dot(merged(r), wo_ref[...], preferred_element_type=F32) for r in rows]
    x1, h2, f = [None] * n_sub, [None] * n_sub, [None] * n_sub
    for s in range(n_sub):
        x1[s] = x_ref[rows[s], :] + _rms(mix[s], g1_ref[...])
        h2[s] = _rms(x1[s], g2_ref[...]).astype(BF16)
        f[s] = jnp.zeros_like(x1[s])
        if s > 0:
            for c in range(1, n_ff):
                f[s - 1] = ffn_chunk(h2[s - 1], f[s - 1], c)
            y_ref[rows[s - 1], :] = x1[s - 1] + _rms(f[s - 1], g3_ref[...])
        f[s] = ffn_chunk(h2[s], f[s], 0)
    for c in range(1, n_ff):
        f[-1] = ffn_chunk(h2[-1], f[-1], c)
    y_ref[rows[-1], :] = x1[-1] + _rms(f[-1], g3_ref[...])


def _tail(x2, att, rn, sga, sgb, wao, wro, wo, g1, g2, g3, wup, wdn):
    rows = x2.shape[0]
    row = lambda w: pl.BlockSpec((ROW_TILE, w), lambda i: (i, 0))
    gspec = _resident((1, D_MODEL))
    return pl.pallas_call(
        _tail_kernel,
        grid=(rows // ROW_TILE,),
        in_specs=[row(D_MODEL), row(A_WIDTH), row(R_WIDTH), row(D_MODEL), row(D_MODEL),
                  _resident(wao.shape), _resident(wro.shape), _resident(wo.shape),
                  gspec, gspec, gspec, _resident(wup.shape), _resident(wdn.shape)],
        out_specs=row(D_MODEL),
        out_shape=jax.ShapeDtypeStruct((rows, D_MODEL), F32),
        compiler_params=pltpu.CompilerParams(
            dimension_semantics=("arbitrary",), vmem_limit_bytes=VMEM_LIMIT),
        name="tail",
    )(x2, att, rn, sga, sgb, wao, wro, wo, g1, g2, g3, wup, wdn)


def _rope_tables(pos):
    half = R_DIM // 2
    inv = ROPE_BASE ** (-jnp.arange(half, dtype=F32) / half)
    ang = pos[:, None] * inv[None, :]
    cos, sin = jnp.cos(ang), jnp.sin(ang)
    return jnp.concatenate([cos, cos], axis=1), jnp.concatenate([-sin, sin], axis=1)


BIAS_ROW = 1024
BIAS_TPAD = 384


def _bias_kernel(t_ref, bpt_ref, bs_ref, *, t_s, nk_valid_s):
    tq = ATT_Q_TILE
    t8 = jnp.broadcast_to(t_ref[0], (8, BIAS_TPAD))
    t1 = t8.astype(BF16)
    r1 = t8 - t1.astype(F32)
    t2 = r1.astype(BF16)
    t3 = (r1 - t2.astype(F32)).astype(BF16)

    def generator(width, offset, sign):
        k_idx = lax.broadcasted_iota(jnp.int32, (BIAS_TPAD, width), 0)
        x_idx = lax.broadcasted_iota(jnp.int32, (BIAS_TPAD, width), 1)
        sel = jnp.where(k_idx == jnp.clip(offset + sign * x_idx, 0, 2 * REL_CLIP), 1.0, 0.0).astype(BF16)
        gen = (jnp.dot(t1, sel, preferred_element_type=F32) + jnp.dot(t2, sel, preferred_element_type=F32)
               + jnp.dot(t3, sel, preferred_element_type=F32))
        return gen[0:1] * LOG2E

    def toeplitz(gen, rows):
        width = gen.shape[1]
        return pltpu.roll(jnp.broadcast_to(gen, (rows, width)), width - tq, 1, stride=1, stride_axis=0)

    qc = lax.shift_right_logical(lax.broadcasted_iota(jnp.int32, (tq, tq), 1), 6)
    key = lax.broadcasted_iota(jnp.int32, (tq, tq), 0)
    for c in range(bpt_ref.shape[1]):
        if c * tq < ATT_KEYS:
            gen = generator(2 * tq, ATT_WINDOW + REL_CLIP - tq - tq * c, 1)
            kc = lax.shift_right_logical(key + c * tq, 6)
            valid = (kc >= qc) & (kc <= qc + LEFT_CHUNKS)
            bpt_ref[0, c] = jnp.where(valid, toeplitz(gen, tq)[:, 0:tq], NEG_BIG)
        else:
            bpt_ref[0, c] = jnp.full((tq, tq), NEG_BIG, F32)
    nk_s = bs_ref.shape[2]
    gen_s = generator(BIAS_ROW, ATT_WINDOW + tq + REL_CLIP, -1)
    lane = lax.broadcasted_iota(jnp.int32, (t_s, nk_s), 1)
    bs_ref[0] = jnp.where(lane < nk_valid_s, toeplitz(gen_s, t_s)[:, 0:nk_s], NEG_BIG)


def _build_bias(table, t_s, l_s, nk_s):
    assert CHUNK == 64 and l_s == ATT_WINDOW and t_s <= ATT_Q_TILE and nk_s <= ATT_KEYS
    hn, tl = table.shape
    tpad = jnp.pad(table.astype(F32), ((0, 0), (0, BIAS_TPAD - tl))).reshape(hn, 1, BIAS_TPAD)
    n_c = ATT_KEYS // ATT_Q_TILE + 1
    return pl.pallas_call(
        functools.partial(_bias_kernel, t_s=t_s, nk_valid_s=l_s + t_s),
        grid=(hn,),
        in_specs=[pl.BlockSpec((1, 1, BIAS_TPAD), lambda h: (h, 0, 0))],
        out_specs=[pl.BlockSpec((1, n_c, ATT_Q_TILE, ATT_Q_TILE), lambda h: (h, 0, 0, 0)),
                   pl.BlockSpec((1, t_s, nk_s), lambda h: (h, 0, 0))],
        out_shape=[jax.ShapeDtypeStruct((hn, n_c, ATT_Q_TILE, ATT_Q_TILE), F32),
                   jax.ShapeDtypeStruct((hn, t_s, nk_s), F32)],
        compiler_params=pltpu.CompilerParams(
            dimension_semantics=("arbitrary",), vmem_limit_bytes=VMEM_LIMIT),
        name="rel_bias",
    )(tpad)


def kernel(x_prompt, x_sample, cache_attn_k, cache_attn_v, state_retention, norm_mix_pre, w_in,
           rel_bias_table, w_attn_out, w_ret_out, w_o, norm_mix_post, norm_ffn_pre, w_ff_up,
           w_ff_down, norm_ffn_post):
    depth = w_in.shape[0]
    bp, sp, _ = x_prompt.shape
    bs, ts, _ = x_sample.shape
    lc = cache_attn_k.shape[2]
    n_keys_s = -(-(lc + ts) // LANES) * LANES

    cs_p, sn_p = _rope_tables(jnp.arange(sp, dtype=F32))
    pos_s = PAST_LEN + jnp.arange(ts, dtype=F32)
    cs_s, sn_s = _rope_tables(jnp.tile(pos_s, bs))

    xp = x_prompt.reshape(bp * sp, D_MODEL)
    xs = x_sample.reshape(bs * ts, D_MODEL)
    outs = [[] for _ in range(6)]
    for l in range(depth):
        w_in_b = w_in[l].astype(BF16)
        w_v_t = _w_v_transposed(w_in, l)
        wao, wro, wo = w_attn_out[l].astype(BF16), w_ret_out[l].astype(BF16), w_o[l].astype(BF16)
        wup, wdn = w_ff_up[l].astype(BF16), w_ff_down[l].astype(BF16)
        g0 = norm_mix_pre[l].reshape(1, D_MODEL)
        g1 = norm_mix_post[l].reshape(1, D_MODEL)
        g2 = norm_ffn_pre[l].reshape(1, D_MODEL)
        g3 = norm_ffn_post[l].reshape(1, D_MODEL)
        tail_w = (wao, wro, wo, g1, g2, g3, wup, wdn)
        bias_p, bias_s = _build_bias(rel_bias_table[l], ts, lc, n_keys_s)

        qa, ka, va_t, qr, kr, vr, sgr, sga, sgb, kt, vt_t = _in_proj(
            xp, g0, w_in_b, w_v_t, cs_p, sn_p, tail_period=sp // ROW_TILE, v_transposed=True)
        r3 = lambda a: a.reshape(bp, sp, a.shape[-1])
        att = _attn_prompt(r3(qa), r3(ka), va_t.reshape(bp, sp // SUB_ROWS, A_WIDTH, SUB_ROWS), bias_p)
        rn, s_p = _retention(r3(qr), r3(kr), r3(vr), r3(sgr),
                             jnp.zeros((bp, R_HEADS, R_DIM, R_DIM), F32), RET_TILE, RET_BLOCKS)
        xp = _tail(xp, att.reshape(bp * sp, A_WIDTH), rn.reshape(bp * sp, R_WIDTH), sga, sgb, *tail_w)
        outs[0].append(kt.reshape(bp, ATT_WINDOW, A_HEADS, A_HEAD_DIM))
        outs[1].append(vt_t.transpose(0, 2, 1).reshape(bp, ATT_WINDOW, A_HEADS, A_HEAD_DIM))
        outs[2].append(s_p)

        qa, ka, va, qr, kr, vr, sgr, sga, sgb, kt, vt = _in_proj(
            xs, g0, w_in_b, w_v_t, cs_s, sn_s, tail_period=1, v_transposed=False)
        r3 = lambda a: a.reshape(bs, ts, a.shape[-1])
        att = _attn_sample(r3(qa), r3(ka), r3(va),
                           cache_attn_k[l].reshape(bs, lc, A_WIDTH),
                           cache_attn_v[l].reshape(bs, lc, A_WIDTH),
                           bias_s)
        rn, s_s = _retention(r3(qr), r3(kr), r3(vr), r3(sgr),
                             state_retention[l].astype(F32), ts, 1)
        xs = _tail(xs, att.reshape(bs * ts, A_WIDTH), rn.reshape(bs * ts, R_WIDTH), sga, sgb, *tail_w)
        outs[3].append(kt.reshape(bs, ts, A_HEADS, A_HEAD_DIM).astype(cache_attn_k.dtype))
        outs[4].append(vt.reshape(bs, ts, A_HEADS, A_HEAD_DIM).astype(cache_attn_v.dtype))
        outs[5].append(s_s.astype(state_retention.dtype))

    return (xp.reshape(bp, sp, D_MODEL), xs.reshape(bs, ts, D_MODEL),
            jnp.stack(outs[0]), jnp.stack(outs[1]), jnp.stack(outs[2]),
            jnp.stack(outs[3]), jnp.stack(outs[4]), jnp.stack(outs[5]))
```

```python
import functools
import math

import jax
import jax.numpy as jnp
import numpy as np
from jax import lax
from jax.experimental import pallas as pl
from jax.experimental.pallas import tpu as pltpu

F32 = jnp.float32
BF16 = jnp.bfloat16

D_MODEL = 1024
CHUNK = 64
LEFT_CHUNKS = 8
ATT_WINDOW = LEFT_CHUNKS * CHUNK
A_HEADS = 8
A_HEAD_DIM = 64
A_WIDTH = A_HEADS * A_HEAD_DIM
REL_CLIP = 128
R_HEADS = 4
R_DIM = 128
R_WIDTH = R_HEADS * R_DIM
ROPE_BASE = 10000.0
D_FF = 4 * D_MODEL
NORM_EPS = 1e-6
GN_EPS = 1e-6
PAST_LEN = 4096

IN_SPLITS = (A_WIDTH, A_WIDTH, A_WIDTH, R_WIDTH, R_WIDTH, R_WIDTH, R_WIDTH, D_MODEL, D_MODEL)
IN_COLS = sum(IN_SPLITS)
IN_OFFS = tuple(int(o) for o in np.cumsum((0,) + IN_SPLITS))

LANES = 128
MXU_N = 256
ROW_TILE = 1024
SUB_ROWS = 256
ATT_Q_TILE = 256
ATT_KEYS = ATT_WINDOW + ATT_Q_TILE
ATT_QB = 4
ATT_AHEAD = 3
PV_EXTRA_ROWS = 16
RET_TILE = 256
RET_BLOCKS = 4
FF_CHUNK = 1024
VMEM_LIMIT = 56 * 1024 * 1024
NEG_BIG = -1e30
LOG2E = math.log2(math.e)
ATT_Q_SCALE = (A_HEAD_DIM ** -0.5) * LOG2E
LOG_DECAY = tuple(math.log1p(-(2.0 ** (-5 - h))) for h in range(R_HEADS))


def _resident(shape):
    nd = len(shape)
    return pl.BlockSpec(shape, lambda *_: (0,) * nd, pipeline_mode=pl.Buffered(1))


def _rms(x, g):
    ms = jnp.mean(x * x, axis=-1, keepdims=True)
    return x * lax.rsqrt(ms + NORM_EPS) * g


def _inproj_kernel(x_ref, g_ref, w_ref, wvt_ref, cs_ref, sn_ref,
                   qa_ref, ka_ref, va_ref, qr_ref, kr_ref, vr_ref, sgr_ref, sga_ref, sgb_ref,
                   kt_ref, vt_ref, *, v_transposed):
    row_tile, tail_rows = x_ref.shape[0], kt_ref.shape[0]
    tail_start = row_tile - tail_rows
    for r0 in range(0, row_tile, SUB_ROWS):
        rows = slice(r0, r0 + SUB_ROWS)
        in_tail = r0 >= tail_start
        trows = slice(r0 - tail_start, r0 - tail_start + SUB_ROWS)
        h = _rms(x_ref[rows, :], g_ref[...]).astype(BF16)

        def seg(s):
            return jnp.dot(h, w_ref[:, IN_OFFS[s]:IN_OFFS[s + 1]], preferred_element_type=F32)

        cs = cs_ref[rows, :]
        sn = sn_ref[rows, :]

        def rope(z, scale):
            parts = []
            for hh in range(R_HEADS):
                zh = z[:, hh * R_DIM:(hh + 1) * R_DIM]
                r = zh * cs + pltpu.roll(zh, R_DIM // 2, 1) * sn
                if scale != 1.0:
                    r = r * scale
                parts.append(r.astype(BF16))
            return jnp.concatenate(parts, axis=1)

        qa_ref[rows, :] = (seg(0) * ATT_Q_SCALE).astype(BF16)
        ka = seg(1)
        ka_ref[rows, :] = ka.astype(BF16)
        if in_tail:
            kt_ref[trows, :] = ka
        if v_transposed:
            va_t = lax.dot_general(wvt_ref[...], h, (((1,), (1,)), ((), ())), preferred_element_type=F32)
            va_ref[r0 // SUB_ROWS] = va_t.astype(BF16)
            if in_tail:
                vt_ref[0, :, trows] = va_t
        else:
            va = seg(2)
            va_ref[rows, :] = va.astype(BF16)
            if in_tail:
                vt_ref[trows, :] = va
        qr_ref[rows, :] = rope(seg(3), 1.0)
        kr_ref[rows, :] = rope(seg(4), R_DIM ** -0.5)
        vr_ref[rows, :] = seg(5).astype(BF16)
        gr = seg(6)
        sgr_ref[rows, :] = (gr * jax.nn.sigmoid(gr)).astype(BF16)
        sga_ref[rows, :] = jax.nn.sigmoid(seg(7)).astype(BF16)
        sgb_ref[rows, :] = jax.nn.sigmoid(seg(8)).astype(BF16)


def _w_v_transposed(w_in, l):
    assert IN_OFFS[2] % A_WIDTH == 0

    def body(w_ref, o_ref):
        o_ref[...] = w_ref[0].T.astype(BF16)

    return pl.pallas_call(
        body,
        grid=(1,),
        in_specs=[pl.BlockSpec((1, D_MODEL, A_WIDTH), lambda i: (l, 0, IN_OFFS[2] // A_WIDTH))],
        out_specs=pl.BlockSpec((A_WIDTH, D_MODEL), lambda i: (0, 0)),
        out_shape=jax.ShapeDtypeStruct((A_WIDTH, D_MODEL), BF16),
        name="w_v_transpose",
    )(w_in)


def _in_proj(x2, g, w_bf, wvt_bf, cs, sn, seq_rows, tail_rows, v_transposed):
    rows = x2.shape[0]
    row_tile = min(seq_rows, ROW_TILE)
    assert tail_rows <= row_tile and tail_rows % SUB_ROWS == 0 and seq_rows % row_tile == 0
    tail_period = seq_rows // row_tile
    n = rows // row_tile
    n_pos = cs.shape[0] // row_tile
    n_sub = row_tile // SUB_ROWS
    row = lambda w: pl.BlockSpec((row_tile, w), lambda i: (i, 0))
    tail = pl.BlockSpec((tail_rows, A_WIDTH), lambda i: (i // tail_period, 0))
    pos = pl.BlockSpec((row_tile, R_DIM), lambda i: (i % n_pos, 0))
    bf = lambda w: jax.ShapeDtypeStruct((rows, w), BF16)
    n_grp = rows // seq_rows
    tail_shape = jax.ShapeDtypeStruct((n_grp * tail_rows, A_WIDTH), F32)
    if v_transposed:
        v_spec = pl.BlockSpec((n_sub, A_WIDTH, SUB_ROWS), lambda i: (i, 0, 0))
        v_shape = jax.ShapeDtypeStruct((rows // SUB_ROWS, A_WIDTH, SUB_ROWS), BF16)
        vt_spec = pl.BlockSpec((1, A_WIDTH, tail_rows), lambda i: (i // tail_period, 0, 0))
        vt_shape = jax.ShapeDtypeStruct((n_grp, A_WIDTH, tail_rows), F32)
    else:
        v_spec, v_shape, vt_spec, vt_shape = row(A_WIDTH), bf(A_WIDTH), tail, tail_shape
    return pl.pallas_call(
        functools.partial(_inproj_kernel, v_transposed=v_transposed),
        grid=(n,),
        in_specs=[row(D_MODEL), _resident((1, D_MODEL)), _resident((D_MODEL, IN_COLS)),
                  _resident((A_WIDTH, D_MODEL)), pos, pos],
        out_specs=[row(A_WIDTH), row(A_WIDTH), v_spec] + [row(R_WIDTH)] * 4 + [row(D_MODEL)] * 2
        + [tail, vt_spec],
        out_shape=[bf(A_WIDTH), bf(A_WIDTH), v_shape] + [bf(R_WIDTH)] * 4 + [bf(D_MODEL)] * 2
        + [tail_shape, vt_shape],
        compiler_params=pltpu.CompilerParams(
            dimension_semantics=("arbitrary",), vmem_limit_bytes=VMEM_LIMIT),
        name="in_proj",
    )(x2, g, w_bf, wvt_bf, cs, sn)


def _attend_heads(q_all, k_all, v_all, bias_of_head):
    tq = q_all.shape[0]
    lane_q = lax.broadcasted_iota(jnp.int32, (tq, LANES), 1)
    lane_o = lax.broadcasted_iota(jnp.int32, (tq, MXU_N), 1)
    heads_per_group = MXU_N // A_HEAD_DIM
    pairs_per_group = heads_per_group // 2
    outs = []
    for grp in range(A_HEADS // heads_per_group):
        v4 = v_all[:, grp * MXU_N:(grp + 1) * MXU_N]
        e_parts, d_parts = [], []
        for pg in range(pairs_per_group):
            pair = grp * pairs_per_group + pg
            q2 = q_all[:, pair * LANES:(pair + 1) * LANES]
            k2 = k_all[:, pair * LANES:(pair + 1) * LANES]
            zero = jnp.zeros_like(q2)
            q_st = jnp.concatenate([jnp.where(lane_q < A_HEAD_DIM, q2, zero),
                                    jnp.where(lane_q >= A_HEAD_DIM, q2, zero)], axis=0)
            s = lax.dot_general(q_st, k2, (((1,), (1,)), ((), ())), preferred_element_type=F32)
            s = s + jnp.concatenate([bias_of_head(2 * pair), bias_of_head(2 * pair + 1)], axis=0)
            m = jnp.max(s, axis=-1, keepdims=True)
            e = jnp.exp2(s - m)
            d_parts.append(jnp.sum(e, axis=-1, keepdims=True))
            e_parts.append(e.astype(BF16))
        o = jnp.dot(jnp.concatenate(e_parts, axis=0), v4, preferred_element_type=F32)
        o = o * (1.0 / jnp.concatenate(d_parts, axis=0))
        acc = jnp.zeros((tq, MXU_N), F32)
        for hg in range(heads_per_group):
            olo = hg * A_HEAD_DIM
            acc = jnp.where((lane_o >= olo) & (lane_o < olo + A_HEAD_DIM), o[hg * tq:(hg + 1) * tq], acc)
        outs.append(acc.astype(BF16))
    return jnp.concatenate(outs, axis=1)


def _reduce_rows(x, op):
    parts = [x[r:r + 8] for r in range(0, x.shape[0], 8)]
    while len(parts) > 1:
        paired = [op(parts[i], parts[i + 1]) for i in range(0, len(parts) - 1, 2)]
        parts = paired + parts[len(parts) - len(parts) % 2:]
    red = jnp.max if op is jnp.maximum else jnp.sum
    return red(parts[0], axis=0, keepdims=True)


def _attn_prompt_kernel(q_ref, k_ref, vt_ref, bmt_ref, o_ref):
    n_c = ATT_KEYS // ATT_Q_TILE
    n_left = n_c - 1
    lane_q = lax.broadcasted_iota(jnp.int32, (ATT_Q_TILE, LANES), 1)
    ones_rows = jnp.ones((PV_EXTRA_ROWS, ATT_Q_TILE), BF16)
    band_keys = (LEFT_CHUNKS + 2) * CHUNK
    n_lt = ATT_Q_TILE // LANES

    def pieces(t):
        lo, hi = LANES * t, LANES * t + band_keys
        out = []
        for c in range(n_c):
            a, b = max(lo, c * ATT_Q_TILE), min(hi, (c + 1) * ATT_Q_TILE)
            if a < b:
                out.append((c, a - c * ATT_Q_TILE, b - c * ATT_Q_TILE))
        return out

    def window(qb):
        j = pl.program_id(1) * ATT_QB + qb
        kc = [jnp.maximum(j - n_left + c, 0) for c in range(n_c)]
        bc = [jnp.where(j - n_left + c >= 0, c, n_c) for c in range(n_c)]
        return kc, bc

    def scores(qb, hd):
        kc, bc = window(qb)
        pair = hd // 2
        q2 = q_ref[0, qb * ATT_Q_TILE:(qb + 1) * ATT_Q_TILE, pair * LANES:(pair + 1) * LANES]
        k2 = jnp.concatenate(
            [k_ref[0, pl.ds(pl.multiple_of(kc[c] * ATT_Q_TILE, ATT_Q_TILE), ATT_Q_TILE),
                   pair * LANES:(pair + 1) * LANES] for c in range(n_c)], axis=0)
        lo = (hd % 2) * A_HEAD_DIM
        qm = jnp.where((lane_q >= lo) & (lane_q < lo + A_HEAD_DIM), q2, jnp.zeros_like(q2))
        st = lax.dot_general(k2, qm, (((1,), (1,)), ((), ())), preferred_element_type=F32)
        tiles = []
        for t in range(n_lt):
            lanes = slice(t * LANES, (t + 1) * LANES)
            tiles.append([st[c * ATT_Q_TILE + a:c * ATT_Q_TILE + b, lanes] + bmt_ref[hd, bc[c], a:b, lanes]
                          for c, a, b in pieces(t)])
        return tiles

    def softmax(tiles):
        cols = []
        for t in range(n_lt):
            st = jnp.concatenate(tiles[t], axis=0)
            m = _reduce_rows(st, jnp.maximum)
            e = jnp.exp2(st - m).astype(BF16)
            zero = lambda n: [jnp.zeros((n, LANES), BF16)] if n else []
            cols.append(jnp.concatenate(zero(LANES * t) + [e] + zero(ATT_KEYS - band_keys - LANES * t), axis=0))
        return jnp.concatenate(cols, axis=1)

    def pv(qb, hd, pt):
        kc, _ = window(qb)
        ot = jnp.zeros((A_HEAD_DIM + PV_EXTRA_ROWS, ATT_Q_TILE), F32)
        for c in range(n_c):
            vt_c = vt_ref[0, kc[c], hd * A_HEAD_DIM:(hd + 1) * A_HEAD_DIM, :]
            ot = ot + jnp.dot(jnp.concatenate([vt_c, ones_rows], axis=0),
                              pt[c * ATT_Q_TILE:(c + 1) * ATT_Q_TILE, :], preferred_element_type=F32)
        return ot[0:A_HEAD_DIM] * (1.0 / ot[A_HEAD_DIM:A_HEAD_DIM + 1])

    items = [(qb, hd) for qb in range(ATT_QB) for hd in range(A_HEADS)]
    tiles = {it: scores(*it) for it in items[:ATT_AHEAD]}
    prev = None
    for n, (qb, hd) in enumerate(items):
        pt = softmax(tiles.pop((qb, hd)))
        if n + ATT_AHEAD < len(items):
            nxt = items[n + ATT_AHEAD]
            tiles[nxt] = scores(*nxt)
        ot = pv(qb, hd, pt)
        if hd % 2 == 0:
            prev = ot
        else:
            pair_t = jnp.concatenate([prev, ot], axis=0).T.astype(BF16)
            o_ref[0, qb * ATT_Q_TILE:(qb + 1) * ATT_Q_TILE, (hd // 2) * LANES:(hd // 2 + 1) * LANES] = pair_t


def _attn_prompt(q, k, vt, bmt):
    b, s, _ = q.shape
    assert vt.shape == (b, s // ATT_Q_TILE, A_WIDTH, ATT_Q_TILE)
    blk = pl.BlockSpec((1, ATT_QB * ATT_Q_TILE, A_WIDTH), lambda bi, j: (bi, j, 0))
    full = pl.BlockSpec((1, s, A_WIDTH), lambda bi, j: (bi, 0, 0))
    full_t = pl.BlockSpec((1,) + vt.shape[1:], lambda bi, j: (bi, 0, 0, 0))
    return pl.pallas_call(
        _attn_prompt_kernel,
        grid=(b, s // (ATT_QB * ATT_Q_TILE)),
        in_specs=[blk, full, full_t, _resident(bmt.shape)],
        out_specs=blk,
        out_shape=jax.ShapeDtypeStruct((b, s, A_WIDTH), BF16),
        compiler_params=pltpu.CompilerParams(
            dimension_semantics=("arbitrary", "arbitrary"), vmem_limit_bytes=VMEM_LIMIT),
        name="attn_prompt",
    )(q, k, vt, bmt)


def _attn_sample_kernel(q_ref, kn_ref, vn_ref, kc_ref, vc_ref, bias_ref, o_ref, *, n_pad):
    zpad = jnp.zeros((n_pad, A_WIDTH), BF16)
    k_all = jnp.concatenate([kc_ref[0].astype(BF16), kn_ref[0], zpad], axis=0)
    v_all = jnp.concatenate([vc_ref[0].astype(BF16), vn_ref[0], zpad], axis=0)
    o_ref[0] = _attend_heads(q_ref[0], k_all, v_all, lambda hd: bias_ref[hd])


def _attn_sample(q, kn, vn, kc, vc, bias):
    b, t, _ = q.shape
    l = kc.shape[1]
    nk = bias.shape[-1]
    new = pl.BlockSpec((1, t, A_WIDTH), lambda bi: (bi, 0, 0))
    cache = pl.BlockSpec((1, l, A_WIDTH), lambda bi: (bi, 0, 0))
    return pl.pallas_call(
        functools.partial(_attn_sample_kernel, n_pad=nk - l - t),
        grid=(b,),
        in_specs=[new, new, new, cache, cache, _resident(bias.shape)],
        out_specs=new,
        out_shape=jax.ShapeDtypeStruct((b, t, A_WIDTH), BF16),
        compiler_params=pltpu.CompilerParams(
            dimension_semantics=("arbitrary",), vmem_limit_bytes=VMEM_LIMIT),
        name="attn_sample",
    )(q, kn, vn, kc, vc, bias)


def _retention_kernel(q_ref, k_ref, v_ref, sg_ref, s0_ref, rn_ref, sout_ref,
                      state, dmat, rdec, kdec):
    bi = pl.program_id(0)
    j = pl.program_id(1)
    t = dmat.shape[1]
    n_blk = q_ref.shape[1] // t

    @pl.when((bi == 0) & (j == 0))
    def _():
        row = lax.broadcasted_iota(jnp.int32, (t, t), 0)
        col = lax.broadcasted_iota(jnp.int32, (t, t), 1)
        diff = (row - col).astype(F32)
        rowl = lax.broadcasted_iota(jnp.int32, (t, R_DIM), 0).astype(F32)
        for hh in range(R_HEADS):
            lg = LOG_DECAY[hh]
            dmat[hh] = jnp.where(diff >= 0, jnp.exp(jnp.maximum(diff, 0.0) * lg), 0.0)
            rdec[hh] = jnp.exp((rowl + 1.0) * lg)
            kdec[hh] = jnp.exp((t - 1.0 - rowl) * lg)

    @pl.when(j == 0)
    def _():
        state[...] = s0_ref[0]

    s_cur = [state[hh] for hh in range(R_HEADS)]

    def recur(blk, hh):
        rows, sl = slice(blk * t, (blk + 1) * t), slice(hh * R_DIM, (hh + 1) * R_DIM)
        q = q_ref[0, rows, sl]
        k = k_ref[0, rows, sl]
        v = v_ref[0, rows, sl]
        s_prev = s_cur[hh]
        a = lax.dot_general(q, k, (((1,), (1,)), ((), ())), preferred_element_type=F32) * dmat[hh]
        intra = jnp.dot(a.astype(BF16), v, preferred_element_type=F32)
        cross = jnp.dot(q, s_prev.astype(BF16), preferred_element_type=F32) * rdec[hh]
        kd = (k.astype(F32) * kdec[hh]).astype(BF16)
        upd = lax.dot_general(kd, v, (((0,), (0,)), ((), ())), preferred_element_type=F32)
        s_cur[hh] = s_prev * math.exp(t * LOG_DECAY[hh]) + upd
        return intra + cross

    def norm_gate(blk, hh, o):
        rows, sl = slice(blk * t, (blk + 1) * t), slice(hh * R_DIM, (hh + 1) * R_DIM)
        mu = jnp.mean(o, axis=-1, keepdims=True)
        d = o - mu
        var = jnp.mean(d * d, axis=-1, keepdims=True)
        rn = d * lax.rsqrt(var + GN_EPS) * sg_ref[0, rows, sl].astype(F32)
        rn_ref[0, rows, sl] = rn.astype(BF16)

    items = [(blk, hh) for blk in range(n_blk) for hh in range(R_HEADS)]
    o_next = recur(*items[0])
    for n, it in enumerate(items):
        o = o_next
        if n + 1 < len(items):
            o_next = recur(*items[n + 1])
        norm_gate(*it, o)

    for hh in range(R_HEADS):
        state[hh] = s_cur[hh]
        sout_ref[0, hh] = s_cur[hh]


def _retention(q, k, v, sg, s0, tile, n_blk):
    b, s, _ = q.shape
    blk = pl.BlockSpec((1, n_blk * tile, R_WIDTH), lambda bi, j: (bi, j, 0))
    st = pl.BlockSpec((1, R_HEADS, R_DIM, R_DIM), lambda bi, j: (bi, 0, 0, 0))
    return pl.pallas_call(
        _retention_kernel,
        grid=(b, s // (n_blk * tile)),
        in_specs=[blk, blk, blk, blk, st],
        out_specs=[blk, st],
        out_shape=[jax.ShapeDtypeStruct((b, s, R_WIDTH), BF16),
                   jax.ShapeDtypeStruct((b, R_HEADS, R_DIM, R_DIM), F32)],
        scratch_shapes=[pltpu.VMEM((R_HEADS, R_DIM, R_DIM), F32),
                        pltpu.VMEM((R_HEADS, tile, tile), F32),
                        pltpu.VMEM((R_HEADS, tile, R_DIM), F32),
                        pltpu.VMEM((R_HEADS, tile, R_DIM), F32)],
        compiler_params=pltpu.CompilerParams(
            dimension_semantics=("arbitrary", "arbitrary"), vmem_limit_bytes=VMEM_LIMIT),
        name="retention",
    )(q, k, v, sg, s0)


def _tail_kernel(x_ref, att_ref, rn_ref, sga_ref, sgb_ref, wao_ref, wro_ref, wo_ref,
                 g1_ref, g2_ref, g3_ref, wup_ref, wdn_ref, y_ref):
    n_sub = x_ref.shape[0] // SUB_ROWS
    rows = [slice(r * SUB_ROWS, (r + 1) * SUB_ROWS) for r in range(n_sub)]

    def merged(r):
        ya = jnp.dot(att_ref[r, :], wao_ref[...], preferred_element_type=F32)
        yr = jnp.dot(rn_ref[r, :], wro_ref[...], preferred_element_type=F32)
        return (sga_ref[r, :].astype(F32) * ya + sgb_ref[r, :].astype(F32) * yr).astype(BF16)

    def ffn_chunk(h2, f, c):
        u = jnp.dot(h2, wup_ref[:, c * FF_CHUNK:(c + 1) * FF_CHUNK], preferred_element_type=F32)
        u = jnp.square(jnp.maximum(u, 0.0)).astype(BF16)
        return f + jnp.dot(u, wdn_ref[c * FF_CHUNK:(c + 1) * FF_CHUNK, :], preferred_element_type=F32)

    n_ff = D_FF // FF_CHUNK
    mixed = lambda r: jnp.dot(merged(r), wo_ref[...], preferred_element_type=F32)
    x1, h2, f = [None] * n_sub, [None] * n_sub, [None] * n_sub
    mix_next = mixed(rows[0])
    for s in range(n_sub):
        mix = mix_next
        if s + 1 < n_sub:
            mix_next = mixed(rows[s + 1])
        x1[s] = x_ref[rows[s], :] + _rms(mix, g1_ref[...])
        h2[s] = _rms(x1[s], g2_ref[...]).astype(BF16)
        f[s] = jnp.zeros_like(x1[s])
        if s > 0:
            for c in range(1, n_ff):
                f[s - 1] = ffn_chunk(h2[s - 1], f[s - 1], c)
            y_ref[rows[s - 1], :] = x1[s - 1] + _rms(f[s - 1], g3_ref[...])
        f[s] = ffn_chunk(h2[s], f[s], 0)
    for c in range(1, n_ff):
        f[-1] = ffn_chunk(h2[-1], f[-1], c)
    y_ref[rows[-1], :] = x1[-1] + _rms(f[-1], g3_ref[...])


def _tail(x2, att, rn, sga, sgb, wao, wro, wo, g1, g2, g3, wup, wdn):
    rows = x2.shape[0]
    row_tile = min(rows, ROW_TILE)
    row = lambda w: pl.BlockSpec((row_tile, w), lambda i: (i, 0))
    gspec = _resident((1, D_MODEL))
    return pl.pallas_call(
        _tail_kernel,
        grid=(rows // row_tile,),
        in_specs=[row(D_MODEL), row(A_WIDTH), row(R_WIDTH), row(D_MODEL), row(D_MODEL),
                  _resident(wao.shape), _resident(wro.shape), _resident(wo.shape),
                  gspec, gspec, gspec, _resident(wup.shape), _resident(wdn.shape)],
        out_specs=row(D_MODEL),
        out_shape=jax.ShapeDtypeStruct((rows, D_MODEL), F32),
        compiler_params=pltpu.CompilerParams(
            dimension_semantics=("arbitrary",), vmem_limit_bytes=VMEM_LIMIT),
        name="tail",
    )(x2, att, rn, sga, sgb, wao, wro, wo, g1, g2, g3, wup, wdn)


def _rope_tables(pos):
    half = R_DIM // 2
    inv = ROPE_BASE ** (-jnp.arange(half, dtype=F32) / half)
    ang = pos[:, None] * inv[None, :]
    cos, sin = jnp.cos(ang), jnp.sin(ang)
    return jnp.concatenate([cos, cos], axis=1), jnp.concatenate([-sin, sin], axis=1)


BIAS_ROW = 1024
BIAS_TPAD = 384


def _bias_kernel(t_ref, bpt_ref, bs_ref, *, t_s, nk_valid_s):
    tq = ATT_Q_TILE
    t8 = jnp.broadcast_to(t_ref[0], (8, BIAS_TPAD))
    t1 = t8.astype(BF16)
    r1 = t8 - t1.astype(F32)
    t2 = r1.astype(BF16)
    t3 = (r1 - t2.astype(F32)).astype(BF16)

    def generator(width, offset, sign):
        k_idx = lax.broadcasted_iota(jnp.int32, (BIAS_TPAD, width), 0)
        x_idx = lax.broadcasted_iota(jnp.int32, (BIAS_TPAD, width), 1)
        sel = jnp.where(k_idx == jnp.clip(offset + sign * x_idx, 0, 2 * REL_CLIP), 1.0, 0.0).astype(BF16)
        gen = (jnp.dot(t1, sel, preferred_element_type=F32) + jnp.dot(t2, sel, preferred_element_type=F32)
               + jnp.dot(t3, sel, preferred_element_type=F32))
        return gen[0:1] * LOG2E

    def toeplitz(gen, rows):
        width = gen.shape[1]
        return pltpu.roll(jnp.broadcast_to(gen, (rows, width)), width - tq, 1, stride=1, stride_axis=0)

    qc = lax.shift_right_logical(lax.broadcasted_iota(jnp.int32, (tq, tq), 1), 6)
    key = lax.broadcasted_iota(jnp.int32, (tq, tq), 0)
    for c in range(bpt_ref.shape[1]):
        if c * tq < ATT_KEYS:
            gen = generator(2 * tq, ATT_WINDOW + REL_CLIP - tq - tq * c, 1)
            kc = lax.shift_right_logical(key + c * tq, 6)
            valid = (kc >= qc) & (kc <= qc + LEFT_CHUNKS)
            bpt_ref[0, c] = jnp.where(valid, toeplitz(gen, tq)[:, 0:tq], NEG_BIG)
        else:
            bpt_ref[0, c] = jnp.full((tq, tq), NEG_BIG, F32)
    nk_s = bs_ref.shape[2]
    gen_s = generator(BIAS_ROW, ATT_WINDOW + tq + REL_CLIP, -1)
    lane = lax.broadcasted_iota(jnp.int32, (t_s, nk_s), 1)
    bs_ref[0] = jnp.where(lane < nk_valid_s, toeplitz(gen_s, t_s)[:, 0:nk_s], NEG_BIG)


def _build_bias(table, t_s, l_s, nk_s):
    assert CHUNK == 64 and l_s == ATT_WINDOW and t_s <= ATT_Q_TILE and nk_s <= ATT_KEYS
    hn, tl = table.shape
    tpad = jnp.pad(table.astype(F32), ((0, 0), (0, BIAS_TPAD - tl))).reshape(hn, 1, BIAS_TPAD)
    n_c = ATT_KEYS // ATT_Q_TILE + 1
    return pl.pallas_call(
        functools.partial(_bias_kernel, t_s=t_s, nk_valid_s=l_s + t_s),
        grid=(hn,),
        in_specs=[pl.BlockSpec((1, 1, BIAS_TPAD), lambda h: (h, 0, 0))],
        out_specs=[pl.BlockSpec((1, n_c, ATT_Q_TILE, ATT_Q_TILE), lambda h: (h, 0, 0, 0)),
                   pl.BlockSpec((1, t_s, nk_s), lambda h: (h, 0, 0))],
        out_shape=[jax.ShapeDtypeStruct((hn, n_c, ATT_Q_TILE, ATT_Q_TILE), F32),
                   jax.ShapeDtypeStruct((hn, t_s, nk_s), F32)],
        compiler_params=pltpu.CompilerParams(
            dimension_semantics=("arbitrary",), vmem_limit_bytes=VMEM_LIMIT),
        name="rel_bias",
    )(tpad)


def kernel(x_prompt, x_sample, cache_attn_k, cache_attn_v, state_retention, norm_mix_pre, w_in,
           rel_bias_table, w_attn_out, w_ret_out, w_o, norm_mix_post, norm_ffn_pre, w_ff_up,
           w_ff_down, norm_ffn_post):
    depth = w_in.shape[0]
    bp, sp, _ = x_prompt.shape
    bs, ts, _ = x_sample.shape
    lc = cache_attn_k.shape[2]
    n_keys_s = -(-(lc + ts) // LANES) * LANES

    cs_p, sn_p = _rope_tables(jnp.arange(sp, dtype=F32))
    pos_s = PAST_LEN + jnp.arange(ts, dtype=F32)
    cs_s, sn_s = _rope_tables(jnp.tile(pos_s, bs))

    xp = x_prompt.reshape(bp * sp, D_MODEL)
    xs = x_sample.reshape(bs * ts, D_MODEL)
    outs = [[] for _ in range(6)]
    for l in range(depth):
        w_in_b = w_in[l].astype(BF16)
        w_v_t = _w_v_transposed(w_in, l)
        wao, wro, wo = w_attn_out[l].astype(BF16), w_ret_out[l].astype(BF16), w_o[l].astype(BF16)
        wup, wdn = w_ff_up[l].astype(BF16), w_ff_down[l].astype(BF16)
        g0 = norm_mix_pre[l].reshape(1, D_MODEL)
        g1 = norm_mix_post[l].reshape(1, D_MODEL)
        g2 = norm_ffn_pre[l].reshape(1, D_MODEL)
        g3 = norm_ffn_post[l].reshape(1, D_MODEL)
        tail_w = (wao, wro, wo, g1, g2, g3, wup, wdn)
        bias_p, bias_s = _build_bias(rel_bias_table[l], ts, lc, n_keys_s)

        qa, ka, va_t, qr, kr, vr, sgr, sga, sgb, kt, vt_t = _in_proj(
            xp, g0, w_in_b, w_v_t, cs_p, sn_p, seq_rows=sp, tail_rows=ATT_WINDOW, v_transposed=True)
        r3 = lambda a: a.reshape(bp, sp, a.shape[-1])
        att = _attn_prompt(r3(qa), r3(ka), va_t.reshape(bp, sp // SUB_ROWS, A_WIDTH, SUB_ROWS), bias_p)
        rn, s_p = _retention(r3(qr), r3(kr), r3(vr), r3(sgr),
                             jnp.zeros((bp, R_HEADS, R_DIM, R_DIM), F32), RET_TILE, RET_BLOCKS)
        xp = _tail(xp, att.reshape(bp * sp, A_WIDTH), rn.reshape(bp * sp, R_WIDTH), sga, sgb, *tail_w)
        outs[0].append(kt.reshape(bp, ATT_WINDOW, A_HEADS, A_HEAD_DIM))
        outs[1].append(vt_t.transpose(0, 2, 1).reshape(bp, ATT_WINDOW, A_HEADS, A_HEAD_DIM))
        outs[2].append(s_p)

        qa, ka, va, qr, kr, vr, sgr, sga, sgb, kt, vt = _in_proj(
            xs, g0, w_in_b, w_v_t, cs_s, sn_s, seq_rows=bs * ts, tail_rows=bs * ts, v_transposed=False)
        r3 = lambda a: a.reshape(bs, ts, a.shape[-1])
        att = _attn_sample(r3(qa), r3(ka), r3(va),
                           cache_attn_k[l].reshape(bs, lc, A_WIDTH),
                           cache_attn_v[l].reshape(bs, lc, A_WIDTH),
                           bias_s)
        rn, s_s = _retention(r3(qr), r3(kr), r3(vr), r3(sgr),
                             state_retention[l].astype(F32), ts, 1)
        xs = _tail(xs, att.reshape(bs * ts, A_WIDTH), rn.reshape(bs * ts, R_WIDTH), sga, sgb, *tail_w)
        outs[3].append(kt.reshape(bs, ts, A_HEADS, A_HEAD_DIM).astype(cache_attn_k.dtype))
        outs[4].append(vt.reshape(bs, ts, A_HEADS, A_HEAD_DIM).astype(cache_attn_v.dtype))
        outs[5].append(s_s.astype(state_retention.dtype))

    return (xp.reshape(bp, sp, D_MODEL), xs.reshape(bs, ts, D_MODEL),
            jnp.stack(outs[0]), jnp.stack(outs[1]), jnp.stack(outs[2]),
            jnp.stack(outs[3]), jnp.stack(outs[4]), jnp.stack(outs[5]))
```

```python
import functools
import math

import jax
import jax.numpy as jnp
import numpy as np
from jax import lax
from jax.experimental import pallas as pl
from jax.experimental.pallas import tpu as pltpu

F32 = jnp.float32
BF16 = jnp.bfloat16

D_MODEL = 1024
CHUNK = 64
LEFT_CHUNKS = 8
ATT_WINDOW = LEFT_CHUNKS * CHUNK
A_HEADS = 8
A_HEAD_DIM = 64
A_WIDTH = A_HEADS * A_HEAD_DIM
REL_CLIP = 128
R_HEADS = 4
R_DIM = 128
R_WIDTH = R_HEADS * R_DIM
ROPE_BASE = 10000.0
D_FF = 4 * D_MODEL
NORM_EPS = 1e-6
GN_EPS = 1e-6
PAST_LEN = 4096

IN_SPLITS = (A_WIDTH, A_WIDTH, A_WIDTH, R_WIDTH, R_WIDTH, R_WIDTH, R_WIDTH, D_MODEL, D_MODEL)
IN_COLS = sum(IN_SPLITS)
IN_OFFS = tuple(int(o) for o in np.cumsum((0,) + IN_SPLITS))

LANES = 128
MXU_N = 256
ROW_TILE = 1024
SUB_ROWS = 256
ATT_Q_TILE = 256
ATT_KEYS = ATT_WINDOW + ATT_Q_TILE
ATT_QB = 4
ATT_AHEAD = 3
PV_EXTRA_ROWS = 16
RET_TILE = 256
RET_BLOCKS = 4
FF_CHUNK = 1024
VMEM_LIMIT = 56 * 1024 * 1024
NEG_BIG = -1e30
LOG2E = math.log2(math.e)
ATT_Q_SCALE = (A_HEAD_DIM ** -0.5) * LOG2E
LOG_DECAY = tuple(math.log1p(-(2.0 ** (-5 - h))) for h in range(R_HEADS))


def _resident(shape):
    nd = len(shape)
    return pl.BlockSpec(shape, lambda *_: (0,) * nd, pipeline_mode=pl.Buffered(1))


def _rms(x, g):
    ms = jnp.mean(x * x, axis=-1, keepdims=True)
    return x * lax.rsqrt(ms + NORM_EPS) * g


def _inproj_kernel(x_ref, g_ref, w_ref, wvt_ref, cs_ref, sn_ref,
                   qa_ref, ka_ref, va_ref, qr_ref, kr_ref, vr_ref, sgr_ref, sga_ref, sgb_ref,
                   kt_ref, vt_ref, *, v_transposed):
    row_tile, tail_rows = x_ref.shape[0], kt_ref.shape[0]
    tail_start = row_tile - tail_rows
    for r0 in range(0, row_tile, SUB_ROWS):
        rows = slice(r0, r0 + SUB_ROWS)
        in_tail = r0 >= tail_start
        trows = slice(r0 - tail_start, r0 - tail_start + SUB_ROWS)
        h = _rms(x_ref[rows, :], g_ref[...]).astype(BF16)

        def seg(s):
            return jnp.dot(h, w_ref[:, IN_OFFS[s]:IN_OFFS[s + 1]], preferred_element_type=F32)

        cs = cs_ref[rows, :]
        sn = sn_ref[rows, :]

        def rope(z, scale):
            parts = []
            for hh in range(R_HEADS):
                zh = z[:, hh * R_DIM:(hh + 1) * R_DIM]
                r = zh * cs + pltpu.roll(zh, R_DIM // 2, 1) * sn
                if scale != 1.0:
                    r = r * scale
                parts.append(r.astype(BF16))
            return jnp.concatenate(parts, axis=1)

        qa_ref[rows, :] = (seg(0) * ATT_Q_SCALE).astype(BF16)
        ka = seg(1)
        ka_ref[rows, :] = ka.astype(BF16)
        if in_tail:
            kt_ref[trows] = pltpu.einshape("t(hd)->thd", ka, h=A_HEADS)
        if v_transposed:
            va_t = lax.dot_general(wvt_ref[...], h, (((1,), (1,)), ((), ())), preferred_element_type=F32)
            va_ref[r0 // SUB_ROWS] = va_t.astype(BF16)
            if in_tail:
                vt_ref[trows] = pltpu.einshape("t(hd)->thd", va_t.T, h=A_HEADS)
        else:
            va = seg(2)
            va_ref[rows, :] = va.astype(BF16)
            if in_tail:
                vt_ref[trows] = pltpu.einshape("t(hd)->thd", va, h=A_HEADS)
        qr_ref[rows, :] = rope(seg(3), 1.0)
        kr_ref[rows, :] = rope(seg(4), R_DIM ** -0.5)
        vr_ref[rows, :] = seg(5).astype(BF16)
        gr = seg(6)
        sgr_ref[rows, :] = (gr * jax.nn.sigmoid(gr)).astype(BF16)
        sga_ref[rows, :] = jax.nn.sigmoid(seg(7)).astype(BF16)
        sgb_ref[rows, :] = jax.nn.sigmoid(seg(8)).astype(BF16)


def _w_v_transposed(w_in, l):
    assert IN_OFFS[2] % A_WIDTH == 0

    def body(w_ref, o_ref):
        o_ref[...] = w_ref[0].T.astype(BF16)

    return pl.pallas_call(
        body,
        grid=(1,),
        in_specs=[pl.BlockSpec((1, D_MODEL, A_WIDTH), lambda i: (l, 0, IN_OFFS[2] // A_WIDTH))],
        out_specs=pl.BlockSpec((A_WIDTH, D_MODEL), lambda i: (0, 0)),
        out_shape=jax.ShapeDtypeStruct((A_WIDTH, D_MODEL), BF16),
        name="w_v_transpose",
    )(w_in)


def _in_proj(x2, g, w_bf, wvt_bf, cs, sn, seq_rows, tail_rows, v_transposed):
    rows = x2.shape[0]
    row_tile = min(seq_rows, ROW_TILE)
    assert tail_rows <= row_tile and tail_rows % SUB_ROWS == 0 and seq_rows % row_tile == 0
    tail_period = seq_rows // row_tile
    n = rows // row_tile
    n_pos = cs.shape[0] // row_tile
    n_sub = row_tile // SUB_ROWS
    row = lambda w: pl.BlockSpec((row_tile, w), lambda i: (i, 0))
    tail = pl.BlockSpec((tail_rows, A_HEADS, A_HEAD_DIM), lambda i: (i // tail_period, 0, 0))
    pos = pl.BlockSpec((row_tile, R_DIM), lambda i: (i % n_pos, 0))
    bf = lambda w: jax.ShapeDtypeStruct((rows, w), BF16)
    tail_shape = jax.ShapeDtypeStruct((rows // seq_rows * tail_rows, A_HEADS, A_HEAD_DIM), F32)
    if v_transposed:
        v_spec = pl.BlockSpec((n_sub, A_WIDTH, SUB_ROWS), lambda i: (i, 0, 0))
        v_shape = jax.ShapeDtypeStruct((rows // SUB_ROWS, A_WIDTH, SUB_ROWS), BF16)
    else:
        v_spec, v_shape = row(A_WIDTH), bf(A_WIDTH)
    return pl.pallas_call(
        functools.partial(_inproj_kernel, v_transposed=v_transposed),
        grid=(n,),
        in_specs=[row(D_MODEL), _resident((1, D_MODEL)), _resident((D_MODEL, IN_COLS)),
                  _resident((A_WIDTH, D_MODEL)), pos, pos],
        out_specs=[row(A_WIDTH), row(A_WIDTH), v_spec] + [row(R_WIDTH)] * 4 + [row(D_MODEL)] * 2
        + [tail, tail],
        out_shape=[bf(A_WIDTH), bf(A_WIDTH), v_shape] + [bf(R_WIDTH)] * 4 + [bf(D_MODEL)] * 2
        + [tail_shape, tail_shape],
        compiler_params=pltpu.CompilerParams(
            dimension_semantics=("arbitrary",), vmem_limit_bytes=VMEM_LIMIT),
        name="in_proj",
    )(x2, g, w_bf, wvt_bf, cs, sn)


def _attend_heads(q_all, k_all, v_all, bias_of_head):
    tq = q_all.shape[0]
    lane_q = lax.broadcasted_iota(jnp.int32, (tq, LANES), 1)
    lane_o = lax.broadcasted_iota(jnp.int32, (tq, MXU_N), 1)
    heads_per_group = MXU_N // A_HEAD_DIM
    pairs_per_group = heads_per_group // 2
    outs = []
    for grp in range(A_HEADS // heads_per_group):
        v4 = v_all[:, grp * MXU_N:(grp + 1) * MXU_N]
        e_parts, d_parts = [], []
        for pg in range(pairs_per_group):
            pair = grp * pairs_per_group + pg
            q2 = q_all[:, pair * LANES:(pair + 1) * LANES]
            k2 = k_all[:, pair * LANES:(pair + 1) * LANES]
            zero = jnp.zeros_like(q2)
            q_st = jnp.concatenate([jnp.where(lane_q < A_HEAD_DIM, q2, zero),
                                    jnp.where(lane_q >= A_HEAD_DIM, q2, zero)], axis=0)
            s = lax.dot_general(q_st, k2, (((1,), (1,)), ((), ())), preferred_element_type=F32)
            s = s + jnp.concatenate([bias_of_head(2 * pair), bias_of_head(2 * pair + 1)], axis=0)
            m = jnp.max(s, axis=-1, keepdims=True)
            e = jnp.exp2(s - m)
            d_parts.append(jnp.sum(e, axis=-1, keepdims=True))
            e_parts.append(e.astype(BF16))
        o = jnp.dot(jnp.concatenate(e_parts, axis=0), v4, preferred_element_type=F32)
        o = o * (1.0 / jnp.concatenate(d_parts, axis=0))
        acc = jnp.zeros((tq, MXU_N), F32)
        for hg in range(heads_per_group):
            olo = hg * A_HEAD_DIM
            acc = jnp.where((lane_o >= olo) & (lane_o < olo + A_HEAD_DIM), o[hg * tq:(hg + 1) * tq], acc)
        outs.append(acc.astype(BF16))
    return jnp.concatenate(outs, axis=1)


def _reduce_rows(x, op):
    parts = [x[r:r + 8] for r in range(0, x.shape[0], 8)]
    while len(parts) > 1:
        paired = [op(parts[i], parts[i + 1]) for i in range(0, len(parts) - 1, 2)]
        parts = paired + parts[len(parts) - len(parts) % 2:]
    red = jnp.max if op is jnp.maximum else jnp.sum
    return red(parts[0], axis=0, keepdims=True)


def _attn_prompt_kernel(q_ref, k_ref, vt_ref, bmt_ref, o_ref):
    n_c = ATT_KEYS // ATT_Q_TILE
    n_left = n_c - 1
    lane_q = lax.broadcasted_iota(jnp.int32, (ATT_Q_TILE, LANES), 1)
    ones_rows = jnp.ones((PV_EXTRA_ROWS, ATT_Q_TILE), BF16)
    band_keys = (LEFT_CHUNKS + 2) * CHUNK
    n_lt = ATT_Q_TILE // LANES

    def pieces(t):
        lo, hi = LANES * t, LANES * t + band_keys
        out = []
        for c in range(n_c):
            a, b = max(lo, c * ATT_Q_TILE), min(hi, (c + 1) * ATT_Q_TILE)
            if a < b:
                out.append((c, a - c * ATT_Q_TILE, b - c * ATT_Q_TILE))
        return out

    def window(qb):
        j = pl.program_id(1) * ATT_QB + qb
        kc = [jnp.maximum(j - n_left + c, 0) for c in range(n_c)]
        bc = [jnp.where(j - n_left + c >= 0, c, n_c) for c in range(n_c)]
        return kc, bc

    def scores(qb, hd):
        kc, bc = window(qb)
        pair = hd // 2
        q2 = q_ref[0, qb * ATT_Q_TILE:(qb + 1) * ATT_Q_TILE, pair * LANES:(pair + 1) * LANES]
        k2 = jnp.concatenate(
            [k_ref[0, pl.ds(pl.multiple_of(kc[c] * ATT_Q_TILE, ATT_Q_TILE), ATT_Q_TILE),
                   pair * LANES:(pair + 1) * LANES] for c in range(n_c)], axis=0)
        lo = (hd % 2) * A_HEAD_DIM
        qm = jnp.where((lane_q >= lo) & (lane_q < lo + A_HEAD_DIM), q2, jnp.zeros_like(q2))
        st = lax.dot_general(k2, qm, (((1,), (1,)), ((), ())), preferred_element_type=F32)
        tiles = []
        for t in range(n_lt):
            lanes = slice(t * LANES, (t + 1) * LANES)
            tiles.append([st[c * ATT_Q_TILE + a:c * ATT_Q_TILE + b, lanes] + bmt_ref[hd, bc[c], a:b, lanes]
                          for c, a, b in pieces(t)])
        return tiles

    def softmax(tiles):
        cols = []
        for t in range(n_lt):
            st = jnp.concatenate(tiles[t], axis=0)
            m = _reduce_rows(st, jnp.maximum)
            e = jnp.exp2(st - m).astype(BF16)
            zero = lambda n: [jnp.zeros((n, LANES), BF16)] if n else []
            cols.append(jnp.concatenate(zero(LANES * t) + [e] + zero(ATT_KEYS - band_keys - LANES * t), axis=0))
        return jnp.concatenate(cols, axis=1)

    def pv(qb, hd, pt):
        kc, _ = window(qb)
        ot = jnp.zeros((A_HEAD_DIM + PV_EXTRA_ROWS, ATT_Q_TILE), F32)
        for c in range(n_c):
            vt_c = vt_ref[0, kc[c], hd * A_HEAD_DIM:(hd + 1) * A_HEAD_DIM, :]
            ot = ot + jnp.dot(jnp.concatenate([vt_c, ones_rows], axis=0),
                              pt[c * ATT_Q_TILE:(c + 1) * ATT_Q_TILE, :], preferred_element_type=F32)
        return ot[0:A_HEAD_DIM] * (1.0 / ot[A_HEAD_DIM:A_HEAD_DIM + 1])

    items = [(qb, hd) for qb in range(ATT_QB) for hd in range(A_HEADS)]
    tiles = {it: scores(*it) for it in items[:ATT_AHEAD]}
    prev = None
    for n, (qb, hd) in enumerate(items):
        pt = softmax(tiles.pop((qb, hd)))
        if n + ATT_AHEAD < len(items):
            nxt = items[n + ATT_AHEAD]
            tiles[nxt] = scores(*nxt)
        ot = pv(qb, hd, pt)
        if hd % 2 == 0:
            prev = ot
        else:
            pair_t = jnp.concatenate([prev, ot], axis=0).T.astype(BF16)
            o_ref[0, qb * ATT_Q_TILE:(qb + 1) * ATT_Q_TILE, (hd // 2) * LANES:(hd // 2 + 1) * LANES] = pair_t


def _attn_prompt(q, k, vt, bmt):
    b, s, _ = q.shape
    assert vt.shape == (b, s // ATT_Q_TILE, A_WIDTH, ATT_Q_TILE)
    blk = pl.BlockSpec((1, ATT_QB * ATT_Q_TILE, A_WIDTH), lambda bi, j: (bi, j, 0))
    full = pl.BlockSpec((1, s, A_WIDTH), lambda bi, j: (bi, 0, 0))
    full_t = pl.BlockSpec((1,) + vt.shape[1:], lambda bi, j: (bi, 0, 0, 0))
    return pl.pallas_call(
        _attn_prompt_kernel,
        grid=(b, s // (ATT_QB * ATT_Q_TILE)),
        in_specs=[blk, full, full_t, _resident(bmt.shape)],
        out_specs=blk,
        out_shape=jax.ShapeDtypeStruct((b, s, A_WIDTH), BF16),
        compiler_params=pltpu.CompilerParams(
            dimension_semantics=("arbitrary", "arbitrary"), vmem_limit_bytes=VMEM_LIMIT),
        name="attn_prompt",
    )(q, k, vt, bmt)


def _attn_sample_kernel(q_ref, kn_ref, vn_ref, kc_ref, vc_ref, bias_ref, o_ref, *, n_pad):
    zpad = jnp.zeros((n_pad, A_WIDTH), BF16)
    merge = lambda c_ref: pltpu.einshape("khd->k(hd)", c_ref[0]).astype(BF16)
    k_all = jnp.concatenate([merge(kc_ref), kn_ref[0], zpad], axis=0)
    v_all = jnp.concatenate([merge(vc_ref), vn_ref[0], zpad], axis=0)
    o_ref[0] = _attend_heads(q_ref[0], k_all, v_all, lambda hd: bias_ref[hd])


def _attn_sample(q, kn, vn, kc, vc, bias):
    b, t, _ = q.shape
    l = kc.shape[1]
    nk = bias.shape[-1]
    new = pl.BlockSpec((1, t, A_WIDTH), lambda bi: (bi, 0, 0))
    cache = pl.BlockSpec((1, l, A_HEADS, A_HEAD_DIM), lambda bi: (bi, 0, 0, 0))
    return pl.pallas_call(
        functools.partial(_attn_sample_kernel, n_pad=nk - l - t),
        grid=(b,),
        in_specs=[new, new, new, cache, cache, _resident(bias.shape)],
        out_specs=new,
        out_shape=jax.ShapeDtypeStruct((b, t, A_WIDTH), BF16),
        compiler_params=pltpu.CompilerParams(
            dimension_semantics=("arbitrary",), vmem_limit_bytes=VMEM_LIMIT),
        name="attn_sample",
    )(q, kn, vn, kc, vc, bias)


def _retention_kernel(q_ref, k_ref, v_ref, sg_ref, s0_ref, rn_ref, sout_ref,
                      state, dmat, rdec, kdec):
    bi = pl.program_id(0)
    j = pl.program_id(1)
    t = dmat.shape[1]
    n_blk = q_ref.shape[1] // t

    @pl.when((bi == 0) & (j == 0))
    def _():
        row = lax.broadcasted_iota(jnp.int32, (t, t), 0)
        col = lax.broadcasted_iota(jnp.int32, (t, t), 1)
        diff = (row - col).astype(F32)
        rowl = lax.broadcasted_iota(jnp.int32, (t, R_DIM), 0).astype(F32)
        for hh in range(R_HEADS):
            lg = LOG_DECAY[hh]
            dmat[hh] = jnp.where(diff >= 0, jnp.exp(jnp.maximum(diff, 0.0) * lg), 0.0)
            rdec[hh] = jnp.exp((rowl + 1.0) * lg)
            kdec[hh] = jnp.exp((t - 1.0 - rowl) * lg)

    @pl.when(j == 0)
    def _():
        state[...] = s0_ref[0]

    s_cur = [state[hh] for hh in range(R_HEADS)]

    def recur(blk, hh):
        rows, sl = slice(blk * t, (blk + 1) * t), slice(hh * R_DIM, (hh + 1) * R_DIM)
        q = q_ref[0, rows, sl]
        k = k_ref[0, rows, sl]
        v = v_ref[0, rows, sl]
        s_prev = s_cur[hh]
        a = lax.dot_general(q, k, (((1,), (1,)), ((), ())), preferred_element_type=F32) * dmat[hh]
        intra = jnp.dot(a.astype(BF16), v, preferred_element_type=F32)
        cross = jnp.dot(q, s_prev.astype(BF16), preferred_element_type=F32) * rdec[hh]
        kd = (k.astype(F32) * kdec[hh]).astype(BF16)
        upd = lax.dot_general(kd, v, (((0,), (0,)), ((), ())), preferred_element_type=F32)
        s_cur[hh] = s_prev * math.exp(t * LOG_DECAY[hh]) + upd
        return intra + cross

    def norm_gate(blk, hh, o):
        rows, sl = slice(blk * t, (blk + 1) * t), slice(hh * R_DIM, (hh + 1) * R_DIM)
        mu = jnp.mean(o, axis=-1, keepdims=True)
        d = o - mu
        var = jnp.mean(d * d, axis=-1, keepdims=True)
        rn = d * lax.rsqrt(var + GN_EPS) * sg_ref[0, rows, sl].astype(F32)
        rn_ref[0, rows, sl] = rn.astype(BF16)

    items = [(blk, hh) for blk in range(n_blk) for hh in range(R_HEADS)]
    o_next = recur(*items[0])
    for n, it in enumerate(items):
        o = o_next
        if n + 1 < len(items):
            o_next = recur(*items[n + 1])
        norm_gate(*it, o)

    for hh in range(R_HEADS):
        state[hh] = s_cur[hh]
        sout_ref[0, hh] = s_cur[hh]


def _retention(q, k, v, sg, s0, tile, n_blk):
    b, s, _ = q.shape
    blk = pl.BlockSpec((1, n_blk * tile, R_WIDTH), lambda bi, j: (bi, j, 0))
    st = pl.BlockSpec((1, R_HEADS, R_DIM, R_DIM), lambda bi, j: (bi, 0, 0, 0))
    return pl.pallas_call(
        _retention_kernel,
        grid=(b, s // (n_blk * tile)),
        in_specs=[blk, blk, blk, blk, st],
        out_specs=[blk, st],
        out_shape=[jax.ShapeDtypeStruct((b, s, R_WIDTH), BF16),
                   jax.ShapeDtypeStruct((b, R_HEADS, R_DIM, R_DIM), F32)],
        scratch_shapes=[pltpu.VMEM((R_HEADS, R_DIM, R_DIM), F32),
                        pltpu.VMEM((R_HEADS, tile, tile), F32),
                        pltpu.VMEM((R_HEADS, tile, R_DIM), F32),
                        pltpu.VMEM((R_HEADS, tile, R_DIM), F32)],
        compiler_params=pltpu.CompilerParams(
            dimension_semantics=("arbitrary", "arbitrary"), vmem_limit_bytes=VMEM_LIMIT),
        name="retention",
    )(q, k, v, sg, s0)


def _tail_kernel(x_ref, att_ref, rn_ref, sga_ref, sgb_ref, wao_ref, wro_ref, wo_ref,
                 g1_ref, g2_ref, g3_ref, wup_ref, wdn_ref, y_ref):
    n_sub = x_ref.shape[0] // SUB_ROWS
    rows = [slice(r * SUB_ROWS, (r + 1) * SUB_ROWS) for r in range(n_sub)]

    def merged(r):
        ya = jnp.dot(att_ref[r, :], wao_ref[...], preferred_element_type=F32)
        yr = jnp.dot(rn_ref[r, :], wro_ref[...], preferred_element_type=F32)
        return (sga_ref[r, :].astype(F32) * ya + sgb_ref[r, :].astype(F32) * yr).astype(BF16)

    def ffn_chunk(h2, f, c):
        u = jnp.dot(h2, wup_ref[:, c * FF_CHUNK:(c + 1) * FF_CHUNK], preferred_element_type=F32)
        u = jnp.square(jnp.maximum(u, 0.0)).astype(BF16)
        return f + jnp.dot(u, wdn_ref[c * FF_CHUNK:(c + 1) * FF_CHUNK, :], preferred_element_type=F32)

    n_ff = D_FF // FF_CHUNK
    mixed = lambda r: jnp.dot(merged(r), wo_ref[...], preferred_element_type=F32)
    x1, h2, f = [None] * n_sub, [None] * n_sub, [None] * n_sub
    mix_next = mixed(rows[0])
    for s in range(n_sub):
        mix = mix_next
        if s + 1 < n_sub:
            mix_next = mixed(rows[s + 1])
        x1[s] = x_ref[rows[s], :] + _rms(mix, g1_ref[...])
        h2[s] = _rms(x1[s], g2_ref[...]).astype(BF16)
        f[s] = jnp.zeros_like(x1[s])
        if s > 0:
            for c in range(1, n_ff):
                f[s - 1] = ffn_chunk(h2[s - 1], f[s - 1], c)
            y_ref[rows[s - 1], :] = x1[s - 1] + _rms(f[s - 1], g3_ref[...])
        f[s] = ffn_chunk(h2[s], f[s], 0)
    for c in range(1, n_ff):
        f[-1] = ffn_chunk(h2[-1], f[-1], c)
    y_ref[rows[-1], :] = x1[-1] + _rms(f[-1], g3_ref[...])


def _tail(x2, att, rn, sga, sgb, wao, wro, wo, g1, g2, g3, wup, wdn):
    rows = x2.shape[0]
    row_tile = min(rows, ROW_TILE)
    row = lambda w: pl.BlockSpec((row_tile, w), lambda i: (i, 0))
    gspec = _resident((1, D_MODEL))
    return pl.pallas_call(
        _tail_kernel,
        grid=(rows // row_tile,),
        in_specs=[row(D_MODEL), row(A_WIDTH), row(R_WIDTH), row(D_MODEL), row(D_MODEL),
                  _resident(wao.shape), _resident(wro.shape), _resident(wo.shape),
                  gspec, gspec, gspec, _resident(wup.shape), _resident(wdn.shape)],
        out_specs=row(D_MODEL),
        out_shape=jax.ShapeDtypeStruct((rows, D_MODEL), F32),
        compiler_params=pltpu.CompilerParams(
            dimension_semantics=("arbitrary",), vmem_limit_bytes=VMEM_LIMIT),
        name="tail",
    )(x2, att, rn, sga, sgb, wao, wro, wo, g1, g2, g3, wup, wdn)


def _rope_tables(pos):
    half = R_DIM // 2
    inv = ROPE_BASE ** (-jnp.arange(half, dtype=F32) / half)
    ang = pos[:, None] * inv[None, :]
    cos, sin = jnp.cos(ang), jnp.sin(ang)
    return jnp.concatenate([cos, cos], axis=1), jnp.concatenate([-sin, sin], axis=1)


BIAS_ROW = 1024
BIAS_TPAD = 384


def _bias_kernel(t_ref, bpt_ref, bs_ref, *, t_s, nk_valid_s):
    tq = ATT_Q_TILE
    t8 = jnp.broadcast_to(t_ref[0], (8, BIAS_TPAD))
    t1 = t8.astype(BF16)
    r1 = t8 - t1.astype(F32)
    t2 = r1.astype(BF16)
    t3 = (r1 - t2.astype(F32)).astype(BF16)

    def generator(width, offset, sign):
        k_idx = lax.broadcasted_iota(jnp.int32, (BIAS_TPAD, width), 0)
        x_idx = lax.broadcasted_iota(jnp.int32, (BIAS_TPAD, width), 1)
        sel = jnp.where(k_idx == jnp.clip(offset + sign * x_idx, 0, 2 * REL_CLIP), 1.0, 0.0).astype(BF16)
        gen = (jnp.dot(t1, sel, preferred_element_type=F32) + jnp.dot(t2, sel, preferred_element_type=F32)
               + jnp.dot(t3, sel, preferred_element_type=F32))
        return gen[0:1] * LOG2E

    def toeplitz(gen, rows):
        width = gen.shape[1]
        return pltpu.roll(jnp.broadcast_to(gen, (rows, width)), width - tq, 1, stride=1, stride_axis=0)

    qc = lax.shift_right_logical(lax.broadcasted_iota(jnp.int32, (tq, tq), 1), 6)
    key = lax.broadcasted_iota(jnp.int32, (tq, tq), 0)
    for c in range(bpt_ref.shape[1]):
        if c * tq < ATT_KEYS:
            gen = generator(2 * tq, ATT_WINDOW + REL_CLIP - tq - tq * c, 1)
            kc = lax.shift_right_logical(key + c * tq, 6)
            valid = (kc >= qc) & (kc <= qc + LEFT_CHUNKS)
            bpt_ref[0, c] = jnp.where(valid, toeplitz(gen, tq)[:, 0:tq], NEG_BIG)
        else:
            bpt_ref[0, c] = jnp.full((tq, tq), NEG_BIG, F32)
    nk_s = bs_ref.shape[2]
    gen_s = generator(BIAS_ROW, ATT_WINDOW + tq + REL_CLIP, -1)
    lane = lax.broadcasted_iota(jnp.int32, (t_s, nk_s), 1)
    bs_ref[0] = jnp.where(lane < nk_valid_s, toeplitz(gen_s, t_s)[:, 0:nk_s], NEG_BIG)


def _build_bias(table, t_s, l_s, nk_s):
    assert CHUNK == 64 and l_s == ATT_WINDOW and t_s <= ATT_Q_TILE and nk_s <= ATT_KEYS
    hn, tl = table.shape
    tpad = jnp.pad(table.astype(F32), ((0, 0), (0, BIAS_TPAD - tl))).reshape(hn, 1, BIAS_TPAD)
    n_c = ATT_KEYS // ATT_Q_TILE + 1
    return pl.pallas_call(
        functools.partial(_bias_kernel, t_s=t_s, nk_valid_s=l_s + t_s),
        grid=(hn,),
        in_specs=[pl.BlockSpec((1, 1, BIAS_TPAD), lambda h: (h, 0, 0))],
        out_specs=[pl.BlockSpec((1, n_c, ATT_Q_TILE, ATT_Q_TILE), lambda h: (h, 0, 0, 0)),
                   pl.BlockSpec((1, t_s, nk_s), lambda h: (h, 0, 0))],
        out_shape=[jax.ShapeDtypeStruct((hn, n_c, ATT_Q_TILE, ATT_Q_TILE), F32),
                   jax.ShapeDtypeStruct((hn, t_s, nk_s), F32)],
        compiler_params=pltpu.CompilerParams(
            dimension_semantics=("arbitrary",), vmem_limit_bytes=VMEM_LIMIT),
        name="rel_bias",
    )(tpad)


def kernel(x_prompt, x_sample, cache_attn_k, cache_attn_v, state_retention, norm_mix_pre, w_in,
           rel_bias_table, w_attn_out, w_ret_out, w_o, norm_mix_post, norm_ffn_pre, w_ff_up,
           w_ff_down, norm_ffn_post):
    depth = w_in.shape[0]
    bp, sp, _ = x_prompt.shape
    bs, ts, _ = x_sample.shape
    lc = cache_attn_k.shape[2]
    n_keys_s = -(-(lc + ts) // LANES) * LANES

    cs_p, sn_p = _rope_tables(jnp.arange(sp, dtype=F32))
    pos_s = PAST_LEN + jnp.arange(ts, dtype=F32)
    cs_s, sn_s = _rope_tables(jnp.tile(pos_s, bs))

    xp = x_prompt.reshape(bp * sp, D_MODEL)
    xs = x_sample.reshape(bs * ts, D_MODEL)
    outs = [[] for _ in range(6)]
    for l in range(depth):
        w_in_b = w_in[l].astype(BF16)
        w_v_t = _w_v_transposed(w_in, l)
        wao, wro, wo = w_attn_out[l].astype(BF16), w_ret_out[l].astype(BF16), w_o[l].astype(BF16)
        wup, wdn = w_ff_up[l].astype(BF16), w_ff_down[l].astype(BF16)
        g0 = norm_mix_pre[l].reshape(1, D_MODEL)
        g1 = norm_mix_post[l].reshape(1, D_MODEL)
        g2 = norm_ffn_pre[l].reshape(1, D_MODEL)
        g3 = norm_ffn_post[l].reshape(1, D_MODEL)
        tail_w = (wao, wro, wo, g1, g2, g3, wup, wdn)
        bias_p, bias_s = _build_bias(rel_bias_table[l], ts, lc, n_keys_s)

        qa, ka, va_t, qr, kr, vr, sgr, sga, sgb, kt, vt = _in_proj(
            xp, g0, w_in_b, w_v_t, cs_p, sn_p, seq_rows=sp, tail_rows=ATT_WINDOW, v_transposed=True)
        r3 = lambda a: a.reshape(bp, sp, a.shape[-1])
        att = _attn_prompt(r3(qa), r3(ka), va_t.reshape(bp, sp // SUB_ROWS, A_WIDTH, SUB_ROWS), bias_p)
        rn, s_p = _retention(r3(qr), r3(kr), r3(vr), r3(sgr),
                             jnp.zeros((bp, R_HEADS, R_DIM, R_DIM), F32), RET_TILE, RET_BLOCKS)
        xp = _tail(xp, att.reshape(bp * sp, A_WIDTH), rn.reshape(bp * sp, R_WIDTH), sga, sgb, *tail_w)
        outs[0].append(kt.reshape(bp, ATT_WINDOW, A_HEADS, A_HEAD_DIM))
        outs[1].append(vt.reshape(bp, ATT_WINDOW, A_HEADS, A_HEAD_DIM))
        outs[2].append(s_p)

        qa, ka, va, qr, kr, vr, sgr, sga, sgb, kt, vt = _in_proj(
            xs, g0, w_in_b, w_v_t, cs_s, sn_s, seq_rows=bs * ts, tail_rows=bs * ts, v_transposed=False)
        r3 = lambda a: a.reshape(bs, ts, a.shape[-1])
        att = _attn_sample(r3(qa), r3(ka), r3(va),
                           cache_attn_k[l], cache_attn_v[l],
                           bias_s)
        rn, s_s = _retention(r3(qr), r3(kr), r3(vr), r3(sgr),
                             state_retention[l].astype(F32), ts, 1)
        xs = _tail(xs, att.reshape(bs * ts, A_WIDTH), rn.reshape(bs * ts, R_WIDTH), sga, sgb, *tail_w)
        outs[3].append(kt.reshape(bs, ts, A_HEADS, A_HEAD_DIM).astype(cache_attn_k.dtype))
        outs[4].append(vt.reshape(bs, ts, A_HEADS, A_HEAD_DIM).astype(cache_attn_v.dtype))
        outs[5].append(s_s.astype(state_retention.dtype))

    return (xp.reshape(bp, sp, D_MODEL), xs.reshape(bs, ts, D_MODEL),
            jnp.stack(outs[0]), jnp.stack(outs[1]), jnp.stack(outs[2]),
            jnp.stack(outs[3]), jnp.stack(outs[4]), jnp.stack(outs[5]))
```

```python
import functools
import math

import jax
import jax.numpy as jnp
import numpy as np
from jax import lax
from jax.experimental import pallas as pl
from jax.experimental.pallas import tpu as pltpu

F32 = jnp.float32
BF16 = jnp.bfloat16

D_MODEL = 1024
CHUNK = 64
LEFT_CHUNKS = 8
ATT_WINDOW = LEFT_CHUNKS * CHUNK
A_HEADS = 8
A_HEAD_DIM = 64
A_WIDTH = A_HEADS * A_HEAD_DIM
REL_CLIP = 128
R_HEADS = 4
R_DIM = 128
R_WIDTH = R_HEADS * R_DIM
ROPE_BASE = 10000.0
D_FF = 4 * D_MODEL
NORM_EPS = 1e-6
GN_EPS = 1e-6
PAST_LEN = 4096

IN_SPLITS = (A_WIDTH, A_WIDTH, A_WIDTH, R_WIDTH, R_WIDTH, R_WIDTH, R_WIDTH, D_MODEL, D_MODEL)
IN_COLS = sum(IN_SPLITS)
IN_OFFS = tuple(int(o) for o in np.cumsum((0,) + IN_SPLITS))

LANES = 128
MXU_N = 256
ROW_TILE = 1024
SUB_ROWS = 256
ATT_Q_TILE = 256
ATT_KEYS = ATT_WINDOW + ATT_Q_TILE
ATT_QB = 4
ATT_AHEAD = 3
PV_EXTRA_ROWS = 16
RET_TILE = 256
RET_BLOCKS = 4
FF_CHUNK = 1024
VMEM_LIMIT = 56 * 1024 * 1024
NEG_BIG = -1e30
LOG2E = math.log2(math.e)
ATT_Q_SCALE = (A_HEAD_DIM ** -0.5) * LOG2E
LOG_DECAY = tuple(math.log1p(-(2.0 ** (-5 - h))) for h in range(R_HEADS))


def _resident(shape):
    nd = len(shape)
    return pl.BlockSpec(shape, lambda *_: (0,) * nd, pipeline_mode=pl.Buffered(1))


def _rms(x, g):
    ms = jnp.mean(x * x, axis=-1, keepdims=True)
    return x * lax.rsqrt(ms + NORM_EPS) * g


def _inproj_kernel(x_ref, g_ref, w_ref, wvt_ref, cs_ref, sn_ref,
                   qa_ref, ka_ref, va_ref, qr_ref, kr_ref, vr_ref, sgr_ref, sga_ref, sgb_ref,
                   kt_ref, vt_ref, *, v_transposed):
    row_tile, tail_rows = x_ref.shape[0], kt_ref.shape[0]
    tail_start = row_tile - tail_rows
    for r0 in range(0, row_tile, SUB_ROWS):
        rows = slice(r0, r0 + SUB_ROWS)
        in_tail = r0 >= tail_start
        trows = slice(r0 - tail_start, r0 - tail_start + SUB_ROWS)
        h = _rms(x_ref[rows, :], g_ref[...]).astype(BF16)

        def seg(s):
            return jnp.dot(h, w_ref[:, IN_OFFS[s]:IN_OFFS[s + 1]], preferred_element_type=F32)

        cs = cs_ref[rows, :]
        sn = sn_ref[rows, :]

        def rope(z, scale):
            parts = []
            for hh in range(R_HEADS):
                zh = z[:, hh * R_DIM:(hh + 1) * R_DIM]
                r = zh * cs + pltpu.roll(zh, R_DIM // 2, 1) * sn
                if scale != 1.0:
                    r = r * scale
                parts.append(r.astype(BF16))
            return jnp.concatenate(parts, axis=1)

        qa_ref[rows, :] = (seg(0) * ATT_Q_SCALE).astype(BF16)
        ka = seg(1)
        ka_ref[rows, :] = ka.astype(BF16)
        if in_tail:
            kt_ref[trows] = ka.reshape(SUB_ROWS, A_HEADS, A_HEAD_DIM)
        if v_transposed:
            va_t = lax.dot_general(wvt_ref[...], h, (((1,), (1,)), ((), ())), preferred_element_type=F32)
            va_ref[r0 // SUB_ROWS] = va_t.astype(BF16)
            if in_tail:
                vt_ref[trows] = va_t.T.reshape(SUB_ROWS, A_HEADS, A_HEAD_DIM)
        else:
            va = seg(2)
            va_ref[rows, :] = va.astype(BF16)
            if in_tail:
                vt_ref[trows] = va.reshape(SUB_ROWS, A_HEADS, A_HEAD_DIM)
        qr_ref[rows, :] = rope(seg(3), 1.0)
        kr_ref[rows, :] = rope(seg(4), R_DIM ** -0.5)
        vr_ref[rows, :] = seg(5).astype(BF16)
        gr = seg(6)
        sgr_ref[rows, :] = (gr * jax.nn.sigmoid(gr)).astype(BF16)
        sga_ref[rows, :] = jax.nn.sigmoid(seg(7)).astype(BF16)
        sgb_ref[rows, :] = jax.nn.sigmoid(seg(8)).astype(BF16)


def _w_v_transposed(w_in, l):
    assert IN_OFFS[2] % A_WIDTH == 0

    def body(w_ref, o_ref):
        o_ref[...] = w_ref[0].T.astype(BF16)

    return pl.pallas_call(
        body,
        grid=(1,),
        in_specs=[pl.BlockSpec((1, D_MODEL, A_WIDTH), lambda i: (l, 0, IN_OFFS[2] // A_WIDTH))],
        out_specs=pl.BlockSpec((A_WIDTH, D_MODEL), lambda i: (0, 0)),
        out_shape=jax.ShapeDtypeStruct((A_WIDTH, D_MODEL), BF16),
        name="w_v_transpose",
    )(w_in)


def _in_proj(x2, g, w_bf, wvt_bf, cs, sn, seq_rows, tail_rows, v_transposed):
    rows = x2.shape[0]
    row_tile = min(seq_rows, ROW_TILE)
    assert tail_rows <= row_tile and tail_rows % SUB_ROWS == 0 and seq_rows % row_tile == 0
    tail_period = seq_rows // row_tile
    n = rows // row_tile
    n_pos = cs.shape[0] // row_tile
    n_sub = row_tile // SUB_ROWS
    row = lambda w: pl.BlockSpec((row_tile, w), lambda i: (i, 0))
    tail = pl.BlockSpec((tail_rows, A_HEADS, A_HEAD_DIM), lambda i: (i // tail_period, 0, 0))
    pos = pl.BlockSpec((row_tile, R_DIM), lambda i: (i % n_pos, 0))
    bf = lambda w: jax.ShapeDtypeStruct((rows, w), BF16)
    tail_shape = jax.ShapeDtypeStruct((rows // seq_rows * tail_rows, A_HEADS, A_HEAD_DIM), F32)
    if v_transposed:
        v_spec = pl.BlockSpec((n_sub, A_WIDTH, SUB_ROWS), lambda i: (i, 0, 0))
        v_shape = jax.ShapeDtypeStruct((rows // SUB_ROWS, A_WIDTH, SUB_ROWS), BF16)
    else:
        v_spec, v_shape = row(A_WIDTH), bf(A_WIDTH)
    return pl.pallas_call(
        functools.partial(_inproj_kernel, v_transposed=v_transposed),
        grid=(n,),
        in_specs=[row(D_MODEL), _resident((1, D_MODEL)), _resident((D_MODEL, IN_COLS)),
                  _resident((A_WIDTH, D_MODEL)), pos, pos],
        out_specs=[row(A_WIDTH), row(A_WIDTH), v_spec] + [row(R_WIDTH)] * 4 + [row(D_MODEL)] * 2
        + [tail, tail],
        out_shape=[bf(A_WIDTH), bf(A_WIDTH), v_shape] + [bf(R_WIDTH)] * 4 + [bf(D_MODEL)] * 2
        + [tail_shape, tail_shape],
        compiler_params=pltpu.CompilerParams(
            dimension_semantics=("arbitrary",), vmem_limit_bytes=VMEM_LIMIT),
        name="in_proj",
    )(x2, g, w_bf, wvt_bf, cs, sn)


def _attn_sample_kernel(q_ref, kn_ref, vn_ref, kct_ref, vct_ref, bias_ref, o_ref):
    tq = q_ref.shape[1]
    lc = kct_ref.shape[3]
    n_pad = bias_ref.shape[2] - lc
    q_all = q_ref[0]
    zrows = jnp.zeros((n_pad - tq, A_WIDTH), BF16)
    kn_pad = jnp.concatenate([kn_ref[0], zrows], axis=0)
    vn_pad = jnp.concatenate([vn_ref[0], zrows], axis=0)
    lane_q = lax.broadcasted_iota(jnp.int32, (tq, LANES), 1)
    lane_o = lax.broadcasted_iota(jnp.int32, (tq, MXU_N), 1)
    heads_per_group = MXU_N // A_HEAD_DIM
    pairs_per_group = heads_per_group // 2
    nt = (((1,), (1,)), ((), ()))
    outs = []
    for grp in range(A_HEADS // heads_per_group):
        e_parts, d_parts = [], []
        for pg in range(pairs_per_group):
            pair = grp * pairs_per_group + pg
            q2 = q_all[:, pair * LANES:(pair + 1) * LANES]
            zero = jnp.zeros_like(q2)
            q_st = jnp.concatenate([jnp.where(lane_q < A_HEAD_DIM, q2, zero),
                                    jnp.where(lane_q >= A_HEAD_DIM, q2, zero)], axis=0)
            kt_pair = kct_ref[0, 2 * pair:2 * pair + 2].reshape(LANES, lc).astype(BF16)
            s_c = jnp.dot(q_st, kt_pair, preferred_element_type=F32)
            s_n = lax.dot_general(q_st, kn_pad[:, pair * LANES:(pair + 1) * LANES], nt,
                                  preferred_element_type=F32)
            s = jnp.concatenate([s_c, s_n], axis=1)
            s = s + jnp.concatenate([bias_ref[2 * pair], bias_ref[2 * pair + 1]], axis=0)
            m = jnp.max(s, axis=-1, keepdims=True)
            e = jnp.exp2(s - m)
            d_parts.append(jnp.sum(e, axis=-1, keepdims=True))
            e_parts.append(e.astype(BF16))
        p_st = jnp.concatenate(e_parts, axis=0)
        vt_grp = vct_ref[0, heads_per_group * grp:heads_per_group * (grp + 1)].reshape(MXU_N, lc).astype(BF16)
        o = lax.dot_general(p_st[:, 0:lc], vt_grp, nt, preferred_element_type=F32)
        o = o + jnp.dot(p_st[:, lc:], vn_pad[:, grp * MXU_N:(grp + 1) * MXU_N], preferred_element_type=F32)
        o = o * (1.0 / jnp.concatenate(d_parts, axis=0))
        acc = jnp.zeros((tq, MXU_N), F32)
        for hg in range(heads_per_group):
            olo = hg * A_HEAD_DIM
            acc = jnp.where((lane_o >= olo) & (lane_o < olo + A_HEAD_DIM), o[hg * tq:(hg + 1) * tq], acc)
        outs.append(acc.astype(BF16))
    o_ref[0] = jnp.concatenate(outs, axis=1)


def _reduce_rows(x, op):
    parts = [x[r:r + 8] for r in range(0, x.shape[0], 8)]
    while len(parts) > 1:
        paired = [op(parts[i], parts[i + 1]) for i in range(0, len(parts) - 1, 2)]
        parts = paired + parts[len(parts) - len(parts) % 2:]
    red = jnp.max if op is jnp.maximum else jnp.sum
    return red(parts[0], axis=0, keepdims=True)


def _attn_prompt_kernel(q_ref, k_ref, vt_ref, bmt_ref, o_ref):
    n_c = ATT_KEYS // ATT_Q_TILE
    n_left = n_c - 1
    lane_q = lax.broadcasted_iota(jnp.int32, (ATT_Q_TILE, LANES), 1)
    ones_rows = jnp.ones((PV_EXTRA_ROWS, ATT_Q_TILE), BF16)
    band_keys = (LEFT_CHUNKS + 2) * CHUNK
    n_lt = ATT_Q_TILE // LANES

    def pieces(t):
        lo, hi = LANES * t, LANES * t + band_keys
        out = []
        for c in range(n_c):
            a, b = max(lo, c * ATT_Q_TILE), min(hi, (c + 1) * ATT_Q_TILE)
            if a < b:
                out.append((c, a - c * ATT_Q_TILE, b - c * ATT_Q_TILE))
        return out

    def window(qb):
        j = pl.program_id(1) * ATT_QB + qb
        kc = [jnp.maximum(j - n_left + c, 0) for c in range(n_c)]
        bc = [jnp.where(j - n_left + c >= 0, c, n_c) for c in range(n_c)]
        return kc, bc

    def scores(qb, hd):
        kc, bc = window(qb)
        pair = hd // 2
        q2 = q_ref[0, qb * ATT_Q_TILE:(qb + 1) * ATT_Q_TILE, pair * LANES:(pair + 1) * LANES]
        k2 = jnp.concatenate(
            [k_ref[0, pl.ds(pl.multiple_of(kc[c] * ATT_Q_TILE, ATT_Q_TILE), ATT_Q_TILE),
                   pair * LANES:(pair + 1) * LANES] for c in range(n_c)], axis=0)
        lo = (hd % 2) * A_HEAD_DIM
        qm = jnp.where((lane_q >= lo) & (lane_q < lo + A_HEAD_DIM), q2, jnp.zeros_like(q2))
        st = lax.dot_general(k2, qm, (((1,), (1,)), ((), ())), preferred_element_type=F32)
        tiles = []
        for t in range(n_lt):
            lanes = slice(t * LANES, (t + 1) * LANES)
            tiles.append([st[c * ATT_Q_TILE + a:c * ATT_Q_TILE + b, lanes] + bmt_ref[hd, bc[c], a:b, lanes]
                          for c, a, b in pieces(t)])
        return tiles

    def softmax(tiles):
        cols = []
        for t in range(n_lt):
            st = jnp.concatenate(tiles[t], axis=0)
            m = _reduce_rows(st, jnp.maximum)
            e = jnp.exp2(st - m).astype(BF16)
            zero = lambda n: [jnp.zeros((n, LANES), BF16)] if n else []
            cols.append(jnp.concatenate(zero(LANES * t) + [e] + zero(ATT_KEYS - band_keys - LANES * t), axis=0))
        return jnp.concatenate(cols, axis=1)

    def pv(qb, hd, pt):
        kc, _ = window(qb)
        ot = jnp.zeros((A_HEAD_DIM + PV_EXTRA_ROWS, ATT_Q_TILE), F32)
        for c in range(n_c):
            vt_c = vt_ref[0, kc[c], hd * A_HEAD_DIM:(hd + 1) * A_HEAD_DIM, :]
            ot = ot + jnp.dot(jnp.concatenate([vt_c, ones_rows], axis=0),
                              pt[c * ATT_Q_TILE:(c + 1) * ATT_Q_TILE, :], preferred_element_type=F32)
        return ot[0:A_HEAD_DIM] * (1.0 / ot[A_HEAD_DIM:A_HEAD_DIM + 1])

    items = [(qb, hd) for qb in range(ATT_QB) for hd in range(A_HEADS)]
    tiles = {it: scores(*it) for it in items[:ATT_AHEAD]}
    prev = None
    for n, (qb, hd) in enumerate(items):
        pt = softmax(tiles.pop((qb, hd)))
        if n + ATT_AHEAD < len(items):
            nxt = items[n + ATT_AHEAD]
            tiles[nxt] = scores(*nxt)
        ot = pv(qb, hd, pt)
        if hd % 2 == 0:
            prev = ot
        else:
            pair_t = jnp.concatenate([prev, ot], axis=0).T.astype(BF16)
            o_ref[0, qb * ATT_Q_TILE:(qb + 1) * ATT_Q_TILE, (hd // 2) * LANES:(hd // 2 + 1) * LANES] = pair_t


def _attn_prompt(q, k, vt, bmt):
    b, s, _ = q.shape
    assert vt.shape == (b, s // ATT_Q_TILE, A_WIDTH, ATT_Q_TILE)
    blk = pl.BlockSpec((1, ATT_QB * ATT_Q_TILE, A_WIDTH), lambda bi, j: (bi, j, 0))
    full = pl.BlockSpec((1, s, A_WIDTH), lambda bi, j: (bi, 0, 0))
    full_t = pl.BlockSpec((1,) + vt.shape[1:], lambda bi, j: (bi, 0, 0, 0))
    return pl.pallas_call(
        _attn_prompt_kernel,
        grid=(b, s // (ATT_QB * ATT_Q_TILE)),
        in_specs=[blk, full, full_t, _resident(bmt.shape)],
        out_specs=blk,
        out_shape=jax.ShapeDtypeStruct((b, s, A_WIDTH), BF16),
        compiler_params=pltpu.CompilerParams(
            dimension_semantics=("arbitrary", "arbitrary"), vmem_limit_bytes=VMEM_LIMIT),
        name="attn_prompt",
    )(q, k, vt, bmt)


def _attn_sample(q, kn, vn, kct, vct, bias):
    b, t, _ = q.shape
    new = pl.BlockSpec((1, t, A_WIDTH), lambda bi: (bi, 0, 0))
    cache = pl.BlockSpec((1,) + kct.shape[1:], lambda bi: (bi, 0, 0, 0))
    return pl.pallas_call(
        _attn_sample_kernel,
        grid=(b,),
        in_specs=[new, new, new, cache, cache, _resident(bias.shape)],
        out_specs=new,
        out_shape=jax.ShapeDtypeStruct((b, t, A_WIDTH), BF16),
        compiler_params=pltpu.CompilerParams(
            dimension_semantics=("arbitrary",), vmem_limit_bytes=VMEM_LIMIT),
        name="attn_sample",
    )(q, kn, vn, kct, vct, bias)


def _retention_kernel(q_ref, k_ref, v_ref, sg_ref, s0_ref, rn_ref, sout_ref,
                      state, dmat, rdec, kdec):
    bi = pl.program_id(0)
    j = pl.program_id(1)
    t = dmat.shape[1]
    n_blk = q_ref.shape[1] // t

    @pl.when((bi == 0) & (j == 0))
    def _():
        row = lax.broadcasted_iota(jnp.int32, (t, t), 0)
        col = lax.broadcasted_iota(jnp.int32, (t, t), 1)
        diff = (row - col).astype(F32)
        rowl = lax.broadcasted_iota(jnp.int32, (t, R_DIM), 0).astype(F32)
        for hh in range(R_HEADS):
            lg = LOG_DECAY[hh]
            dmat[hh] = jnp.where(diff >= 0, jnp.exp(jnp.maximum(diff, 0.0) * lg), 0.0)
            rdec[hh] = jnp.exp((rowl + 1.0) * lg)
            kdec[hh] = jnp.exp((t - 1.0 - rowl) * lg)

    @pl.when(j == 0)
    def _():
        state[...] = s0_ref[0]

    s_cur = [state[hh] for hh in range(R_HEADS)]

    def recur(blk, hh):
        rows, sl = slice(blk * t, (blk + 1) * t), slice(hh * R_DIM, (hh + 1) * R_DIM)
        q = q_ref[0, rows, sl]
        k = k_ref[0, rows, sl]
        v = v_ref[0, rows, sl]
        s_prev = s_cur[hh]
        a = lax.dot_general(q, k, (((1,), (1,)), ((), ())), preferred_element_type=F32) * dmat[hh]
        intra = jnp.dot(a.astype(BF16), v, preferred_element_type=F32)
        cross = jnp.dot(q, s_prev.astype(BF16), preferred_element_type=F32) * rdec[hh]
        kd = (k.astype(F32) * kdec[hh]).astype(BF16)
        upd = lax.dot_general(kd, v, (((0,), (0,)), ((), ())), preferred_element_type=F32)
        s_cur[hh] = s_prev * math.exp(t * LOG_DECAY[hh]) + upd
        return intra + cross

    def norm_gate(blk, hh, o):
        rows, sl = slice(blk * t, (blk + 1) * t), slice(hh * R_DIM, (hh + 1) * R_DIM)
        mu = jnp.mean(o, axis=-1, keepdims=True)
        d = o - mu
        var = jnp.mean(d * d, axis=-1, keepdims=True)
        rn = d * lax.rsqrt(var + GN_EPS) * sg_ref[0, rows, sl].astype(F32)
        rn_ref[0, rows, sl] = rn.astype(BF16)

    items = [(blk, hh) for blk in range(n_blk) for hh in range(R_HEADS)]
    o_next = recur(*items[0])
    for n, it in enumerate(items):
        o = o_next
        if n + 1 < len(items):
            o_next = recur(*items[n + 1])
        norm_gate(*it, o)

    for hh in range(R_HEADS):
        state[hh] = s_cur[hh]
        sout_ref[0, hh] = s_cur[hh]


def _retention(q, k, v, sg, s0, tile, n_blk):
    b, s, _ = q.shape
    blk = pl.BlockSpec((1, n_blk * tile, R_WIDTH), lambda bi, j: (bi, j, 0))
    st = pl.BlockSpec((1, R_HEADS, R_DIM, R_DIM), lambda bi, j: (bi, 0, 0, 0))
    return pl.pallas_call(
        _retention_kernel,
        grid=(b, s // (n_blk * tile)),
        in_specs=[blk, blk, blk, blk, st],
        out_specs=[blk, st],
        out_shape=[jax.ShapeDtypeStruct((b, s, R_WIDTH), BF16),
                   jax.ShapeDtypeStruct((b, R_HEADS, R_DIM, R_DIM), F32)],
        scratch_shapes=[pltpu.VMEM((R_HEADS, R_DIM, R_DIM), F32),
                        pltpu.VMEM((R_HEADS, tile, tile), F32),
                        pltpu.VMEM((R_HEADS, tile, R_DIM), F32),
                        pltpu.VMEM((R_HEADS, tile, R_DIM), F32)],
        compiler_params=pltpu.CompilerParams(
            dimension_semantics=("arbitrary", "arbitrary"), vmem_limit_bytes=VMEM_LIMIT),
        name="retention",
    )(q, k, v, sg, s0)


def _tail_kernel(x_ref, att_ref, rn_ref, sga_ref, sgb_ref, wao_ref, wro_ref, wo_ref,
                 g1_ref, g2_ref, g3_ref, wup_ref, wdn_ref, y_ref):
    n_sub = x_ref.shape[0] // SUB_ROWS
    rows = [slice(r * SUB_ROWS, (r + 1) * SUB_ROWS) for r in range(n_sub)]

    def merged(r):
        ya = jnp.dot(att_ref[r, :], wao_ref[...], preferred_element_type=F32)
        yr = jnp.dot(rn_ref[r, :], wro_ref[...], preferred_element_type=F32)
        return (sga_ref[r, :].astype(F32) * ya + sgb_ref[r, :].astype(F32) * yr).astype(BF16)

    def ffn_chunk(h2, f, c):
        u = jnp.dot(h2, wup_ref[:, c * FF_CHUNK:(c + 1) * FF_CHUNK], preferred_element_type=F32)
        u = jnp.square(jnp.maximum(u, 0.0)).astype(BF16)
        return f + jnp.dot(u, wdn_ref[c * FF_CHUNK:(c + 1) * FF_CHUNK, :], preferred_element_type=F32)

    n_ff = D_FF // FF_CHUNK
    mixed = lambda r: jnp.dot(merged(r), wo_ref[...], preferred_element_type=F32)
    x1, h2, f = [None] * n_sub, [None] * n_sub, [None] * n_sub
    mix_next = mixed(rows[0])
    for s in range(n_sub):
        mix = mix_next
        if s + 1 < n_sub:
            mix_next = mixed(rows[s + 1])
        x1[s] = x_ref[rows[s], :] + _rms(mix, g1_ref[...])
        h2[s] = _rms(x1[s], g2_ref[...]).astype(BF16)
        f[s] = jnp.zeros_like(x1[s])
        if s > 0:
            for c in range(1, n_ff):
                f[s - 1] = ffn_chunk(h2[s - 1], f[s - 1], c)
            y_ref[rows[s - 1], :] = x1[s - 1] + _rms(f[s - 1], g3_ref[...])
        f[s] = ffn_chunk(h2[s], f[s], 0)
    for c in range(1, n_ff):
        f[-1] = ffn_chunk(h2[-1], f[-1], c)
    y_ref[rows[-1], :] = x1[-1] + _rms(f[-1], g3_ref[...])


def _tail(x2, att, rn, sga, sgb, wao, wro, wo, g1, g2, g3, wup, wdn):
    rows = x2.shape[0]
    row_tile = min(rows, ROW_TILE)
    row = lambda w: pl.BlockSpec((row_tile, w), lambda i: (i, 0))
    gspec = _resident((1, D_MODEL))
    return pl.pallas_call(
        _tail_kernel,
        grid=(rows // row_tile,),
        in_specs=[row(D_MODEL), row(A_WIDTH), row(R_WIDTH), row(D_MODEL), row(D_MODEL),
                  _resident(wao.shape), _resident(wro.shape), _resident(wo.shape),
                  gspec, gspec, gspec, _resident(wup.shape), _resident(wdn.shape)],
        out_specs=row(D_MODEL),
        out_shape=jax.ShapeDtypeStruct((rows, D_MODEL), F32),
        compiler_params=pltpu.CompilerParams(
            dimension_semantics=("arbitrary",), vmem_limit_bytes=VMEM_LIMIT),
        name="tail",
    )(x2, att, rn, sga, sgb, wao, wro, wo, g1, g2, g3, wup, wdn)


def _rope_tables(pos):
    half = R_DIM // 2
    inv = ROPE_BASE ** (-jnp.arange(half, dtype=F32) / half)
    ang = pos[:, None] * inv[None, :]
    cos, sin = jnp.cos(ang), jnp.sin(ang)
    return jnp.concatenate([cos, cos], axis=1), jnp.concatenate([-sin, sin], axis=1)


BIAS_ROW = 1024
BIAS_TPAD = 384


def _bias_kernel(t_ref, bpt_ref, bs_ref, *, t_s, nk_valid_s):
    tq = ATT_Q_TILE
    t8 = jnp.broadcast_to(t_ref[0], (8, BIAS_TPAD))
    t1 = t8.astype(BF16)
    r1 = t8 - t1.astype(F32)
    t2 = r1.astype(BF16)
    t3 = (r1 - t2.astype(F32)).astype(BF16)

    def generator(width, offset, sign):
        k_idx = lax.broadcasted_iota(jnp.int32, (BIAS_TPAD, width), 0)
        x_idx = lax.broadcasted_iota(jnp.int32, (BIAS_TPAD, width), 1)
        sel = jnp.where(k_idx == jnp.clip(offset + sign * x_idx, 0, 2 * REL_CLIP), 1.0, 0.0).astype(BF16)
        gen = (jnp.dot(t1, sel, preferred_element_type=F32) + jnp.dot(t2, sel, preferred_element_type=F32)
               + jnp.dot(t3, sel, preferred_element_type=F32))
        return gen[0:1] * LOG2E

    def toeplitz(gen, rows):
        width = gen.shape[1]
        return pltpu.roll(jnp.broadcast_to(gen, (rows, width)), width - tq, 1, stride=1, stride_axis=0)

    qc = lax.shift_right_logical(lax.broadcasted_iota(jnp.int32, (tq, tq), 1), 6)
    key = lax.broadcasted_iota(jnp.int32, (tq, tq), 0)
    for c in range(bpt_ref.shape[1]):
        if c * tq < ATT_KEYS:
            gen = generator(2 * tq, ATT_WINDOW + REL_CLIP - tq - tq * c, 1)
            kc = lax.shift_right_logical(key + c * tq, 6)
            valid = (kc >= qc) & (kc <= qc + LEFT_CHUNKS)
            bpt_ref[0, c] = jnp.where(valid, toeplitz(gen, tq)[:, 0:tq], NEG_BIG)
        else:
            bpt_ref[0, c] = jnp.full((tq, tq), NEG_BIG, F32)
    nk_s = bs_ref.shape[2]
    gen_s = generator(BIAS_ROW, ATT_WINDOW + tq + REL_CLIP, -1)
    lane = lax.broadcasted_iota(jnp.int32, (t_s, nk_s), 1)
    bs_ref[0] = jnp.where(lane < nk_valid_s, toeplitz(gen_s, t_s)[:, 0:nk_s], NEG_BIG)


def _build_bias(table, t_s, l_s, nk_s):
    assert CHUNK == 64 and l_s == ATT_WINDOW and t_s <= ATT_Q_TILE and nk_s <= ATT_KEYS
    hn, tl = table.shape
    tpad = jnp.pad(table.astype(F32), ((0, 0), (0, BIAS_TPAD - tl))).reshape(hn, 1, BIAS_TPAD)
    n_c = ATT_KEYS // ATT_Q_TILE + 1
    return pl.pallas_call(
        functools.partial(_bias_kernel, t_s=t_s, nk_valid_s=l_s + t_s),
        grid=(hn,),
        in_specs=[pl.BlockSpec((1, 1, BIAS_TPAD), lambda h: (h, 0, 0))],
        out_specs=[pl.BlockSpec((1, n_c, ATT_Q_TILE, ATT_Q_TILE), lambda h: (h, 0, 0, 0)),
                   pl.BlockSpec((1, t_s, nk_s), lambda h: (h, 0, 0))],
        out_shape=[jax.ShapeDtypeStruct((hn, n_c, ATT_Q_TILE, ATT_Q_TILE), F32),
                   jax.ShapeDtypeStruct((hn, t_s, nk_s), F32)],
        compiler_params=pltpu.CompilerParams(
            dimension_semantics=("arbitrary",), vmem_limit_bytes=VMEM_LIMIT),
        name="rel_bias",
    )(tpad)


def kernel(x_prompt, x_sample, cache_attn_k, cache_attn_v, state_retention, norm_mix_pre, w_in,
           rel_bias_table, w_attn_out, w_ret_out, w_o, norm_mix_post, norm_ffn_pre, w_ff_up,
           w_ff_down, norm_ffn_post):
    depth = w_in.shape[0]
    bp, sp, _ = x_prompt.shape
    bs, ts, _ = x_sample.shape
    lc = cache_attn_k.shape[2]
    n_keys_s = -(-(lc + ts) // LANES) * LANES

    cs_p, sn_p = _rope_tables(jnp.arange(sp, dtype=F32))
    pos_s = PAST_LEN + jnp.arange(ts, dtype=F32)
    cs_s, sn_s = _rope_tables(jnp.tile(pos_s, bs))

    xp = x_prompt.reshape(bp * sp, D_MODEL)
    xs = x_sample.reshape(bs * ts, D_MODEL)
    outs = [[] for _ in range(6)]
    for l in range(depth):
        w_in_b = w_in[l].astype(BF16)
        w_v_t = _w_v_transposed(w_in, l)
        wao, wro, wo = w_attn_out[l].astype(BF16), w_ret_out[l].astype(BF16), w_o[l].astype(BF16)
        wup, wdn = w_ff_up[l].astype(BF16), w_ff_down[l].astype(BF16)
        g0 = norm_mix_pre[l].reshape(1, D_MODEL)
        g1 = norm_mix_post[l].reshape(1, D_MODEL)
        g2 = norm_ffn_pre[l].reshape(1, D_MODEL)
        g3 = norm_ffn_post[l].reshape(1, D_MODEL)
        tail_w = (wao, wro, wo, g1, g2, g3, wup, wdn)
        bias_p, bias_s = _build_bias(rel_bias_table[l], ts, lc, n_keys_s)

        qa, ka, va_t, qr, kr, vr, sgr, sga, sgb, kt, vt = _in_proj(
            xp, g0, w_in_b, w_v_t, cs_p, sn_p, seq_rows=sp, tail_rows=ATT_WINDOW, v_transposed=True)
        r3 = lambda a: a.reshape(bp, sp, a.shape[-1])
        att = _attn_prompt(r3(qa), r3(ka), va_t.reshape(bp, sp // SUB_ROWS, A_WIDTH, SUB_ROWS), bias_p)
        rn, s_p = _retention(r3(qr), r3(kr), r3(vr), r3(sgr),
                             jnp.zeros((bp, R_HEADS, R_DIM, R_DIM), F32), RET_TILE, RET_BLOCKS)
        xp = _tail(xp, att.reshape(bp * sp, A_WIDTH), rn.reshape(bp * sp, R_WIDTH), sga, sgb, *tail_w)
        outs[0].append(kt.reshape(bp, ATT_WINDOW, A_HEADS, A_HEAD_DIM))
        outs[1].append(vt.reshape(bp, ATT_WINDOW, A_HEADS, A_HEAD_DIM))
        outs[2].append(s_p)

        qa, ka, va, qr, kr, vr, sgr, sga, sgb, kt, vt = _in_proj(
            xs, g0, w_in_b, w_v_t, cs_s, sn_s, seq_rows=bs * ts, tail_rows=bs * ts, v_transposed=False)
        r3 = lambda a: a.reshape(bs, ts, a.shape[-1])
        att = _attn_sample(r3(qa), r3(ka), r3(va),
                           cache_attn_k[l].transpose(0, 2, 3, 1), cache_attn_v[l].transpose(0, 2, 3, 1),
                           bias_s)
        rn, s_s = _retention(r3(qr), r3(kr), r3(vr), r3(sgr),
                             state_retention[l].astype(F32), ts, 1)
        xs = _tail(xs, att.reshape(bs * ts, A_WIDTH), rn.reshape(bs * ts, R_WIDTH), sga, sgb, *tail_w)
        outs[3].append(kt.reshape(bs, ts, A_HEADS, A_HEAD_DIM).astype(cache_attn_k.dtype))
        outs[4].append(vt.reshape(bs, ts, A_HEADS, A_HEAD_DIM).astype(cache_attn_v.dtype))
        outs[5].append(s_s.astype(state_retention.dtype))

    return (xp.reshape(bp, sp, D_MODEL), xs.reshape(bs, ts, D_MODEL),
            jnp.stack(outs[0]), jnp.stack(outs[1]), jnp.stack(outs[2]),
            jnp.stack(outs[3]), jnp.stack(outs[4]), jnp.stack(outs[5]))
```

```python
import functools
import math

import jax
import jax.numpy as jnp
import numpy as np
from jax import lax
from jax.experimental import pallas as pl
from jax.experimental.pallas import tpu as pltpu

F32 = jnp.float32
BF16 = jnp.bfloat16

D_MODEL = 1024
CHUNK = 64
LEFT_CHUNKS = 8
ATT_WINDOW = LEFT_CHUNKS * CHUNK
A_HEADS = 8
A_HEAD_DIM = 64
A_WIDTH = A_HEADS * A_HEAD_DIM
REL_CLIP = 128
R_HEADS = 4
R_DIM = 128
R_WIDTH = R_HEADS * R_DIM
ROPE_BASE = 10000.0
D_FF = 4 * D_MODEL
NORM_EPS = 1e-6
GN_EPS = 1e-6
PAST_LEN = 4096

IN_SPLITS = (A_WIDTH, A_WIDTH, A_WIDTH, R_WIDTH, R_WIDTH, R_WIDTH, R_WIDTH, D_MODEL, D_MODEL)
IN_COLS = sum(IN_SPLITS)
IN_OFFS = tuple(int(o) for o in np.cumsum((0,) + IN_SPLITS))

LANES = 128
MXU_N = 256
ROW_TILE = 1024
SUB_ROWS = 256
ATT_Q_TILE = 256
ATT_KEYS = ATT_WINDOW + ATT_Q_TILE
ATT_QB = 8
ATT_AHEAD = 3
PV_EXTRA_ROWS = 16
RET_TILE = 256
RET_BLOCKS = 4
RET_SAMPLE_SEQS = 8
FF_CHUNK = 1024
VMEM_LIMIT = 56 * 1024 * 1024
NEG_BIG = -1e30
LOG2E = math.log2(math.e)
ATT_Q_SCALE = (A_HEAD_DIM ** -0.5) * LOG2E
LOG_DECAY = tuple(math.log1p(-(2.0 ** (-5 - h))) for h in range(R_HEADS))


def _resident(shape):
    nd = len(shape)
    return pl.BlockSpec(shape, lambda *_: (0,) * nd, pipeline_mode=pl.Buffered(1))


def _rms(x, g):
    ms = jnp.mean(x * x, axis=-1, keepdims=True)
    return x * lax.rsqrt(ms + NORM_EPS) * g


def _inproj_kernel(x_ref, g_ref, w_ref, wvt_ref, cs_ref, sn_ref,
                   qa_ref, ka_ref, va_ref, qr_ref, kr_ref, vr_ref, sgr_ref, sga_ref, sgb_ref,
                   kt_ref, vt_ref, *, v_transposed):
    row_tile, tail_rows = x_ref.shape[0], kt_ref.shape[0]
    tail_start = row_tile - tail_rows
    for r0 in range(0, row_tile, SUB_ROWS):
        rows = slice(r0, r0 + SUB_ROWS)
        in_tail = r0 >= tail_start
        trows = slice(r0 - tail_start, r0 - tail_start + SUB_ROWS)
        h = _rms(x_ref[rows, :], g_ref[...]).astype(BF16)

        def seg(s):
            return jnp.dot(h, w_ref[:, IN_OFFS[s]:IN_OFFS[s + 1]], preferred_element_type=F32)

        cs = cs_ref[rows, :]
        sn = sn_ref[rows, :]

        def rope(z, scale):
            parts = []
            for hh in range(R_HEADS):
                zh = z[:, hh * R_DIM:(hh + 1) * R_DIM]
                r = zh * cs + pltpu.roll(zh, R_DIM // 2, 1) * sn
                if scale != 1.0:
                    r = r * scale
                parts.append(r.astype(BF16))
            return jnp.concatenate(parts, axis=1)

        qa_ref[rows, :] = (seg(0) * ATT_Q_SCALE).astype(BF16)
        ka = seg(1)
        ka_ref[rows, :] = ka.astype(BF16)
        if in_tail:
            kt_ref[trows] = ka.reshape(SUB_ROWS, A_HEADS, A_HEAD_DIM)
        if v_transposed:
            va_t = lax.dot_general(wvt_ref[...], h, (((1,), (1,)), ((), ())), preferred_element_type=F32)
            va_ref[r0 // SUB_ROWS] = va_t.astype(BF16)
            if in_tail:
                vt_ref[trows] = va_t.T.reshape(SUB_ROWS, A_HEADS, A_HEAD_DIM)
        else:
            va = seg(2)
            va_ref[rows, :] = va.astype(BF16)
            if in_tail:
                vt_ref[trows] = va.reshape(SUB_ROWS, A_HEADS, A_HEAD_DIM)
        qr_ref[rows, :] = rope(seg(3), 1.0)
        kr_ref[rows, :] = rope(seg(4), R_DIM ** -0.5)
        vr_ref[rows, :] = seg(5).astype(BF16)
        gr = seg(6)
        sgr_ref[rows, :] = (gr * jax.nn.sigmoid(gr)).astype(BF16)
        sga_ref[rows, :] = jax.nn.sigmoid(seg(7)).astype(BF16)
        sgb_ref[rows, :] = jax.nn.sigmoid(seg(8)).astype(BF16)


def _w_v_transposed(w_in, l):
    assert IN_OFFS[2] % A_WIDTH == 0

    def body(w_ref, o_ref):
        o_ref[...] = w_ref[0].T.astype(BF16)

    return pl.pallas_call(
        body,
        grid=(1,),
        in_specs=[pl.BlockSpec((1, D_MODEL, A_WIDTH), lambda i: (l, 0, IN_OFFS[2] // A_WIDTH))],
        out_specs=pl.BlockSpec((A_WIDTH, D_MODEL), lambda i: (0, 0)),
        out_shape=jax.ShapeDtypeStruct((A_WIDTH, D_MODEL), BF16),
        name="w_v_transpose",
    )(w_in)


def _in_proj(x2, g, w_bf, wvt_bf, cs, sn, seq_rows, tail_rows, v_transposed):
    rows = x2.shape[0]
    row_tile = min(seq_rows, ROW_TILE)
    assert tail_rows <= row_tile and tail_rows % SUB_ROWS == 0 and seq_rows % row_tile == 0
    tail_period = seq_rows // row_tile
    n = rows // row_tile
    n_pos = cs.shape[0] // row_tile
    n_sub = row_tile // SUB_ROWS
    row = lambda w: pl.BlockSpec((row_tile, w), lambda i: (i, 0))
    tail = pl.BlockSpec((tail_rows, A_HEADS, A_HEAD_DIM), lambda i: (i // tail_period, 0, 0))
    pos = pl.BlockSpec((row_tile, R_DIM), lambda i: (i % n_pos, 0))
    bf = lambda w: jax.ShapeDtypeStruct((rows, w), BF16)
    tail_shape = jax.ShapeDtypeStruct((rows // seq_rows * tail_rows, A_HEADS, A_HEAD_DIM), F32)
    if v_transposed:
        v_spec = pl.BlockSpec((n_sub, A_WIDTH, SUB_ROWS), lambda i: (i, 0, 0))
        v_shape = jax.ShapeDtypeStruct((rows // SUB_ROWS, A_WIDTH, SUB_ROWS), BF16)
    else:
        v_spec, v_shape = row(A_WIDTH), bf(A_WIDTH)
    return pl.pallas_call(
        functools.partial(_inproj_kernel, v_transposed=v_transposed),
        grid=(n,),
        in_specs=[row(D_MODEL), _resident((1, D_MODEL)), _resident((D_MODEL, IN_COLS)),
                  _resident((A_WIDTH, D_MODEL)), pos, pos],
        out_specs=[row(A_WIDTH), row(A_WIDTH), v_spec] + [row(R_WIDTH)] * 4 + [row(D_MODEL)] * 2
        + [tail, tail],
        out_shape=[bf(A_WIDTH), bf(A_WIDTH), v_shape] + [bf(R_WIDTH)] * 4 + [bf(D_MODEL)] * 2
        + [tail_shape, tail_shape],
        compiler_params=pltpu.CompilerParams(
            dimension_semantics=("arbitrary",), vmem_limit_bytes=VMEM_LIMIT),
        name="in_proj",
    )(x2, g, w_bf, wvt_bf, cs, sn)


def _attn_sample_kernel(q_ref, kn_ref, vn_ref, kct_ref, vct_ref, bias_ref, o_ref):
    tq = q_ref.shape[1]
    lc = kct_ref.shape[3]
    n_pad = bias_ref.shape[2] - lc
    q_all = q_ref[0]
    zrows = jnp.zeros((n_pad - tq, A_WIDTH), BF16)
    kn_pad = jnp.concatenate([kn_ref[0], zrows], axis=0)
    vn_pad = jnp.concatenate([vn_ref[0], zrows], axis=0)
    lane_q = lax.broadcasted_iota(jnp.int32, (tq, LANES), 1)
    lane_o = lax.broadcasted_iota(jnp.int32, (tq, MXU_N), 1)
    heads_per_group = MXU_N // A_HEAD_DIM
    pairs_per_group = heads_per_group // 2
    nt = (((1,), (1,)), ((), ()))
    outs = []
    for grp in range(A_HEADS // heads_per_group):
        e_parts, d_parts = [], []
        for pg in range(pairs_per_group):
            pair = grp * pairs_per_group + pg
            q2 = q_all[:, pair * LANES:(pair + 1) * LANES]
            zero = jnp.zeros_like(q2)
            q_st = jnp.concatenate([jnp.where(lane_q < A_HEAD_DIM, q2, zero),
                                    jnp.where(lane_q >= A_HEAD_DIM, q2, zero)], axis=0)
            kt_pair = kct_ref[0, 2 * pair:2 * pair + 2].reshape(LANES, lc).astype(BF16)
            s_c = jnp.dot(q_st, kt_pair, preferred_element_type=F32)
            s_n = lax.dot_general(q_st, kn_pad[:, pair * LANES:(pair + 1) * LANES], nt,
                                  preferred_element_type=F32)
            s = jnp.concatenate([s_c, s_n], axis=1)
            s = s + jnp.concatenate([bias_ref[2 * pair], bias_ref[2 * pair + 1]], axis=0)
            m = jnp.max(s, axis=-1, keepdims=True)
            e = jnp.exp2(s - m)
            d_parts.append(jnp.sum(e, axis=-1, keepdims=True))
            e_parts.append(e.astype(BF16))
        p_st = jnp.concatenate(e_parts, axis=0)
        vt_grp = vct_ref[0, heads_per_group * grp:heads_per_group * (grp + 1)].reshape(MXU_N, lc).astype(BF16)
        o = lax.dot_general(p_st[:, 0:lc], vt_grp, nt, preferred_element_type=F32)
        o = o + jnp.dot(p_st[:, lc:], vn_pad[:, grp * MXU_N:(grp + 1) * MXU_N], preferred_element_type=F32)
        o = o * (1.0 / jnp.concatenate(d_parts, axis=0))
        acc = jnp.zeros((tq, MXU_N), F32)
        for hg in range(heads_per_group):
            olo = hg * A_HEAD_DIM
            acc = jnp.where((lane_o >= olo) & (lane_o < olo + A_HEAD_DIM), o[hg * tq:(hg + 1) * tq], acc)
        outs.append(acc.astype(BF16))
    o_ref[0] = jnp.concatenate(outs, axis=1)


def _reduce_rows(x, op):
    parts = [x[r:r + 8] for r in range(0, x.shape[0], 8)]
    while len(parts) > 1:
        paired = [op(parts[i], parts[i + 1]) for i in range(0, len(parts) - 1, 2)]
        parts = paired + parts[len(parts) - len(parts) % 2:]
    red = jnp.max if op is jnp.maximum else jnp.sum
    return red(parts[0], axis=0, keepdims=True)


def _attn_prompt_kernel(q_ref, k_ref, vt_ref, bmt_ref, o_ref):
    n_c = ATT_KEYS // ATT_Q_TILE
    n_left = n_c - 1
    lane_q = lax.broadcasted_iota(jnp.int32, (ATT_Q_TILE, LANES), 1)
    ones_rows = jnp.ones((PV_EXTRA_ROWS, ATT_Q_TILE), BF16)
    band_keys = (LEFT_CHUNKS + 2) * CHUNK
    n_lt = ATT_Q_TILE // LANES

    def pieces(t):
        lo, hi = LANES * t, LANES * t + band_keys
        out = []
        for c in range(n_c):
            a, b = max(lo, c * ATT_Q_TILE), min(hi, (c + 1) * ATT_Q_TILE)
            if a < b:
                out.append((c, a - c * ATT_Q_TILE, b - c * ATT_Q_TILE))
        return out

    def window(qb):
        j = pl.program_id(1) * ATT_QB + qb
        kc = [jnp.maximum(j - n_left + c, 0) for c in range(n_c)]
        bc = [jnp.where(j - n_left + c >= 0, c, n_c) for c in range(n_c)]
        return kc, bc

    def scores(qb, hd):
        kc, bc = window(qb)
        pair = hd // 2
        q2 = q_ref[0, qb * ATT_Q_TILE:(qb + 1) * ATT_Q_TILE, pair * LANES:(pair + 1) * LANES]
        k2 = jnp.concatenate(
            [k_ref[0, pl.ds(pl.multiple_of(kc[c] * ATT_Q_TILE, ATT_Q_TILE), ATT_Q_TILE),
                   pair * LANES:(pair + 1) * LANES] for c in range(n_c)], axis=0)
        lo = (hd % 2) * A_HEAD_DIM
        qm = jnp.where((lane_q >= lo) & (lane_q < lo + A_HEAD_DIM), q2, jnp.zeros_like(q2))
        st = lax.dot_general(k2, qm, (((1,), (1,)), ((), ())), preferred_element_type=F32)
        tiles = []
        for t in range(n_lt):
            lanes = slice(t * LANES, (t + 1) * LANES)
            tiles.append([st[c * ATT_Q_TILE + a:c * ATT_Q_TILE + b, lanes] + bmt_ref[hd, bc[c], a:b, lanes]
                          for c, a, b in pieces(t)])
        return tiles

    def softmax(tiles):
        cols = []
        for t in range(n_lt):
            st = jnp.concatenate(tiles[t], axis=0)
            m = _reduce_rows(st, jnp.maximum)
            e = jnp.exp2(st - m).astype(BF16)
            zero = lambda n: [jnp.zeros((n, LANES), BF16)] if n else []
            cols.append(jnp.concatenate(zero(LANES * t) + [e] + zero(ATT_KEYS - band_keys - LANES * t), axis=0))
        return jnp.concatenate(cols, axis=1)

    def pv(qb, hd, pt):
        kc, _ = window(qb)
        ot = jnp.zeros((A_HEAD_DIM + PV_EXTRA_ROWS, ATT_Q_TILE), F32)
        for c in range(n_c):
            vt_c = vt_ref[0, kc[c], hd * A_HEAD_DIM:(hd + 1) * A_HEAD_DIM, :]
            ot = ot + jnp.dot(jnp.concatenate([vt_c, ones_rows], axis=0),
                              pt[c * ATT_Q_TILE:(c + 1) * ATT_Q_TILE, :], preferred_element_type=F32)
        return ot[0:A_HEAD_DIM] * (1.0 / ot[A_HEAD_DIM:A_HEAD_DIM + 1])

    items = [(qb, hd) for qb in range(ATT_QB) for hd in range(A_HEADS)]
    tiles = {it: scores(*it) for it in items[:ATT_AHEAD]}
    prev = None
    for n, (qb, hd) in enumerate(items):
        pt = softmax(tiles.pop((qb, hd)))
        if n + ATT_AHEAD < len(items):
            nxt = items[n + ATT_AHEAD]
            tiles[nxt] = scores(*nxt)
        ot = pv(qb, hd, pt)
        if hd % 2 == 0:
            prev = ot
        else:
            pair_t = jnp.concatenate([prev, ot], axis=0).T.astype(BF16)
            o_ref[0, qb * ATT_Q_TILE:(qb + 1) * ATT_Q_TILE, (hd // 2) * LANES:(hd // 2 + 1) * LANES] = pair_t


def _attn_prompt(q, k, vt, bmt):
    b, s, _ = q.shape
    assert vt.shape == (b, s // ATT_Q_TILE, A_WIDTH, ATT_Q_TILE)
    blk = pl.BlockSpec((1, ATT_QB * ATT_Q_TILE, A_WIDTH), lambda bi, j: (bi, j, 0))
    full = pl.BlockSpec((1, s, A_WIDTH), lambda bi, j: (bi, 0, 0))
    full_t = pl.BlockSpec((1,) + vt.shape[1:], lambda bi, j: (bi, 0, 0, 0))
    return pl.pallas_call(
        _attn_prompt_kernel,
        grid=(b, s // (ATT_QB * ATT_Q_TILE)),
        in_specs=[blk, full, full_t, _resident(bmt.shape)],
        out_specs=blk,
        out_shape=jax.ShapeDtypeStruct((b, s, A_WIDTH), BF16),
        compiler_params=pltpu.CompilerParams(
            dimension_semantics=("arbitrary", "arbitrary"), vmem_limit_bytes=VMEM_LIMIT),
        name="attn_prompt",
    )(q, k, vt, bmt)


def _attn_sample(q, kn, vn, kct, vct, bias):
    b, t, _ = q.shape
    new = pl.BlockSpec((1, t, A_WIDTH), lambda bi: (bi, 0, 0))
    cache = pl.BlockSpec((1,) + kct.shape[1:], lambda bi: (bi, 0, 0, 0))
    return pl.pallas_call(
        _attn_sample_kernel,
        grid=(b,),
        in_specs=[new, new, new, cache, cache, _resident(bias.shape)],
        out_specs=new,
        out_shape=jax.ShapeDtypeStruct((b, t, A_WIDTH), BF16),
        compiler_params=pltpu.CompilerParams(
            dimension_semantics=("arbitrary",), vmem_limit_bytes=VMEM_LIMIT),
        name="attn_sample",
    )(q, kn, vn, kct, vct, bias)


def _retention_kernel(q_ref, k_ref, v_ref, sg_ref, s0_ref, rn_ref, sout_ref,
                      state, dmat, rdec, kdec):
    bi = pl.program_id(0)
    j = pl.program_id(1)
    t = dmat.shape[1]
    n_seq = q_ref.shape[0]
    n_blk = q_ref.shape[1] // t

    @pl.when((bi == 0) & (j == 0))
    def _():
        row = lax.broadcasted_iota(jnp.int32, (t, t), 0)
        col = lax.broadcasted_iota(jnp.int32, (t, t), 1)
        diff = (row - col).astype(F32)
        rowl = lax.broadcasted_iota(jnp.int32, (t, R_DIM), 0).astype(F32)
        for hh in range(R_HEADS):
            lg = LOG_DECAY[hh]
            dmat[hh] = jnp.where(diff >= 0, jnp.exp(jnp.maximum(diff, 0.0) * lg), 0.0)
            rdec[hh] = jnp.exp((rowl + 1.0) * lg)
            kdec[hh] = jnp.exp((t - 1.0 - rowl) * lg)

    @pl.when(j == 0)
    def _():
        state[...] = s0_ref[...]

    s_cur = {(sq, hh): state[sq, hh] for sq in range(n_seq) for hh in range(R_HEADS)}

    def recur(sq, blk, hh):
        rows, sl = slice(blk * t, (blk + 1) * t), slice(hh * R_DIM, (hh + 1) * R_DIM)
        q = q_ref[sq, rows, sl]
        k = k_ref[sq, rows, sl]
        v = v_ref[sq, rows, sl]
        s_prev = s_cur[sq, hh]
        a = lax.dot_general(q, k, (((1,), (1,)), ((), ())), preferred_element_type=F32) * dmat[hh]
        intra = jnp.dot(a.astype(BF16), v, preferred_element_type=F32)
        cross = jnp.dot(q, s_prev.astype(BF16), preferred_element_type=F32) * rdec[hh]
        kd = (k.astype(F32) * kdec[hh]).astype(BF16)
        upd = lax.dot_general(kd, v, (((0,), (0,)), ((), ())), preferred_element_type=F32)
        s_cur[sq, hh] = s_prev * math.exp(t * LOG_DECAY[hh]) + upd
        return intra + cross

    def norm_gate(sq, blk, hh, o):
        rows, sl = slice(blk * t, (blk + 1) * t), slice(hh * R_DIM, (hh + 1) * R_DIM)
        mu = jnp.mean(o, axis=-1, keepdims=True)
        d = o - mu
        var = jnp.mean(d * d, axis=-1, keepdims=True)
        rn = d * lax.rsqrt(var + GN_EPS) * sg_ref[sq, rows, sl].astype(F32)
        rn_ref[sq, rows, sl] = rn.astype(BF16)

    items = [(sq, blk, hh) for sq in range(n_seq) for blk in range(n_blk) for hh in range(R_HEADS)]
    o_next = recur(*items[0])
    for n, it in enumerate(items):
        o = o_next
        if n + 1 < len(items):
            o_next = recur(*items[n + 1])
        norm_gate(*it, o)

    for (sq, hh), s_new in s_cur.items():
        state[sq, hh] = s_new
        sout_ref[sq, hh] = s_new


def _retention(q, k, v, sg, s0, tile, n_blk, n_seq):
    b, s, _ = q.shape
    blk = pl.BlockSpec((n_seq, n_blk * tile, R_WIDTH), lambda bi, j: (bi, j, 0))
    st = pl.BlockSpec((n_seq, R_HEADS, R_DIM, R_DIM), lambda bi, j: (bi, 0, 0, 0))
    return pl.pallas_call(
        _retention_kernel,
        grid=(b // n_seq, s // (n_blk * tile)),
        in_specs=[blk, blk, blk, blk, st],
        out_specs=[blk, st],
        out_shape=[jax.ShapeDtypeStruct((b, s, R_WIDTH), BF16),
                   jax.ShapeDtypeStruct((b, R_HEADS, R_DIM, R_DIM), F32)],
        scratch_shapes=[pltpu.VMEM((n_seq, R_HEADS, R_DIM, R_DIM), F32),
                        pltpu.VMEM((R_HEADS, tile, tile), F32),
                        pltpu.VMEM((R_HEADS, tile, R_DIM), F32),
                        pltpu.VMEM((R_HEADS, tile, R_DIM), F32)],
        compiler_params=pltpu.CompilerParams(
            dimension_semantics=("arbitrary", "arbitrary"), vmem_limit_bytes=VMEM_LIMIT),
        name="retention",
    )(q, k, v, sg, s0)


def _tail_kernel(x_ref, att_ref, rn_ref, sga_ref, sgb_ref, wao_ref, wro_ref, wo_ref,
                 g1_ref, g2_ref, g3_ref, wup_ref, wdn_ref, y_ref):
    n_sub = x_ref.shape[0] // SUB_ROWS
    rows = [slice(r * SUB_ROWS, (r + 1) * SUB_ROWS) for r in range(n_sub)]
    n_ff = D_FF // FF_CHUNK
    assert n_ff == 4
    m, mix, x1, h2, f = ([None] * n_sub for _ in range(5))

    def merge(s):
        r = rows[s]
        ya = jnp.dot(att_ref[r, :], wao_ref[...], preferred_element_type=F32)
        yr = jnp.dot(rn_ref[r, :], wro_ref[...], preferred_element_type=F32)
        m[s] = (sga_ref[r, :].astype(F32) * ya + sgb_ref[r, :].astype(F32) * yr).astype(BF16)

    def project(s):
        mix[s] = jnp.dot(m[s], wo_ref[...], preferred_element_type=F32)

    def norms(s):
        x1[s] = x_ref[rows[s], :] + _rms(mix[s], g1_ref[...])
        h2[s] = _rms(x1[s], g2_ref[...]).astype(BF16)
        f[s] = jnp.zeros_like(x1[s])

    def ffn(s, c):
        u = jnp.dot(h2[s], wup_ref[:, c * FF_CHUNK:(c + 1) * FF_CHUNK], preferred_element_type=F32)
        u = jnp.square(jnp.maximum(u, 0.0)).astype(BF16)
        f[s] = f[s] + jnp.dot(u, wdn_ref[c * FF_CHUNK:(c + 1) * FF_CHUNK, :], preferred_element_type=F32)

    def finish(s):
        y_ref[rows[s], :] = x1[s] + _rms(f[s], g3_ref[...])

    def when(s, fn):
        if s < n_sub:
            fn(s)

    merge(0)
    when(1, merge)
    project(0)
    norms(0)
    when(1, project)
    ffn(0, 0)
    when(1, norms)
    for s in range(n_sub):
        when(s + 2, merge)
        ffn(s, 1)
        when(s + 2, project)
        ffn(s, 2)
        when(s + 2, norms)
        ffn(s, 3)
        finish(s)
        when(s + 1, lambda t: ffn(t, 0))


def _tail(x2, att, rn, sga, sgb, wao, wro, wo, g1, g2, g3, wup, wdn):
    rows = x2.shape[0]
    row_tile = min(rows, ROW_TILE)
    row = lambda w: pl.BlockSpec((row_tile, w), lambda i: (i, 0))
    gspec = _resident((1, D_MODEL))
    return pl.pallas_call(
        _tail_kernel,
        grid=(rows // row_tile,),
        in_specs=[row(D_MODEL), row(A_WIDTH), row(R_WIDTH), row(D_MODEL), row(D_MODEL),
                  _resident(wao.shape), _resident(wro.shape), _resident(wo.shape),
                  gspec, gspec, gspec, _resident(wup.shape), _resident(wdn.shape)],
        out_specs=row(D_MODEL),
        out_shape=jax.ShapeDtypeStruct((rows, D_MODEL), F32),
        compiler_params=pltpu.CompilerParams(
            dimension_semantics=("arbitrary",), vmem_limit_bytes=VMEM_LIMIT),
        name="tail",
    )(x2, att, rn, sga, sgb, wao, wro, wo, g1, g2, g3, wup, wdn)


def _rope_tables(pos):
    half = R_DIM // 2
    inv = ROPE_BASE ** (-jnp.arange(half, dtype=F32) / half)
    ang = pos[:, None] * inv[None, :]
    cos, sin = jnp.cos(ang), jnp.sin(ang)
    return jnp.concatenate([cos, cos], axis=1), jnp.concatenate([-sin, sin], axis=1)


BIAS_ROW = 1024
BIAS_TPAD = 384


def _bias_kernel(t_ref, bpt_ref, bs_ref, *, t_s, nk_valid_s):
    tq = ATT_Q_TILE
    t8 = jnp.broadcast_to(t_ref[0], (8, BIAS_TPAD))
    t1 = t8.astype(BF16)
    r1 = t8 - t1.astype(F32)
    t2 = r1.astype(BF16)
    t3 = (r1 - t2.astype(F32)).astype(BF16)

    def generator(width, offset, sign):
        k_idx = lax.broadcasted_iota(jnp.int32, (BIAS_TPAD, width), 0)
        x_idx = lax.broadcasted_iota(jnp.int32, (BIAS_TPAD, width), 1)
        sel = jnp.where(k_idx == jnp.clip(offset + sign * x_idx, 0, 2 * REL_CLIP), 1.0, 0.0).astype(BF16)
        gen = (jnp.dot(t1, sel, preferred_element_type=F32) + jnp.dot(t2, sel, preferred_element_type=F32)
               + jnp.dot(t3, sel, preferred_element_type=F32))
        return gen[0:1] * LOG2E

    def toeplitz(gen, rows):
        width = gen.shape[1]
        return pltpu.roll(jnp.broadcast_to(gen, (rows, width)), width - tq, 1, stride=1, stride_axis=0)

    qc = lax.shift_right_logical(lax.broadcasted_iota(jnp.int32, (tq, tq), 1), 6)
    key = lax.broadcasted_iota(jnp.int32, (tq, tq), 0)
    for c in range(bpt_ref.shape[1]):
        if c * tq < ATT_KEYS:
            gen = generator(2 * tq, ATT_WINDOW + REL_CLIP - tq - tq * c, 1)
            kc = lax.shift_right_logical(key + c * tq, 6)
            valid = (kc >= qc) & (kc <= qc + LEFT_CHUNKS)
            bpt_ref[0, c] = jnp.where(valid, toeplitz(gen, tq)[:, 0:tq], NEG_BIG)
        else:
            bpt_ref[0, c] = jnp.full((tq, tq), NEG_BIG, F32)
    nk_s = bs_ref.shape[2]
    gen_s = generator(BIAS_ROW, ATT_WINDOW + tq + REL_CLIP, -1)
    lane = lax.broadcasted_iota(jnp.int32, (t_s, nk_s), 1)
    bs_ref[0] = jnp.where(lane < nk_valid_s, toeplitz(gen_s, t_s)[:, 0:nk_s], NEG_BIG)


def _build_bias(table, t_s, l_s, nk_s):
    assert CHUNK == 64 and l_s == ATT_WINDOW and t_s <= ATT_Q_TILE and nk_s <= ATT_KEYS
    hn, tl = table.shape
    tpad = jnp.pad(table.astype(F32), ((0, 0), (0, BIAS_TPAD - tl))).reshape(hn, 1, BIAS_TPAD)
    n_c = ATT_KEYS // ATT_Q_TILE + 1
    return pl.pallas_call(
        functools.partial(_bias_kernel, t_s=t_s, nk_valid_s=l_s + t_s),
        grid=(hn,),
        in_specs=[pl.BlockSpec((1, 1, BIAS_TPAD), lambda h: (h, 0, 0))],
        out_specs=[pl.BlockSpec((1, n_c, ATT_Q_TILE, ATT_Q_TILE), lambda h: (h, 0, 0, 0)),
                   pl.BlockSpec((1, t_s, nk_s), lambda h: (h, 0, 0))],
        out_shape=[jax.ShapeDtypeStruct((hn, n_c, ATT_Q_TILE, ATT_Q_TILE), F32),
                   jax.ShapeDtypeStruct((hn, t_s, nk_s), F32)],
        compiler_params=pltpu.CompilerParams(
            dimension_semantics=("arbitrary",), vmem_limit_bytes=VMEM_LIMIT),
        name="rel_bias",
    )(tpad)


def kernel(x_prompt, x_sample, cache_attn_k, cache_attn_v, state_retention, norm_mix_pre, w_in,
           rel_bias_table, w_attn_out, w_ret_out, w_o, norm_mix_post, norm_ffn_pre, w_ff_up,
           w_ff_down, norm_ffn_post):
    depth = w_in.shape[0]
    bp, sp, _ = x_prompt.shape
    bs, ts, _ = x_sample.shape
    lc = cache_attn_k.shape[2]
    n_keys_s = -(-(lc + ts) // LANES) * LANES

    cs_p, sn_p = _rope_tables(jnp.arange(sp, dtype=F32))
    pos_s = PAST_LEN + jnp.arange(ts, dtype=F32)
    cs_s, sn_s = _rope_tables(jnp.tile(pos_s, bs))

    xp = x_prompt.reshape(bp * sp, D_MODEL)
    xs = x_sample.reshape(bs * ts, D_MODEL)
    outs = [[] for _ in range(6)]
    for l in range(depth):
        w_in_b = w_in[l].astype(BF16)
        w_v_t = _w_v_transposed(w_in, l)
        wao, wro, wo = w_attn_out[l].astype(BF16), w_ret_out[l].astype(BF16), w_o[l].astype(BF16)
        wup, wdn = w_ff_up[l].astype(BF16), w_ff_down[l].astype(BF16)
        g0 = norm_mix_pre[l].reshape(1, D_MODEL)
        g1 = norm_mix_post[l].reshape(1, D_MODEL)
        g2 = norm_ffn_pre[l].reshape(1, D_MODEL)
        g3 = norm_ffn_post[l].reshape(1, D_MODEL)
        tail_w = (wao, wro, wo, g1, g2, g3, wup, wdn)
        bias_p, bias_s = _build_bias(rel_bias_table[l], ts, lc, n_keys_s)

        qa, ka, va_t, qr, kr, vr, sgr, sga, sgb, kt, vt = _in_proj(
            xp, g0, w_in_b, w_v_t, cs_p, sn_p, seq_rows=sp, tail_rows=ATT_WINDOW, v_transposed=True)
        r3 = lambda a: a.reshape(bp, sp, a.shape[-1])
        att = _attn_prompt(r3(qa), r3(ka), va_t.reshape(bp, sp // SUB_ROWS, A_WIDTH, SUB_ROWS), bias_p)
        rn, s_p = _retention(r3(qr), r3(kr), r3(vr), r3(sgr),
                             jnp.zeros((bp, R_HEADS, R_DIM, R_DIM), F32), RET_TILE, RET_BLOCKS, 1)
        xp = _tail(xp, att.reshape(bp * sp, A_WIDTH), rn.reshape(bp * sp, R_WIDTH), sga, sgb, *tail_w)
        outs[0].append(kt.reshape(bp, ATT_WINDOW, A_HEADS, A_HEAD_DIM))
        outs[1].append(vt.reshape(bp, ATT_WINDOW, A_HEADS, A_HEAD_DIM))
        outs[2].append(s_p)

        qa, ka, va, qr, kr, vr, sgr, sga, sgb, kt, vt = _in_proj(
            xs, g0, w_in_b, w_v_t, cs_s, sn_s, seq_rows=bs * ts, tail_rows=bs * ts, v_transposed=False)
        r3 = lambda a: a.reshape(bs, ts, a.shape[-1])
        att = _attn_sample(r3(qa), r3(ka), r3(va),
                           cache_attn_k[l].transpose(0, 2, 3, 1), cache_attn_v[l].transpose(0, 2, 3, 1),
                           bias_s)
        rn, s_s = _retention(r3(qr), r3(kr), r3(vr), r3(sgr),
                             state_retention[l].astype(F32), ts, 1, RET_SAMPLE_SEQS)
        xs = _tail(xs, att.reshape(bs * ts, A_WIDTH), rn.reshape(bs * ts, R_WIDTH), sga, sgb, *tail_w)
        outs[3].append(kt.reshape(bs, ts, A_HEADS, A_HEAD_DIM).astype(cache_attn_k.dtype))
        outs[4].append(vt.reshape(bs, ts, A_HEADS, A_HEAD_DIM).astype(cache_attn_v.dtype))
        outs[5].append(s_s.astype(state_retention.dtype))

    return (xp.reshape(bp, sp, D_MODEL), xs.reshape(bs, ts, D_MODEL),
            jnp.stack(outs[0]), jnp.stack(outs[1]), jnp.stack(outs[2]),
            jnp.stack(outs[3]), jnp.stack(outs[4]), jnp.stack(outs[5]))
```

```python
import functools
import math

import jax
import jax.numpy as jnp
import numpy as np
from jax import lax
from jax.experimental import pallas as pl
from jax.experimental.pallas import tpu as pltpu

F32 = jnp.float32
BF16 = jnp.bfloat16

D_MODEL = 1024
CHUNK = 64
LEFT_CHUNKS = 8
ATT_WINDOW = LEFT_CHUNKS * CHUNK
A_HEADS = 8
A_HEAD_DIM = 64
A_WIDTH = A_HEADS * A_HEAD_DIM
REL_CLIP = 128
R_HEADS = 4
R_DIM = 128
R_WIDTH = R_HEADS * R_DIM
ROPE_BASE = 10000.0
D_FF = 4 * D_MODEL
NORM_EPS = 1e-6
GN_EPS = 1e-6
PAST_LEN = 4096

IN_SPLITS = (A_WIDTH, A_WIDTH, A_WIDTH, R_WIDTH, R_WIDTH, R_WIDTH, R_WIDTH, D_MODEL, D_MODEL)
IN_COLS = sum(IN_SPLITS)
IN_OFFS = tuple(int(o) for o in np.cumsum((0,) + IN_SPLITS))

LANES = 128
BF16_SUBLANES = 16
MXU_N = 256
ROW_TILE = 1024
SUB_ROWS = 256
ATT_Q_TILE = 256
ATT_KEYS = ATT_WINDOW + ATT_Q_TILE
ATT_QB = 8
ATT_AHEAD = 3
PV_EXTRA_ROWS = 16
RET_TILE = 256
RET_BLOCKS = 4
RET_SAMPLE_SEQS = 8
FF_CHUNK = 1024
VMEM_LIMIT = 56 * 1024 * 1024
NEG_BIG = -1e30
LOG2E = math.log2(math.e)
ATT_Q_SCALE = (A_HEAD_DIM ** -0.5) * LOG2E
LOG_DECAY = tuple(math.log1p(-(2.0 ** (-5 - h))) for h in range(R_HEADS))


def _resident(shape):
    nd = len(shape)
    return pl.BlockSpec(shape, lambda *_: (0,) * nd, pipeline_mode=pl.Buffered(1))


def _rms(x, g):
    ms = jnp.mean(x * x, axis=-1, keepdims=True)
    return x * lax.rsqrt(ms + NORM_EPS) * g


def _inproj_kernel(*refs, v_transposed, n_cast):
    x_ref, g_ref, w_ref, wvt_ref, cs_ref, sn_ref = refs[:6]
    cast_in = refs[6:6 + n_cast]
    (qa_ref, ka_ref, va_ref, qr_ref, kr_ref, vr_ref, sgr_ref, sga_ref, sgb_ref,
     kt_ref, vt_ref) = refs[6 + n_cast:17 + n_cast]
    cast_out = refs[17 + n_cast:]
    for src, dst in zip(cast_in, cast_out):
        dst[...] = src[...].astype(BF16)
    row_tile, tail_rows = x_ref.shape[0], kt_ref.shape[0]
    tail_start = row_tile - tail_rows
    for r0 in range(0, row_tile, SUB_ROWS):
        rows = slice(r0, r0 + SUB_ROWS)
        in_tail = r0 >= tail_start
        trows = slice(r0 - tail_start, r0 - tail_start + SUB_ROWS)
        h = _rms(x_ref[rows, :], g_ref[...]).astype(BF16)

        def seg(s):
            return jnp.dot(h, w_ref[:, IN_OFFS[s]:IN_OFFS[s + 1]], preferred_element_type=F32)

        cs = cs_ref[rows, :]
        sn = sn_ref[rows, :]

        def rope(z, scale):
            parts = []
            for hh in range(R_HEADS):
                zh = z[:, hh * R_DIM:(hh + 1) * R_DIM]
                r = zh * cs + pltpu.roll(zh, R_DIM // 2, 1) * sn
                if scale != 1.0:
                    r = r * scale
                parts.append(r.astype(BF16))
            return jnp.concatenate(parts, axis=1)

        qa_ref[rows, :] = (seg(0) * ATT_Q_SCALE).astype(BF16)
        ka = seg(1)
        ka_ref[rows, :] = ka.astype(BF16)
        if in_tail:
            kt_ref[trows] = ka.reshape(SUB_ROWS, A_HEADS, A_HEAD_DIM)
        if v_transposed:
            va_t = lax.dot_general(wvt_ref[...], h, (((1,), (1,)), ((), ())), preferred_element_type=F32)
            va_ref[r0 // SUB_ROWS] = va_t.astype(BF16)
            if in_tail:
                vt_ref[trows] = va_t.T.reshape(SUB_ROWS, A_HEADS, A_HEAD_DIM)
        else:
            va = seg(2)
            va_ref[rows, :] = va.astype(BF16)
            if in_tail:
                vt_ref[trows] = va.reshape(SUB_ROWS, A_HEADS, A_HEAD_DIM)
        qr_ref[rows, :] = rope(seg(3), 1.0)
        kr_ref[rows, :] = rope(seg(4), R_DIM ** -0.5)
        vr_ref[rows, :] = seg(5).astype(BF16)
        gr = seg(6)
        sgr_ref[rows, :] = (gr * jax.nn.sigmoid(gr)).astype(BF16)
        sga_ref[rows, :] = jax.nn.sigmoid(seg(7)).astype(BF16)
        sgb_ref[rows, :] = jax.nn.sigmoid(seg(8)).astype(BF16)


def _w_v_transposed(w_in, l):
    assert IN_OFFS[2] % A_WIDTH == 0

    def body(w_ref, o_ref):
        o_ref[...] = w_ref[0].T.astype(BF16)

    return pl.pallas_call(
        body,
        grid=(1,),
        in_specs=[pl.BlockSpec((1, D_MODEL, A_WIDTH), lambda i: (l, 0, IN_OFFS[2] // A_WIDTH))],
        out_specs=pl.BlockSpec((A_WIDTH, D_MODEL), lambda i: (0, 0)),
        out_shape=jax.ShapeDtypeStruct((A_WIDTH, D_MODEL), BF16),
        name="w_v_transpose",
    )(w_in)


def _in_proj(x2, g, w_bf, wvt_bf, cs, sn, seq_rows, tail_rows, v_transposed, cast=()):
    rows = x2.shape[0]
    row_tile = min(seq_rows, ROW_TILE)
    assert tail_rows <= row_tile and tail_rows % SUB_ROWS == 0 and seq_rows % row_tile == 0
    tail_period = seq_rows // row_tile
    n = rows // row_tile
    n_pos = cs.shape[0] // row_tile
    n_sub = row_tile // SUB_ROWS
    row = lambda w: pl.BlockSpec((row_tile, w), lambda i: (i, 0))
    tail = pl.BlockSpec((tail_rows, A_HEADS, A_HEAD_DIM), lambda i: (i // tail_period, 0, 0),
                        pipeline_mode=pl.Buffered(1))
    pos = pl.BlockSpec((row_tile, R_DIM), lambda i: (i % n_pos, 0))
    bf = lambda w: jax.ShapeDtypeStruct((rows, w), BF16)
    tail_shape = jax.ShapeDtypeStruct((rows // seq_rows * tail_rows, A_HEADS, A_HEAD_DIM), F32)
    if v_transposed:
        v_spec = pl.BlockSpec((n_sub, A_WIDTH, SUB_ROWS), lambda i: (i, 0, 0))
        v_shape = jax.ShapeDtypeStruct((rows // SUB_ROWS, A_WIDTH, SUB_ROWS), BF16)
    else:
        v_spec, v_shape = row(A_WIDTH), bf(A_WIDTH)
    cast_specs, cast_shapes = [], []
    for w in cast:
        assert w.shape[0] % n == 0 and (w.shape[0] // n) % BF16_SUBLANES == 0
        cast_specs.append(pl.BlockSpec((w.shape[0] // n, w.shape[1]), lambda i: (i, 0)))
        cast_shapes.append(jax.ShapeDtypeStruct(w.shape, BF16))
    return pl.pallas_call(
        functools.partial(_inproj_kernel, v_transposed=v_transposed, n_cast=len(cast)),
        grid=(n,),
        in_specs=[row(D_MODEL), _resident((1, D_MODEL)), _resident((D_MODEL, IN_COLS)),
                  _resident((A_WIDTH, D_MODEL)), pos, pos] + cast_specs,
        out_specs=[row(A_WIDTH), row(A_WIDTH), v_spec] + [row(R_WIDTH)] * 4 + [row(D_MODEL)] * 2
        + [tail, tail] + cast_specs,
        out_shape=[bf(A_WIDTH), bf(A_WIDTH), v_shape] + [bf(R_WIDTH)] * 4 + [bf(D_MODEL)] * 2
        + [tail_shape, tail_shape] + cast_shapes,
        compiler_params=pltpu.CompilerParams(
            dimension_semantics=("arbitrary",), vmem_limit_bytes=VMEM_LIMIT),
        name="in_proj",
    )(x2, g, w_bf, wvt_bf, cs, sn, *cast)


def _attn_sample_kernel(q_ref, kn_ref, vn_ref, kct_ref, vct_ref, bias_ref, o_ref):
    tq = q_ref.shape[1]
    lc = kct_ref.shape[3]
    n_pad = bias_ref.shape[2] - lc
    q_all = q_ref[0]
    zrows = jnp.zeros((n_pad - tq, A_WIDTH), BF16)
    kn_pad = jnp.concatenate([kn_ref[0], zrows], axis=0)
    vn_pad = jnp.concatenate([vn_ref[0], zrows], axis=0)
    lane_q = lax.broadcasted_iota(jnp.int32, (tq, LANES), 1)
    lane_o = lax.broadcasted_iota(jnp.int32, (tq, MXU_N), 1)
    heads_per_group = MXU_N // A_HEAD_DIM
    pairs_per_group = heads_per_group // 2
    nt = (((1,), (1,)), ((), ()))
    outs = []
    for grp in range(A_HEADS // heads_per_group):
        e_parts, d_parts = [], []
        for pg in range(pairs_per_group):
            pair = grp * pairs_per_group + pg
            q2 = q_all[:, pair * LANES:(pair + 1) * LANES]
            zero = jnp.zeros_like(q2)
            q_st = jnp.concatenate([jnp.where(lane_q < A_HEAD_DIM, q2, zero),
                                    jnp.where(lane_q >= A_HEAD_DIM, q2, zero)], axis=0)
            kt_pair = kct_ref[0, 2 * pair:2 * pair + 2].reshape(LANES, lc).astype(BF16)
            s_c = jnp.dot(q_st, kt_pair, preferred_element_type=F32)
            s_n = lax.dot_general(q_st, kn_pad[:, pair * LANES:(pair + 1) * LANES], nt,
                                  preferred_element_type=F32)
            s = jnp.concatenate([s_c, s_n], axis=1)
            s = s + jnp.concatenate([bias_ref[2 * pair], bias_ref[2 * pair + 1]], axis=0)
            m = jnp.max(s, axis=-1, keepdims=True)
            e = jnp.exp2(s - m)
            d_parts.append(jnp.sum(e, axis=-1, keepdims=True))
            e_parts.append(e.astype(BF16))
        p_st = jnp.concatenate(e_parts, axis=0)
        vt_grp = vct_ref[0, heads_per_group * grp:heads_per_group * (grp + 1)].reshape(MXU_N, lc).astype(BF16)
        o = lax.dot_general(p_st[:, 0:lc], vt_grp, nt, preferred_element_type=F32)
        o = o + jnp.dot(p_st[:, lc:], vn_pad[:, grp * MXU_N:(grp + 1) * MXU_N], preferred_element_type=F32)
        o = o * (1.0 / jnp.concatenate(d_parts, axis=0))
        acc = jnp.zeros((tq, MXU_N), F32)
        for hg in range(heads_per_group):
            olo = hg * A_HEAD_DIM
            acc = jnp.where((lane_o >= olo) & (lane_o < olo + A_HEAD_DIM), o[hg * tq:(hg + 1) * tq], acc)
        outs.append(acc.astype(BF16))
    o_ref[0] = jnp.concatenate(outs, axis=1)


def _reduce_rows(x, op):
    parts = [x[r:r + 8] for r in range(0, x.shape[0], 8)]
    while len(parts) > 1:
        paired = [op(parts[i], parts[i + 1]) for i in range(0, len(parts) - 1, 2)]
        parts = paired + parts[len(parts) - len(parts) % 2:]
    red = jnp.max if op is jnp.maximum else jnp.sum
    return red(parts[0], axis=0, keepdims=True)


def _attn_prompt_kernel(q_ref, k_ref, vt_ref, bmt_ref, o_ref):
    n_c = ATT_KEYS // ATT_Q_TILE
    n_left = n_c - 1
    lane_q = lax.broadcasted_iota(jnp.int32, (ATT_Q_TILE, LANES), 1)
    ones_rows = jnp.ones((PV_EXTRA_ROWS, ATT_Q_TILE), BF16)
    band_keys = (LEFT_CHUNKS + 2) * CHUNK
    n_lt = ATT_Q_TILE // LANES

    def pieces(t):
        lo, hi = LANES * t, LANES * t + band_keys
        out = []
        for c in range(n_c):
            a, b = max(lo, c * ATT_Q_TILE), min(hi, (c + 1) * ATT_Q_TILE)
            if a < b:
                out.append((c, a - c * ATT_Q_TILE, b - c * ATT_Q_TILE))
        return out

    def window(qb):
        j = pl.program_id(1) * ATT_QB + qb
        kc = [jnp.maximum(j - n_left + c, 0) for c in range(n_c)]
        bc = [jnp.where(j - n_left + c >= 0, c, n_c) for c in range(n_c)]
        return kc, bc

    def scores(qb, hd):
        kc, bc = window(qb)
        pair = hd // 2
        q2 = q_ref[0, qb * ATT_Q_TILE:(qb + 1) * ATT_Q_TILE, pair * LANES:(pair + 1) * LANES]
        k2 = jnp.concatenate(
            [k_ref[0, pl.ds(pl.multiple_of(kc[c] * ATT_Q_TILE, ATT_Q_TILE), ATT_Q_TILE),
                   pair * LANES:(pair + 1) * LANES] for c in range(n_c)], axis=0)
        lo = (hd % 2) * A_HEAD_DIM
        qm = jnp.where((lane_q >= lo) & (lane_q < lo + A_HEAD_DIM), q2, jnp.zeros_like(q2))
        st = lax.dot_general(k2, qm, (((1,), (1,)), ((), ())), preferred_element_type=F32)
        tiles = []
        for t in range(n_lt):
            lanes = slice(t * LANES, (t + 1) * LANES)
            tiles.append([st[c * ATT_Q_TILE + a:c * ATT_Q_TILE + b, lanes] + bmt_ref[hd, bc[c], a:b, lanes]
                          for c, a, b in pieces(t)])
        return tiles

    def softmax(tiles):
        cols = []
        for t in range(n_lt):
            st = jnp.concatenate(tiles[t], axis=0)
            m = _reduce_rows(st, jnp.maximum)
            e = jnp.exp2(st - m).astype(BF16)
            zero = lambda n: [jnp.zeros((n, LANES), BF16)] if n else []
            cols.append(jnp.concatenate(zero(LANES * t) + [e] + zero(ATT_KEYS - band_keys - LANES * t), axis=0))
        return jnp.concatenate(cols, axis=1)

    def pv(qb, hd, pt):
        kc, _ = window(qb)
        ot = jnp.zeros((A_HEAD_DIM + PV_EXTRA_ROWS, ATT_Q_TILE), F32)
        for c in range(n_c):
            vt_c = vt_ref[0, kc[c], hd * A_HEAD_DIM:(hd + 1) * A_HEAD_DIM, :]
            ot = ot + jnp.dot(jnp.concatenate([vt_c, ones_rows], axis=0),
                              pt[c * ATT_Q_TILE:(c + 1) * ATT_Q_TILE, :], preferred_element_type=F32)
        return ot[0:A_HEAD_DIM] * (1.0 / ot[A_HEAD_DIM:A_HEAD_DIM + 1])

    items = [(qb, hd) for qb in range(ATT_QB) for hd in range(A_HEADS)]
    tiles = {it: scores(*it) for it in items[:ATT_AHEAD]}
    prev = None
    for n, (qb, hd) in enumerate(items):
        pt = softmax(tiles.pop((qb, hd)))
        if n + ATT_AHEAD < len(items):
            nxt = items[n + ATT_AHEAD]
            tiles[nxt] = scores(*nxt)
        ot = pv(qb, hd, pt)
        if hd % 2 == 0:
            prev = ot
        else:
            pair_t = jnp.concatenate([prev, ot], axis=0).T.astype(BF16)
            o_ref[0, qb * ATT_Q_TILE:(qb + 1) * ATT_Q_TILE, (hd // 2) * LANES:(hd // 2 + 1) * LANES] = pair_t


def _attn_prompt(q, k, vt, bmt):
    b, s, _ = q.shape
    assert vt.shape == (b, s // ATT_Q_TILE, A_WIDTH, ATT_Q_TILE)
    blk = pl.BlockSpec((1, ATT_QB * ATT_Q_TILE, A_WIDTH), lambda bi, j: (bi, j, 0))
    full = pl.BlockSpec((1, s, A_WIDTH), lambda bi, j: (bi, 0, 0))
    full_t = pl.BlockSpec((1,) + vt.shape[1:], lambda bi, j: (bi, 0, 0, 0))
    return pl.pallas_call(
        _attn_prompt_kernel,
        grid=(b, s // (ATT_QB * ATT_Q_TILE)),
        in_specs=[blk, full, full_t, _resident(bmt.shape)],
        out_specs=blk,
        out_shape=jax.ShapeDtypeStruct((b, s, A_WIDTH), BF16),
        compiler_params=pltpu.CompilerParams(
            dimension_semantics=("arbitrary", "arbitrary"), vmem_limit_bytes=VMEM_LIMIT),
        name="attn_prompt",
    )(q, k, vt, bmt)


def _attn_sample(q, kn, vn, kct, vct, bias):
    b, t, _ = q.shape
    new = pl.BlockSpec((1, t, A_WIDTH), lambda bi: (bi, 0, 0))
    cache = pl.BlockSpec((1,) + kct.shape[1:], lambda bi: (bi, 0, 0, 0))
    return pl.pallas_call(
        _attn_sample_kernel,
        grid=(b,),
        in_specs=[new, new, new, cache, cache, _resident(bias.shape)],
        out_specs=new,
        out_shape=jax.ShapeDtypeStruct((b, t, A_WIDTH), BF16),
        compiler_params=pltpu.CompilerParams(
            dimension_semantics=("arbitrary",), vmem_limit_bytes=VMEM_LIMIT),
        name="attn_sample",
    )(q, kn, vn, kct, vct, bias)


def _retention_kernel(q_ref, k_ref, v_ref, sg_ref, s0_ref, rn_ref, sout_ref,
                      state, dmat, rdec, kdec):
    bi = pl.program_id(0)
    j = pl.program_id(1)
    t = dmat.shape[1]
    n_seq = q_ref.shape[0]
    n_blk = q_ref.shape[1] // t

    @pl.when((bi == 0) & (j == 0))
    def _():
        row = lax.broadcasted_iota(jnp.int32, (t, t), 0)
        col = lax.broadcasted_iota(jnp.int32, (t, t), 1)
        diff = (row - col).astype(F32)
        rowl = lax.broadcasted_iota(jnp.int32, (t, R_DIM), 0).astype(F32)
        for hh in range(R_HEADS):
            lg = LOG_DECAY[hh]
            dmat[hh] = jnp.where(diff >= 0, jnp.exp(jnp.maximum(diff, 0.0) * lg), 0.0)
            rdec[hh] = jnp.exp((rowl + 1.0) * lg)
            kdec[hh] = jnp.exp((t - 1.0 - rowl) * lg)

    @pl.when(j == 0)
    def _():
        state[...] = s0_ref[...]

    s_cur = {(sq, hh): state[sq, hh] for sq in range(n_seq) for hh in range(R_HEADS)}

    def recur(sq, blk, hh):
        rows, sl = slice(blk * t, (blk + 1) * t), slice(hh * R_DIM, (hh + 1) * R_DIM)
        q = q_ref[sq, rows, sl]
        k = k_ref[sq, rows, sl]
        v = v_ref[sq, rows, sl]
        s_prev = s_cur[sq, hh]
        a = lax.dot_general(q, k, (((1,), (1,)), ((), ())), preferred_element_type=F32) * dmat[hh]
        intra = jnp.dot(a.astype(BF16), v, preferred_element_type=F32)
        cross = jnp.dot(q, s_prev.astype(BF16), preferred_element_type=F32) * rdec[hh]
        kd = (k.astype(F32) * kdec[hh]).astype(BF16)
        upd = lax.dot_general(kd, v, (((0,), (0,)), ((), ())), preferred_element_type=F32)
        s_cur[sq, hh] = s_prev * math.exp(t * LOG_DECAY[hh]) + upd
        return intra + cross

    def norm_gate(sq, blk, hh, o):
        rows, sl = slice(blk * t, (blk + 1) * t), slice(hh * R_DIM, (hh + 1) * R_DIM)
        mu = jnp.mean(o, axis=-1, keepdims=True)
        d = o - mu
        var = jnp.mean(d * d, axis=-1, keepdims=True)
        rn = d * lax.rsqrt(var + GN_EPS) * sg_ref[sq, rows, sl].astype(F32)
        rn_ref[sq, rows, sl] = rn.astype(BF16)

    items = [(sq, blk, hh) for sq in range(n_seq) for blk in range(n_blk) for hh in range(R_HEADS)]
    o_next = recur(*items[0])
    for n, it in enumerate(items):
        o = o_next
        if n + 1 < len(items):
            o_next = recur(*items[n + 1])
        norm_gate(*it, o)

    for (sq, hh), s_new in s_cur.items():
        state[sq, hh] = s_new
        sout_ref[sq, hh] = s_new


def _retention(q, k, v, sg, s0, tile, n_blk, n_seq):
    b, s, _ = q.shape
    blk = pl.BlockSpec((n_seq, n_blk * tile, R_WIDTH), lambda bi, j: (bi, j, 0))
    st = pl.BlockSpec((n_seq, R_HEADS, R_DIM, R_DIM), lambda bi, j: (bi, 0, 0, 0))
    return pl.pallas_call(
        _retention_kernel,
        grid=(b // n_seq, s // (n_blk * tile)),
        in_specs=[blk, blk, blk, blk, st],
        out_specs=[blk, st],
        out_shape=[jax.ShapeDtypeStruct((b, s, R_WIDTH), BF16),
                   jax.ShapeDtypeStruct((b, R_HEADS, R_DIM, R_DIM), F32)],
        scratch_shapes=[pltpu.VMEM((n_seq, R_HEADS, R_DIM, R_DIM), F32),
                        pltpu.VMEM((R_HEADS, tile, tile), F32),
                        pltpu.VMEM((R_HEADS, tile, R_DIM), F32),
                        pltpu.VMEM((R_HEADS, tile, R_DIM), F32)],
        compiler_params=pltpu.CompilerParams(
            dimension_semantics=("arbitrary", "arbitrary"), vmem_limit_bytes=VMEM_LIMIT),
        name="retention",
    )(q, k, v, sg, s0)


def _tail_kernel(x_ref, att_ref, rn_ref, sga_ref, sgb_ref, wao_ref, wro_ref, wo_ref,
                 g1_ref, g2_ref, g3_ref, wup_ref, wdn_ref, y_ref):
    n_sub = x_ref.shape[0] // SUB_ROWS
    rows = [slice(r * SUB_ROWS, (r + 1) * SUB_ROWS) for r in range(n_sub)]
    n_ff = D_FF // FF_CHUNK
    assert n_ff == 4
    m, mix, x1, h2, f = ([None] * n_sub for _ in range(5))

    def merge(s):
        r = rows[s]
        ya = jnp.dot(att_ref[r, :], wao_ref[...], preferred_element_type=F32)
        yr = jnp.dot(rn_ref[r, :], wro_ref[...], preferred_element_type=F32)
        m[s] = (sga_ref[r, :].astype(F32) * ya + sgb_ref[r, :].astype(F32) * yr).astype(BF16)

    def project(s):
        mix[s] = jnp.dot(m[s], wo_ref[...], preferred_element_type=F32)

    def norms(s):
        x1[s] = x_ref[rows[s], :] + _rms(mix[s], g1_ref[...])
        h2[s] = _rms(x1[s], g2_ref[...]).astype(BF16)
        f[s] = jnp.zeros_like(x1[s])

    def ffn(s, c):
        u = jnp.dot(h2[s], wup_ref[:, c * FF_CHUNK:(c + 1) * FF_CHUNK], preferred_element_type=F32)
        u = jnp.square(jnp.maximum(u, 0.0)).astype(BF16)
        f[s] = f[s] + jnp.dot(u, wdn_ref[c * FF_CHUNK:(c + 1) * FF_CHUNK, :], preferred_element_type=F32)

    def finish(s):
        y_ref[rows[s], :] = x1[s] + _rms(f[s], g3_ref[...])

    def when(s, fn):
        if s < n_sub:
            fn(s)

    merge(0)
    when(1, merge)
    project(0)
    norms(0)
    when(1, project)
    ffn(0, 0)
    when(1, norms)
    for s in range(n_sub):
        when(s + 2, merge)
        ffn(s, 1)
        when(s + 2, project)
        ffn(s, 2)
        when(s + 2, norms)
        ffn(s, 3)
        finish(s)
        when(s + 1, lambda t: ffn(t, 0))


def _tail(x2, att, rn, sga, sgb, wao, wro, wo, g1, g2, g3, wup, wdn):
    rows = x2.shape[0]
    row_tile = min(rows, ROW_TILE)
    row = lambda w: pl.BlockSpec((row_tile, w), lambda i: (i, 0))
    gspec = _resident((1, D_MODEL))
    return pl.pallas_call(
        _tail_kernel,
        grid=(rows // row_tile,),
        in_specs=[row(D_MODEL), row(A_WIDTH), row(R_WIDTH), row(D_MODEL), row(D_MODEL),
                  _resident(wao.shape), _resident(wro.shape), _resident(wo.shape),
                  gspec, gspec, gspec, _resident(wup.shape), _resident(wdn.shape)],
        out_specs=row(D_MODEL),
        out_shape=jax.ShapeDtypeStruct((rows, D_MODEL), F32),
        compiler_params=pltpu.CompilerParams(
            dimension_semantics=("arbitrary",), vmem_limit_bytes=VMEM_LIMIT),
        name="tail",
    )(x2, att, rn, sga, sgb, wao, wro, wo, g1, g2, g3, wup, wdn)


def _rope_tables(pos):
    half = R_DIM // 2
    inv = ROPE_BASE ** (-jnp.arange(half, dtype=F32) / half)
    ang = pos[:, None] * inv[None, :]
    cos, sin = jnp.cos(ang), jnp.sin(ang)
    return jnp.concatenate([cos, cos], axis=1), jnp.concatenate([-sin, sin], axis=1)


BIAS_ROW = 1024
BIAS_TPAD = 384


def _bias_kernel(t_ref, bpt_ref, bs_ref, *, t_s, nk_valid_s):
    tq = ATT_Q_TILE
    t8 = jnp.broadcast_to(t_ref[0], (8, BIAS_TPAD))
    t1 = t8.astype(BF16)
    r1 = t8 - t1.astype(F32)
    t2 = r1.astype(BF16)
    t3 = (r1 - t2.astype(F32)).astype(BF16)

    def generator(width, offset, sign):
        k_idx = lax.broadcasted_iota(jnp.int32, (BIAS_TPAD, width), 0)
        x_idx = lax.broadcasted_iota(jnp.int32, (BIAS_TPAD, width), 1)
        sel = jnp.where(k_idx == jnp.clip(offset + sign * x_idx, 0, 2 * REL_CLIP), 1.0, 0.0).astype(BF16)
        gen = (jnp.dot(t1, sel, preferred_element_type=F32) + jnp.dot(t2, sel, preferred_element_type=F32)
               + jnp.dot(t3, sel, preferred_element_type=F32))
        return gen[0:1] * LOG2E

    def toeplitz(gen, rows):
        width = gen.shape[1]
        return pltpu.roll(jnp.broadcast_to(gen, (rows, width)), width - tq, 1, stride=1, stride_axis=0)

    qc = lax.shift_right_logical(lax.broadcasted_iota(jnp.int32, (tq, tq), 1), 6)
    key = lax.broadcasted_iota(jnp.int32, (tq, tq), 0)
    for c in range(bpt_ref.shape[1]):
        if c * tq < ATT_KEYS:
            gen = generator(2 * tq, ATT_WINDOW + REL_CLIP - tq - tq * c, 1)
            kc = lax.shift_right_logical(key + c * tq, 6)
            valid = (kc >= qc) & (kc <= qc + LEFT_CHUNKS)
            bpt_ref[0, c] = jnp.where(valid, toeplitz(gen, tq)[:, 0:tq], NEG_BIG)
        else:
            bpt_ref[0, c] = jnp.full((tq, tq), NEG_BIG, F32)
    nk_s = bs_ref.shape[2]
    gen_s = generator(BIAS_ROW, ATT_WINDOW + tq + REL_CLIP, -1)
    lane = lax.broadcasted_iota(jnp.int32, (t_s, nk_s), 1)
    bs_ref[0] = jnp.where(lane < nk_valid_s, toeplitz(gen_s, t_s)[:, 0:nk_s], NEG_BIG)


def _build_bias(table, t_s, l_s, nk_s):
    assert CHUNK == 64 and l_s == ATT_WINDOW and t_s <= ATT_Q_TILE and nk_s <= ATT_KEYS
    hn, tl = table.shape
    tpad = jnp.pad(table.astype(F32), ((0, 0), (0, BIAS_TPAD - tl))).reshape(hn, 1, BIAS_TPAD)
    n_c = ATT_KEYS // ATT_Q_TILE + 1
    return pl.pallas_call(
        functools.partial(_bias_kernel, t_s=t_s, nk_valid_s=l_s + t_s),
        grid=(hn,),
        in_specs=[pl.BlockSpec((1, 1, BIAS_TPAD), lambda h: (h, 0, 0))],
        out_specs=[pl.BlockSpec((1, n_c, ATT_Q_TILE, ATT_Q_TILE), lambda h: (h, 0, 0, 0)),
                   pl.BlockSpec((1, t_s, nk_s), lambda h: (h, 0, 0))],
        out_shape=[jax.ShapeDtypeStruct((hn, n_c, ATT_Q_TILE, ATT_Q_TILE), F32),
                   jax.ShapeDtypeStruct((hn, t_s, nk_s), F32)],
        compiler_params=pltpu.CompilerParams(
            dimension_semantics=("arbitrary",), vmem_limit_bytes=VMEM_LIMIT),
        name="rel_bias",
    )(tpad)


def kernel(x_prompt, x_sample, cache_attn_k, cache_attn_v, state_retention, norm_mix_pre, w_in,
           rel_bias_table, w_attn_out, w_ret_out, w_o, norm_mix_post, norm_ffn_pre, w_ff_up,
           w_ff_down, norm_ffn_post):
    depth = w_in.shape[0]
    bp, sp, _ = x_prompt.shape
    bs, ts, _ = x_sample.shape
    lc = cache_attn_k.shape[2]
    n_keys_s = -(-(lc + ts) // LANES) * LANES

    cs_p, sn_p = _rope_tables(jnp.arange(sp, dtype=F32))
    pos_s = PAST_LEN + jnp.arange(ts, dtype=F32)
    cs_s, sn_s = _rope_tables(jnp.tile(pos_s, bs))

    xp = x_prompt.reshape(bp * sp, D_MODEL)
    xs = x_sample.reshape(bs * ts, D_MODEL)
    outs = [[] for _ in range(6)]
    for l in range(depth):
        w_in_b = w_in[l].astype(BF16)
        w_v_t = _w_v_transposed(w_in, l)
        g0 = norm_mix_pre[l].reshape(1, D_MODEL)
        g1 = norm_mix_post[l].reshape(1, D_MODEL)
        g2 = norm_ffn_pre[l].reshape(1, D_MODEL)
        g3 = norm_ffn_post[l].reshape(1, D_MODEL)
        bias_p, bias_s = _build_bias(rel_bias_table[l], ts, lc, n_keys_s)

        qa, ka, va_t, qr, kr, vr, sgr, sga, sgb, kt, vt, wao, wro, wo, wup, wdn = _in_proj(
            xp, g0, w_in_b, w_v_t, cs_p, sn_p, seq_rows=sp, tail_rows=ATT_WINDOW, v_transposed=True,
            cast=(w_attn_out[l], w_ret_out[l], w_o[l], w_ff_up[l], w_ff_down[l]))
        tail_w = (wao, wro, wo, g1, g2, g3, wup, wdn)
        r3 = lambda a: a.reshape(bp, sp, a.shape[-1])
        att = _attn_prompt(r3(qa), r3(ka), va_t.reshape(bp, sp // SUB_ROWS, A_WIDTH, SUB_ROWS), bias_p)
        rn, s_p = _retention(r3(qr), r3(kr), r3(vr), r3(sgr),
                             jnp.zeros((bp, R_HEADS, R_DIM, R_DIM), F32), RET_TILE, RET_BLOCKS, 1)
        xp = _tail(xp, att.reshape(bp * sp, A_WIDTH), rn.reshape(bp * sp, R_WIDTH), sga, sgb, *tail_w)
        outs[0].append(kt.reshape(bp, ATT_WINDOW, A_HEADS, A_HEAD_DIM))
        outs[1].append(vt.reshape(bp, ATT_WINDOW, A_HEADS, A_HEAD_DIM))
        outs[2].append(s_p)

        qa, ka, va, qr, kr, vr, sgr, sga, sgb, kt, vt = _in_proj(
            xs, g0, w_in_b, w_v_t, cs_s, sn_s, seq_rows=bs * ts, tail_rows=bs * ts, v_transposed=False)
        r3 = lambda a: a.reshape(bs, ts, a.shape[-1])
        att = _attn_sample(r3(qa), r3(ka), r3(va),
                           cache_attn_k[l].transpose(0, 2, 3, 1), cache_attn_v[l].transpose(0, 2, 3, 1),
                           bias_s)
        rn, s_s = _retention(r3(qr), r3(kr), r3(vr), r3(sgr),
                             state_retention[l].astype(F32), ts, 1, RET_SAMPLE_SEQS)
        xs = _tail(xs, att.reshape(bs * ts, A_WIDTH), rn.reshape(bs * ts, R_WIDTH), sga, sgb, *tail_w)
        outs[3].append(kt.reshape(bs, ts, A_HEADS, A_HEAD_DIM).astype(cache_attn_k.dtype))
        outs[4].append(vt.reshape(bs, ts, A_HEADS, A_HEAD_DIM).astype(cache_attn_v.dtype))
        outs[5].append(s_s.astype(state_retention.dtype))

    return (xp.reshape(bp, sp, D_MODEL), xs.reshape(bs, ts, D_MODEL),
            jnp.stack(outs[0]), jnp.stack(outs[1]), jnp.stack(outs[2]),
            jnp.stack(outs[3]), jnp.stack(outs[4]), jnp.stack(outs[5]))
```

```python
import functools
import math

import jax
import jax.numpy as jnp
import numpy as np
from jax import lax
from jax.experimental import pallas as pl
from jax.experimental.pallas import tpu as pltpu

F32 = jnp.float32
BF16 = jnp.bfloat16

D_MODEL = 1024
CHUNK = 64
LEFT_CHUNKS = 8
ATT_WINDOW = LEFT_CHUNKS * CHUNK
A_HEADS = 8
A_HEAD_DIM = 64
A_WIDTH = A_HEADS * A_HEAD_DIM
REL_CLIP = 128
R_HEADS = 4
R_DIM = 128
R_WIDTH = R_HEADS * R_DIM
ROPE_BASE = 10000.0
D_FF = 4 * D_MODEL
NORM_EPS = 1e-6
GN_EPS = 1e-6
PAST_LEN = 4096

IN_SPLITS = (A_WIDTH, A_WIDTH, A_WIDTH, R_WIDTH, R_WIDTH, R_WIDTH, R_WIDTH, D_MODEL, D_MODEL)
IN_COLS = sum(IN_SPLITS)
IN_OFFS = tuple(int(o) for o in np.cumsum((0,) + IN_SPLITS))

LANES = 128
BF16_SUBLANES = 16
MXU_N = 256
ROW_TILE = 1024
SUB_ROWS = 256
ATT_Q_TILE = 256
ATT_KEYS = ATT_WINDOW + ATT_Q_TILE
ATT_QB = 8
ATT_AHEAD = 3
PV_EXTRA_ROWS = 16
RET_TILE = 256
RET_BLOCKS = 4
RET_SAMPLE_SEQS = 8
FF_CHUNK = 1024
VMEM_LIMIT = 56 * 1024 * 1024
VMEM_LIMIT_IN_PROJ = 60 * 1024 * 1024
NEG_BIG = -1e30
LOG2E = math.log2(math.e)
ATT_Q_SCALE = (A_HEAD_DIM ** -0.5) * LOG2E
LOG_DECAY = tuple(math.log1p(-(2.0 ** (-5 - h))) for h in range(R_HEADS))


def _resident(shape):
    nd = len(shape)
    return pl.BlockSpec(shape, lambda *_: (0,) * nd, pipeline_mode=pl.Buffered(1))


def _rms(x, g):
    ms = jnp.mean(x * x, axis=-1, keepdims=True)
    return x * lax.rsqrt(ms + NORM_EPS) * g


def _inproj_kernel(*refs, v_transposed, n_cast):
    x_ref, g_ref, w_ref, wvt_ref, cs_ref, sn_ref = refs[:6]
    cast_in = refs[6:6 + n_cast]
    (qa_ref, ka_ref, va_ref, qr_ref, kr_ref, vr_ref, sgr_ref, sga_ref, sgb_ref,
     kt_ref, vt_ref) = refs[6 + n_cast:17 + n_cast]
    cast_out = refs[17 + n_cast:]
    for src, dst in zip(cast_in, cast_out):
        dst[...] = src[...].astype(BF16)
    row_tile, tail_rows = x_ref.shape[0], kt_ref.shape[0]
    tail_start = row_tile - tail_rows
    for r0 in range(0, row_tile, SUB_ROWS):
        rows = slice(r0, r0 + SUB_ROWS)
        in_tail = r0 >= tail_start
        trows = slice(r0 - tail_start, r0 - tail_start + SUB_ROWS)
        h = _rms(x_ref[rows, :], g_ref[...]).astype(BF16)

        def seg(s):
            return jnp.dot(h, w_ref[:, IN_OFFS[s]:IN_OFFS[s + 1]], preferred_element_type=F32)

        cs = cs_ref[rows, :]
        sn = sn_ref[rows, :]

        def rope(z, scale):
            parts = []
            for hh in range(R_HEADS):
                zh = z[:, hh * R_DIM:(hh + 1) * R_DIM]
                r = zh * cs + pltpu.roll(zh, R_DIM // 2, 1) * sn
                if scale != 1.0:
                    r = r * scale
                parts.append(r.astype(BF16))
            return jnp.concatenate(parts, axis=1)

        qa_ref[rows, :] = (seg(0) * ATT_Q_SCALE).astype(BF16)
        ka = seg(1)
        ka_ref[rows, :] = ka.astype(BF16)
        if in_tail:
            kt_ref[trows] = ka.reshape(SUB_ROWS, A_HEADS, A_HEAD_DIM)
        if v_transposed:
            va_t = lax.dot_general(wvt_ref[...], h, (((1,), (1,)), ((), ())), preferred_element_type=F32)
            va_ref[r0 // SUB_ROWS] = va_t.astype(BF16)
            if in_tail:
                vt_ref[trows] = va_t.T.reshape(SUB_ROWS, A_HEADS, A_HEAD_DIM)
        else:
            va = seg(2)
            va_ref[rows, :] = va.astype(BF16)
            if in_tail:
                vt_ref[trows] = va.reshape(SUB_ROWS, A_HEADS, A_HEAD_DIM)
        qr_ref[rows, :] = rope(seg(3), 1.0)
        kr_ref[rows, :] = rope(seg(4), R_DIM ** -0.5)
        vr_ref[rows, :] = seg(5).astype(BF16)
        gr = seg(6)
        sgr_ref[rows, :] = (gr * jax.nn.sigmoid(gr)).astype(BF16)
        sga_ref[rows, :] = jax.nn.sigmoid(seg(7)).astype(BF16)
        sgb_ref[rows, :] = jax.nn.sigmoid(seg(8)).astype(BF16)


def _w_v_transposed(w_in, l):
    assert IN_OFFS[2] % A_WIDTH == 0

    def body(w_ref, o_ref):
        o_ref[...] = w_ref[0].T.astype(BF16)

    return pl.pallas_call(
        body,
        grid=(1,),
        in_specs=[pl.BlockSpec((1, D_MODEL, A_WIDTH), lambda i: (l, 0, IN_OFFS[2] // A_WIDTH))],
        out_specs=pl.BlockSpec((A_WIDTH, D_MODEL), lambda i: (0, 0)),
        out_shape=jax.ShapeDtypeStruct((A_WIDTH, D_MODEL), BF16),
        name="w_v_transpose",
    )(w_in)


def _in_proj(x2, g, w_bf, wvt_bf, cs, sn, seq_rows, tail_rows, v_transposed, cast=()):
    rows = x2.shape[0]
    row_tile = min(seq_rows, ROW_TILE)
    assert tail_rows <= row_tile and tail_rows % SUB_ROWS == 0 and seq_rows % row_tile == 0
    tail_period = seq_rows // row_tile
    n = rows // row_tile
    n_pos = cs.shape[0] // row_tile
    n_sub = row_tile // SUB_ROWS
    row = lambda w: pl.BlockSpec((row_tile, w), lambda i: (i, 0))
    tail = pl.BlockSpec((tail_rows, A_HEADS, A_HEAD_DIM), lambda i: (i // tail_period, 0, 0))
    pos = pl.BlockSpec((row_tile, R_DIM), lambda i: (i % n_pos, 0))
    bf = lambda w: jax.ShapeDtypeStruct((rows, w), BF16)
    tail_shape = jax.ShapeDtypeStruct((rows // seq_rows * tail_rows, A_HEADS, A_HEAD_DIM), F32)
    if v_transposed:
        v_spec = pl.BlockSpec((n_sub, A_WIDTH, SUB_ROWS), lambda i: (i, 0, 0))
        v_shape = jax.ShapeDtypeStruct((rows // SUB_ROWS, A_WIDTH, SUB_ROWS), BF16)
    else:
        v_spec, v_shape = row(A_WIDTH), bf(A_WIDTH)
    cast_specs, cast_shapes = [], []
    for w in cast:
        assert w.shape[0] % n == 0 and (w.shape[0] // n) % BF16_SUBLANES == 0
        cast_specs.append(pl.BlockSpec((w.shape[0] // n, w.shape[1]), lambda i: (i, 0)))
        cast_shapes.append(jax.ShapeDtypeStruct(w.shape, BF16))
    return pl.pallas_call(
        functools.partial(_inproj_kernel, v_transposed=v_transposed, n_cast=len(cast)),
        grid=(n,),
        in_specs=[row(D_MODEL), _resident((1, D_MODEL)), _resident((D_MODEL, IN_COLS)),
                  _resident((A_WIDTH, D_MODEL)), pos, pos] + cast_specs,
        out_specs=[row(A_WIDTH), row(A_WIDTH), v_spec] + [row(R_WIDTH)] * 4 + [row(D_MODEL)] * 2
        + [tail, tail] + cast_specs,
        out_shape=[bf(A_WIDTH), bf(A_WIDTH), v_shape] + [bf(R_WIDTH)] * 4 + [bf(D_MODEL)] * 2
        + [tail_shape, tail_shape] + cast_shapes,
        compiler_params=pltpu.CompilerParams(
            dimension_semantics=("arbitrary",), vmem_limit_bytes=VMEM_LIMIT_IN_PROJ),
        name="in_proj",
    )(x2, g, w_bf, wvt_bf, cs, sn, *cast)


def _attn_sample_kernel(q_ref, kn_ref, vn_ref, kct_ref, vct_ref, bias_ref, o_ref):
    tq = q_ref.shape[1]
    lc = kct_ref.shape[3]
    n_pad = bias_ref.shape[2] - lc
    q_all = q_ref[0]
    zrows = jnp.zeros((n_pad - tq, A_WIDTH), BF16)
    kn_pad = jnp.concatenate([kn_ref[0], zrows], axis=0)
    vn_pad = jnp.concatenate([vn_ref[0], zrows], axis=0)
    lane_q = lax.broadcasted_iota(jnp.int32, (tq, LANES), 1)
    lane_o = lax.broadcasted_iota(jnp.int32, (tq, MXU_N), 1)
    heads_per_group = MXU_N // A_HEAD_DIM
    pairs_per_group = heads_per_group // 2
    nt = (((1,), (1,)), ((), ()))
    outs = []
    for grp in range(A_HEADS // heads_per_group):
        e_parts, d_parts = [], []
        for pg in range(pairs_per_group):
            pair = grp * pairs_per_group + pg
            q2 = q_all[:, pair * LANES:(pair + 1) * LANES]
            zero = jnp.zeros_like(q2)
            q_st = jnp.concatenate([jnp.where(lane_q < A_HEAD_DIM, q2, zero),
                                    jnp.where(lane_q >= A_HEAD_DIM, q2, zero)], axis=0)
            kt_pair = kct_ref[0, 2 * pair:2 * pair + 2].reshape(LANES, lc).astype(BF16)
            s_c = jnp.dot(q_st, kt_pair, preferred_element_type=F32)
            s_n = lax.dot_general(q_st, kn_pad[:, pair * LANES:(pair + 1) * LANES], nt,
                                  preferred_element_type=F32)
            s = jnp.concatenate([s_c, s_n], axis=1)
            s = s + jnp.concatenate([bias_ref[2 * pair], bias_ref[2 * pair + 1]], axis=0)
            m = jnp.max(s, axis=-1, keepdims=True)
            e = jnp.exp2(s - m)
            d_parts.append(jnp.sum(e, axis=-1, keepdims=True))
            e_parts.append(e.astype(BF16))
        p_st = jnp.concatenate(e_parts, axis=0)
        vt_grp = vct_ref[0, heads_per_group * grp:heads_per_group * (grp + 1)].reshape(MXU_N, lc).astype(BF16)
        o = lax.dot_general(p_st[:, 0:lc], vt_grp, nt, preferred_element_type=F32)
        o = o + jnp.dot(p_st[:, lc:], vn_pad[:, grp * MXU_N:(grp + 1) * MXU_N], preferred_element_type=F32)
        o = o * (1.0 / jnp.concatenate(d_parts, axis=0))
        acc = jnp.zeros((tq, MXU_N), F32)
        for hg in range(heads_per_group):
            olo = hg * A_HEAD_DIM
            acc = jnp.where((lane_o >= olo) & (lane_o < olo + A_HEAD_DIM), o[hg * tq:(hg + 1) * tq], acc)
        outs.append(acc.astype(BF16))
    o_ref[0] = jnp.concatenate(outs, axis=1)


def _reduce_rows(x, op):
    parts = [x[r:r + 8] for r in range(0, x.shape[0], 8)]
    while len(parts) > 1:
        paired = [op(parts[i], parts[i + 1]) for i in range(0, len(parts) - 1, 2)]
        parts = paired + parts[len(parts) - len(parts) % 2:]
    red = jnp.max if op is jnp.maximum else jnp.sum
    return red(parts[0], axis=0, keepdims=True)


def _attn_prompt_kernel(q_ref, k_ref, vt_ref, bmt_ref, o_ref):
    n_c = ATT_KEYS // ATT_Q_TILE
    n_left = n_c - 1
    lane_q = lax.broadcasted_iota(jnp.int32, (ATT_Q_TILE, LANES), 1)
    ones_rows = jnp.ones((PV_EXTRA_ROWS, ATT_Q_TILE), BF16)
    band_keys = (LEFT_CHUNKS + 2) * CHUNK
    n_lt = ATT_Q_TILE // LANES

    def pieces(t):
        lo, hi = LANES * t, LANES * t + band_keys
        out = []
        for c in range(n_c):
            a, b = max(lo, c * ATT_Q_TILE), min(hi, (c + 1) * ATT_Q_TILE)
            if a < b:
                out.append((c, a - c * ATT_Q_TILE, b - c * ATT_Q_TILE))
        return out

    def window(qb):
        j = pl.program_id(1) * ATT_QB + qb
        kc = [jnp.maximum(j - n_left + c, 0) for c in range(n_c)]
        bc = [jnp.where(j - n_left + c >= 0, c, n_c) for c in range(n_c)]
        return kc, bc

    def scores(qb, hd):
        kc, bc = window(qb)
        pair = hd // 2
        q2 = q_ref[0, qb * ATT_Q_TILE:(qb + 1) * ATT_Q_TILE, pair * LANES:(pair + 1) * LANES]
        k2 = jnp.concatenate(
            [k_ref[0, pl.ds(pl.multiple_of(kc[c] * ATT_Q_TILE, ATT_Q_TILE), ATT_Q_TILE),
                   pair * LANES:(pair + 1) * LANES] for c in range(n_c)], axis=0)
        lo = (hd % 2) * A_HEAD_DIM
        qm = jnp.where((lane_q >= lo) & (lane_q < lo + A_HEAD_DIM), q2, jnp.zeros_like(q2))
        st = lax.dot_general(k2, qm, (((1,), (1,)), ((), ())), preferred_element_type=F32)
        tiles = []
        for t in range(n_lt):
            lanes = slice(t * LANES, (t + 1) * LANES)
            tiles.append([st[c * ATT_Q_TILE + a:c * ATT_Q_TILE + b, lanes] + bmt_ref[hd, bc[c], a:b, lanes]
                          for c, a, b in pieces(t)])
        return tiles

    def softmax(tiles):
        cols = []
        for t in range(n_lt):
            st = jnp.concatenate(tiles[t], axis=0)
            m = _reduce_rows(st, jnp.maximum)
            e = jnp.exp2(st - m).astype(BF16)
            zero = lambda n: [jnp.zeros((n, LANES), BF16)] if n else []
            cols.append(jnp.concatenate(zero(LANES * t) + [e] + zero(ATT_KEYS - band_keys - LANES * t), axis=0))
        return jnp.concatenate(cols, axis=1)

    def pv(qb, hd, pt):
        kc, _ = window(qb)
        ot = jnp.zeros((A_HEAD_DIM + PV_EXTRA_ROWS, ATT_Q_TILE), F32)
        for c in range(n_c):
            vt_c = vt_ref[0, kc[c], hd * A_HEAD_DIM:(hd + 1) * A_HEAD_DIM, :]
            ot = ot + jnp.dot(jnp.concatenate([vt_c, ones_rows], axis=0),
                              pt[c * ATT_Q_TILE:(c + 1) * ATT_Q_TILE, :], preferred_element_type=F32)
        return ot[0:A_HEAD_DIM] * (1.0 / ot[A_HEAD_DIM:A_HEAD_DIM + 1])

    items = [(qb, hd) for qb in range(ATT_QB) for hd in range(A_HEADS)]
    tiles = {it: scores(*it) for it in items[:ATT_AHEAD]}
    prev = None
    for n, (qb, hd) in enumerate(items):
        pt = softmax(tiles.pop((qb, hd)))
        if n + ATT_AHEAD < len(items):
            nxt = items[n + ATT_AHEAD]
            tiles[nxt] = scores(*nxt)
        ot = pv(qb, hd, pt)
        if hd % 2 == 0:
            prev = ot
        else:
            pair_t = jnp.concatenate([prev, ot], axis=0).T.astype(BF16)
            o_ref[0, qb * ATT_Q_TILE:(qb + 1) * ATT_Q_TILE, (hd // 2) * LANES:(hd // 2 + 1) * LANES] = pair_t


def _attn_prompt(q, k, vt, bmt):
    b, s, _ = q.shape
    assert vt.shape == (b, s // ATT_Q_TILE, A_WIDTH, ATT_Q_TILE)
    blk = pl.BlockSpec((1, ATT_QB * ATT_Q_TILE, A_WIDTH), lambda bi, j: (bi, j, 0))
    full = pl.BlockSpec((1, s, A_WIDTH), lambda bi, j: (bi, 0, 0))
    full_t = pl.BlockSpec((1,) + vt.shape[1:], lambda bi, j: (bi, 0, 0, 0))
    return pl.pallas_call(
        _attn_prompt_kernel,
        grid=(b, s // (ATT_QB * ATT_Q_TILE)),
        in_specs=[blk, full, full_t, _resident(bmt.shape)],
        out_specs=blk,
        out_shape=jax.ShapeDtypeStruct((b, s, A_WIDTH), BF16),
        compiler_params=pltpu.CompilerParams(
            dimension_semantics=("arbitrary", "arbitrary"), vmem_limit_bytes=VMEM_LIMIT),
        name="attn_prompt",
    )(q, k, vt, bmt)


def _attn_sample(q, kn, vn, kct, vct, bias):
    b, t, _ = q.shape
    new = pl.BlockSpec((1, t, A_WIDTH), lambda bi: (bi, 0, 0))
    cache = pl.BlockSpec((1,) + kct.shape[1:], lambda bi: (bi, 0, 0, 0))
    return pl.pallas_call(
        _attn_sample_kernel,
        grid=(b,),
        in_specs=[new, new, new, cache, cache, _resident(bias.shape)],
        out_specs=new,
        out_shape=jax.ShapeDtypeStruct((b, t, A_WIDTH), BF16),
        compiler_params=pltpu.CompilerParams(
            dimension_semantics=("arbitrary",), vmem_limit_bytes=VMEM_LIMIT),
        name="attn_sample",
    )(q, kn, vn, kct, vct, bias)


def _retention_kernel(q_ref, k_ref, v_ref, sg_ref, s0_ref, rn_ref, sout_ref,
                      state, dmat, rdec, kdec):
    bi = pl.program_id(0)
    j = pl.program_id(1)
    t = dmat.shape[1]
    n_seq = q_ref.shape[0]
    n_blk = q_ref.shape[1] // t

    @pl.when((bi == 0) & (j == 0))
    def _():
        row = lax.broadcasted_iota(jnp.int32, (t, t), 0)
        col = lax.broadcasted_iota(jnp.int32, (t, t), 1)
        diff = (row - col).astype(F32)
        rowl = lax.broadcasted_iota(jnp.int32, (t, R_DIM), 0).astype(F32)
        for hh in range(R_HEADS):
            lg = LOG_DECAY[hh]
            dmat[hh] = jnp.where(diff >= 0, jnp.exp(jnp.maximum(diff, 0.0) * lg), 0.0)
            rdec[hh] = jnp.exp((rowl + 1.0) * lg)
            kdec[hh] = jnp.exp((t - 1.0 - rowl) * lg)

    @pl.when(j == 0)
    def _():
        state[...] = s0_ref[...]

    s_cur = {(sq, hh): state[sq, hh] for sq in range(n_seq) for hh in range(R_HEADS)}

    def recur(sq, blk, hh):
        rows, sl = slice(blk * t, (blk + 1) * t), slice(hh * R_DIM, (hh + 1) * R_DIM)
        q = q_ref[sq, rows, sl]
        k = k_ref[sq, rows, sl]
        v = v_ref[sq, rows, sl]
        s_prev = s_cur[sq, hh]
        a = lax.dot_general(q, k, (((1,), (1,)), ((), ())), preferred_element_type=F32) * dmat[hh]
        intra = jnp.dot(a.astype(BF16), v, preferred_element_type=F32)
        cross = jnp.dot(q, s_prev.astype(BF16), preferred_element_type=F32) * rdec[hh]
        kd = (k.astype(F32) * kdec[hh]).astype(BF16)
        upd = lax.dot_general(kd, v, (((0,), (0,)), ((), ())), preferred_element_type=F32)
        s_cur[sq, hh] = s_prev * math.exp(t * LOG_DECAY[hh]) + upd
        return intra + cross

    def norm_gate(sq, blk, hh, o):
        rows, sl = slice(blk * t, (blk + 1) * t), slice(hh * R_DIM, (hh + 1) * R_DIM)
        mu = jnp.mean(o, axis=-1, keepdims=True)
        d = o - mu
        var = jnp.mean(d * d, axis=-1, keepdims=True)
        rn = d * lax.rsqrt(var + GN_EPS) * sg_ref[sq, rows, sl].astype(F32)
        rn_ref[sq, rows, sl] = rn.astype(BF16)

    items = [(sq, blk, hh) for sq in range(n_seq) for blk in range(n_blk) for hh in range(R_HEADS)]
    o_next = recur(*items[0])
    for n, it in enumerate(items):
        o = o_next
        if n + 1 < len(items):
            o_next = recur(*items[n + 1])
        norm_gate(*it, o)

    for (sq, hh), s_new in s_cur.items():
        state[sq, hh] = s_new
        sout_ref[sq, hh] = s_new


def _retention(q, k, v, sg, s0, tile, n_blk, n_seq):
    b, s, _ = q.shape
    blk = pl.BlockSpec((n_seq, n_blk * tile, R_WIDTH), lambda bi, j: (bi, j, 0))
    st = pl.BlockSpec((n_seq, R_HEADS, R_DIM, R_DIM), lambda bi, j: (bi, 0, 0, 0))
    return pl.pallas_call(
        _retention_kernel,
        grid=(b // n_seq, s // (n_blk * tile)),
        in_specs=[blk, blk, blk, blk, st],
        out_specs=[blk, st],
        out_shape=[jax.ShapeDtypeStruct((b, s, R_WIDTH), BF16),
                   jax.ShapeDtypeStruct((b, R_HEADS, R_DIM, R_DIM), F32)],
        scratch_shapes=[pltpu.VMEM((n_seq, R_HEADS, R_DIM, R_DIM), F32),
                        pltpu.VMEM((R_HEADS, tile, tile), F32),
                        pltpu.VMEM((R_HEADS, tile, R_DIM), F32),
                        pltpu.VMEM((R_HEADS, tile, R_DIM), F32)],
        compiler_params=pltpu.CompilerParams(
            dimension_semantics=("arbitrary", "arbitrary"), vmem_limit_bytes=VMEM_LIMIT),
        name="retention",
    )(q, k, v, sg, s0)


def _tail_kernel(x_ref, att_ref, rn_ref, sga_ref, sgb_ref, wao_ref, wro_ref, wo_ref,
                 g1_ref, g2_ref, g3_ref, wup_ref, wdn_ref, y_ref):
    n_sub = x_ref.shape[0] // SUB_ROWS
    rows = [slice(r * SUB_ROWS, (r + 1) * SUB_ROWS) for r in range(n_sub)]
    n_ff = D_FF // FF_CHUNK
    assert n_ff == 4
    m, mix, x1, h2, f = ([None] * n_sub for _ in range(5))

    def merge(s):
        r = rows[s]
        ya = jnp.dot(att_ref[r, :], wao_ref[...], preferred_element_type=F32)
        yr = jnp.dot(rn_ref[r, :], wro_ref[...], preferred_element_type=F32)
        m[s] = (sga_ref[r, :].astype(F32) * ya + sgb_ref[r, :].astype(F32) * yr).astype(BF16)

    def project(s):
        mix[s] = jnp.dot(m[s], wo_ref[...], preferred_element_type=F32)

    def norms(s):
        x1[s] = x_ref[rows[s], :] + _rms(mix[s], g1_ref[...])
        h2[s] = _rms(x1[s], g2_ref[...]).astype(BF16)
        f[s] = jnp.zeros_like(x1[s])

    def ffn(s, c):
        u = jnp.dot(h2[s], wup_ref[:, c * FF_CHUNK:(c + 1) * FF_CHUNK], preferred_element_type=F32)
        u = jnp.square(jnp.maximum(u, 0.0)).astype(BF16)
        f[s] = f[s] + jnp.dot(u, wdn_ref[c * FF_CHUNK:(c + 1) * FF_CHUNK, :], preferred_element_type=F32)

    def finish(s):
        y_ref[rows[s], :] = x1[s] + _rms(f[s], g3_ref[...])

    def when(s, fn):
        if s < n_sub:
            fn(s)

    merge(0)
    when(1, merge)
    project(0)
    norms(0)
    when(1, project)
    ffn(0, 0)
    when(1, norms)
    for s in range(n_sub):
        when(s + 2, merge)
        ffn(s, 1)
        when(s + 2, project)
        ffn(s, 2)
        when(s + 2, norms)
        ffn(s, 3)
        finish(s)
        when(s + 1, lambda t: ffn(t, 0))


def _tail(x2, att, rn, sga, sgb, wao, wro, wo, g1, g2, g3, wup, wdn):
    rows = x2.shape[0]
    row_tile = min(rows, ROW_TILE)
    row = lambda w: pl.BlockSpec((row_tile, w), lambda i: (i, 0))
    gspec = _resident((1, D_MODEL))
    return pl.pallas_call(
        _tail_kernel,
        grid=(rows // row_tile,),
        in_specs=[row(D_MODEL), row(A_WIDTH), row(R_WIDTH), row(D_MODEL), row(D_MODEL),
                  _resident(wao.shape), _resident(wro.shape), _resident(wo.shape),
                  gspec, gspec, gspec, _resident(wup.shape), _resident(wdn.shape)],
        out_specs=row(D_MODEL),
        out_shape=jax.ShapeDtypeStruct((rows, D_MODEL), F32),
        compiler_params=pltpu.CompilerParams(
            dimension_semantics=("arbitrary",), vmem_limit_bytes=VMEM_LIMIT),
        name="tail",
    )(x2, att, rn, sga, sgb, wao, wro, wo, g1, g2, g3, wup, wdn)


def _rope_tables(pos):
    half = R_DIM // 2
    inv = ROPE_BASE ** (-jnp.arange(half, dtype=F32) / half)
    ang = pos[:, None] * inv[None, :]
    cos, sin = jnp.cos(ang), jnp.sin(ang)
    return jnp.concatenate([cos, cos], axis=1), jnp.concatenate([-sin, sin], axis=1)


BIAS_ROW = 1024
BIAS_TPAD = 384


def _bias_kernel(t_ref, bpt_ref, bs_ref, *, t_s, nk_valid_s):
    tq = ATT_Q_TILE
    t8 = jnp.broadcast_to(t_ref[0], (8, BIAS_TPAD))
    t1 = t8.astype(BF16)
    r1 = t8 - t1.astype(F32)
    t2 = r1.astype(BF16)
    t3 = (r1 - t2.astype(F32)).astype(BF16)

    def generator(width, offset, sign):
        k_idx = lax.broadcasted_iota(jnp.int32, (BIAS_TPAD, width), 0)
        x_idx = lax.broadcasted_iota(jnp.int32, (BIAS_TPAD, width), 1)
        sel = jnp.where(k_idx == jnp.clip(offset + sign * x_idx, 0, 2 * REL_CLIP), 1.0, 0.0).astype(BF16)
        gen = (jnp.dot(t1, sel, preferred_element_type=F32) + jnp.dot(t2, sel, preferred_element_type=F32)
               + jnp.dot(t3, sel, preferred_element_type=F32))
        return gen[0:1] * LOG2E

    def toeplitz(gen, rows):
        width = gen.shape[1]
        return pltpu.roll(jnp.broadcast_to(gen, (rows, width)), width - tq, 1, stride=1, stride_axis=0)

    qc = lax.shift_right_logical(lax.broadcasted_iota(jnp.int32, (tq, tq), 1), 6)
    key = lax.broadcasted_iota(jnp.int32, (tq, tq), 0)
    for c in range(bpt_ref.shape[1]):
        if c * tq < ATT_KEYS:
            gen = generator(2 * tq, ATT_WINDOW + REL_CLIP - tq - tq * c, 1)
            kc = lax.shift_right_logical(key + c * tq, 6)
            valid = (kc >= qc) & (kc <= qc + LEFT_CHUNKS)
            bpt_ref[0, c] = jnp.where(valid, toeplitz(gen, tq)[:, 0:tq], NEG_BIG)
        else:
            bpt_ref[0, c] = jnp.full((tq, tq), NEG_BIG, F32)
    nk_s = bs_ref.shape[2]
    gen_s = generator(BIAS_ROW, ATT_WINDOW + tq + REL_CLIP, -1)
    lane = lax.broadcasted_iota(jnp.int32, (t_s, nk_s), 1)
    bs_ref[0] = jnp.where(lane < nk_valid_s, toeplitz(gen_s, t_s)[:, 0:nk_s], NEG_BIG)


def _build_bias(table, t_s, l_s, nk_s):
    assert CHUNK == 64 and l_s == ATT_WINDOW and t_s <= ATT_Q_TILE and nk_s <= ATT_KEYS
    hn, tl = table.shape
    tpad = jnp.pad(table.astype(F32), ((0, 0), (0, BIAS_TPAD - tl))).reshape(hn, 1, BIAS_TPAD)
    n_c = ATT_KEYS // ATT_Q_TILE + 1
    return pl.pallas_call(
        functools.partial(_bias_kernel, t_s=t_s, nk_valid_s=l_s + t_s),
        grid=(hn,),
        in_specs=[pl.BlockSpec((1, 1, BIAS_TPAD), lambda h: (h, 0, 0))],
        out_specs=[pl.BlockSpec((1, n_c, ATT_Q_TILE, ATT_Q_TILE), lambda h: (h, 0, 0, 0)),
                   pl.BlockSpec((1, t_s, nk_s), lambda h: (h, 0, 0))],
        out_shape=[jax.ShapeDtypeStruct((hn, n_c, ATT_Q_TILE, ATT_Q_TILE), F32),
                   jax.ShapeDtypeStruct((hn, t_s, nk_s), F32)],
        compiler_params=pltpu.CompilerParams(
            dimension_semantics=("arbitrary",), vmem_limit_bytes=VMEM_LIMIT),
        name="rel_bias",
    )(tpad)


def kernel(x_prompt, x_sample, cache_attn_k, cache_attn_v, state_retention, norm_mix_pre, w_in,
           rel_bias_table, w_attn_out, w_ret_out, w_o, norm_mix_post, norm_ffn_pre, w_ff_up,
           w_ff_down, norm_ffn_post):
    depth = w_in.shape[0]
    bp, sp, _ = x_prompt.shape
    bs, ts, _ = x_sample.shape
    lc = cache_attn_k.shape[2]
    n_keys_s = -(-(lc + ts) // LANES) * LANES

    cs_p, sn_p = _rope_tables(jnp.arange(sp, dtype=F32))
    pos_s = PAST_LEN + jnp.arange(ts, dtype=F32)
    cs_s, sn_s = _rope_tables(jnp.tile(pos_s, bs))

    xp = x_prompt.reshape(bp * sp, D_MODEL)
    xs = x_sample.reshape(bs * ts, D_MODEL)
    outs = [[] for _ in range(6)]
    for l in range(depth):
        w_in_b = w_in[l].astype(BF16)
        w_v_t = _w_v_transposed(w_in, l)
        g0 = norm_mix_pre[l].reshape(1, D_MODEL)
        g1 = norm_mix_post[l].reshape(1, D_MODEL)
        g2 = norm_ffn_pre[l].reshape(1, D_MODEL)
        g3 = norm_ffn_post[l].reshape(1, D_MODEL)
        bias_p, bias_s = _build_bias(rel_bias_table[l], ts, lc, n_keys_s)

        qa, ka, va_t, qr, kr, vr, sgr, sga, sgb, kt, vt, wao, wro, wo, wup, wdn = _in_proj(
            xp, g0, w_in_b, w_v_t, cs_p, sn_p, seq_rows=sp, tail_rows=ATT_WINDOW, v_transposed=True,
            cast=(w_attn_out[l], w_ret_out[l], w_o[l], w_ff_up[l], w_ff_down[l]))
        tail_w = (wao, wro, wo, g1, g2, g3, wup, wdn)
        r3 = lambda a: a.reshape(bp, sp, a.shape[-1])
        att = _attn_prompt(r3(qa), r3(ka), va_t.reshape(bp, sp // SUB_ROWS, A_WIDTH, SUB_ROWS), bias_p)
        rn, s_p = _retention(r3(qr), r3(kr), r3(vr), r3(sgr),
                             jnp.zeros((bp, R_HEADS, R_DIM, R_DIM), F32), RET_TILE, RET_BLOCKS, 1)
        xp = _tail(xp, att.reshape(bp * sp, A_WIDTH), rn.reshape(bp * sp, R_WIDTH), sga, sgb, *tail_w)
        outs[0].append(kt.reshape(bp, ATT_WINDOW, A_HEADS, A_HEAD_DIM))
        outs[1].append(vt.reshape(bp, ATT_WINDOW, A_HEADS, A_HEAD_DIM))
        outs[2].append(s_p)

        qa, ka, va, qr, kr, vr, sgr, sga, sgb, kt, vt = _in_proj(
            xs, g0, w_in_b, w_v_t, cs_s, sn_s, seq_rows=bs * ts, tail_rows=bs * ts, v_transposed=False)
        r3 = lambda a: a.reshape(bs, ts, a.shape[-1])
        att = _attn_sample(r3(qa), r3(ka), r3(va),
                           cache_attn_k[l].transpose(0, 2, 3, 1), cache_attn_v[l].transpose(0, 2, 3, 1),
                           bias_s)
        rn, s_s = _retention(r3(qr), r3(kr), r3(vr), r3(sgr),
                             state_retention[l].astype(F32), ts, 1, RET_SAMPLE_SEQS)
        xs = _tail(xs, att.reshape(bs * ts, A_WIDTH), rn.reshape(bs * ts, R_WIDTH), sga, sgb, *tail_w)
        outs[3].append(kt.reshape(bs, ts, A_HEADS, A_HEAD_DIM).astype(cache_attn_k.dtype))
        outs[4].append(vt.reshape(bs, ts, A_HEADS, A_HEAD_DIM).astype(cache_attn_v.dtype))
        outs[5].append(s_s.astype(state_retention.dtype))

    return (xp.reshape(bp, sp, D_MODEL), xs.reshape(bs, ts, D_MODEL),
            jnp.stack(outs[0]), jnp.stack(outs[1]), jnp.stack(outs[2]),
            jnp.stack(outs[3]), jnp.stack(outs[4]), jnp.stack(outs[5]))
```

```python
import functools
import math

import jax
import jax.numpy as jnp
import numpy as np
from jax import lax
from jax.experimental import pallas as pl
from jax.experimental.pallas import tpu as pltpu

F32 = jnp.float32
BF16 = jnp.bfloat16

D_MODEL = 1024
CHUNK = 64
LEFT_CHUNKS = 8
ATT_WINDOW = LEFT_CHUNKS * CHUNK
A_HEADS = 8
A_HEAD_DIM = 64
A_WIDTH = A_HEADS * A_HEAD_DIM
REL_CLIP = 128
R_HEADS = 4
R_DIM = 128
R_WIDTH = R_HEADS * R_DIM
ROPE_BASE = 10000.0
D_FF = 4 * D_MODEL
NORM_EPS = 1e-6
GN_EPS = 1e-6
PAST_LEN = 4096

IN_SPLITS = (A_WIDTH, A_WIDTH, A_WIDTH, R_WIDTH, R_WIDTH, R_WIDTH, R_WIDTH, D_MODEL, D_MODEL)
IN_COLS = sum(IN_SPLITS)
IN_OFFS = tuple(int(o) for o in np.cumsum((0,) + IN_SPLITS))

LANES = 128
BF16_SUBLANES = 16
MXU_N = 256
ROW_TILE = 1024
SUB_ROWS = 256
ATT_Q_TILE = 256
ATT_KEYS = ATT_WINDOW + ATT_Q_TILE
ATT_QB = 8
ATT_AHEAD = 3
PV_EXTRA_ROWS = 16
RET_TILE = 256
RET_BLOCKS = 4
RET_SAMPLE_SEQS = 8
FF_CHUNK = 1024
VMEM_LIMIT = 56 * 1024 * 1024
VMEM_LIMIT_IN_PROJ = 60 * 1024 * 1024
NEG_BIG = -1e30
LOG2E = math.log2(math.e)
ATT_Q_SCALE = (A_HEAD_DIM ** -0.5) * LOG2E
LOG_DECAY = tuple(math.log1p(-(2.0 ** (-5 - h))) for h in range(R_HEADS))


def _resident(shape):
    nd = len(shape)
    return pl.BlockSpec(shape, lambda *_: (0,) * nd, pipeline_mode=pl.Buffered(1))


def _rms(x, g):
    ms = jnp.mean(x * x, axis=-1, keepdims=True)
    return x * lax.rsqrt(ms + NORM_EPS) * g


def _inproj_kernel(*refs, v_transposed, n_cast):
    x_ref, g_ref, w_ref, wvt_ref, cs_ref, sn_ref = refs[:6]
    cast_in = refs[6:6 + n_cast]
    (qa_ref, ka_ref, va_ref, qr_ref, kr_ref, vr_ref, sgr_ref, sga_ref, sgb_ref,
     kt_ref, vt_ref) = refs[6 + n_cast:17 + n_cast]
    cast_out = refs[17 + n_cast:]
    for src, dst in zip(cast_in, cast_out):
        dst[...] = src[...].astype(BF16)
    row_tile, tail_rows = x_ref.shape[0], kt_ref.shape[0]
    tail_start = row_tile - tail_rows
    for r0 in range(0, row_tile, SUB_ROWS):
        rows = slice(r0, r0 + SUB_ROWS)
        in_tail = r0 >= tail_start
        trows = slice(r0 - tail_start, r0 - tail_start + SUB_ROWS)
        h = _rms(x_ref[rows, :], g_ref[...]).astype(BF16)

        def seg(s):
            return jnp.dot(h, w_ref[:, IN_OFFS[s]:IN_OFFS[s + 1]], preferred_element_type=F32)

        cs = cs_ref[rows, :]
        sn = sn_ref[rows, :]

        def rope(z, scale):
            parts = []
            for hh in range(R_HEADS):
                zh = z[:, hh * R_DIM:(hh + 1) * R_DIM]
                r = zh * cs + pltpu.roll(zh, R_DIM // 2, 1) * sn
                if scale != 1.0:
                    r = r * scale
                parts.append(r.astype(BF16))
            return jnp.concatenate(parts, axis=1)

        qa_ref[rows, :] = (seg(0) * ATT_Q_SCALE).astype(BF16)
        ka = seg(1)
        ka_ref[rows, :] = ka.astype(BF16)
        if in_tail:
            kt_ref[trows] = ka.reshape(SUB_ROWS, A_HEADS, A_HEAD_DIM)
        if v_transposed:
            va_t = lax.dot_general(wvt_ref[...], h, (((1,), (1,)), ((), ())), preferred_element_type=F32)
            va_ref[r0 // SUB_ROWS] = va_t.astype(BF16)
            if in_tail:
                vt_ref[trows] = va_t.T.reshape(SUB_ROWS, A_HEADS, A_HEAD_DIM)
        else:
            va = seg(2)
            va_ref[rows, :] = va.astype(BF16)
            if in_tail:
                vt_ref[trows] = va.reshape(SUB_ROWS, A_HEADS, A_HEAD_DIM)
        qr_ref[rows, :] = rope(seg(3), 1.0)
        kr_ref[rows, :] = rope(seg(4), R_DIM ** -0.5)
        vr_ref[rows, :] = seg(5).astype(BF16)
        gr = seg(6)
        sgr_ref[rows, :] = (gr * jax.nn.sigmoid(gr)).astype(BF16)
        sga_ref[rows, :] = jax.nn.sigmoid(seg(7)).astype(BF16)
        sgb_ref[rows, :] = jax.nn.sigmoid(seg(8)).astype(BF16)


def _w_v_transposed(w_in, l):
    assert IN_OFFS[2] % A_WIDTH == 0

    def body(w_ref, o_ref):
        o_ref[...] = w_ref[0].T.astype(BF16)

    return pl.pallas_call(
        body,
        grid=(1,),
        in_specs=[pl.BlockSpec((1, D_MODEL, A_WIDTH), lambda i: (l, 0, IN_OFFS[2] // A_WIDTH))],
        out_specs=pl.BlockSpec((A_WIDTH, D_MODEL), lambda i: (0, 0)),
        out_shape=jax.ShapeDtypeStruct((A_WIDTH, D_MODEL), BF16),
        name="w_v_transpose",
    )(w_in)


def _in_proj(x2, g, w_bf, wvt_bf, cs, sn, seq_rows, tail_rows, v_transposed, cast=()):
    rows = x2.shape[0]
    row_tile = min(seq_rows, ROW_TILE)
    assert tail_rows <= row_tile and tail_rows % SUB_ROWS == 0 and seq_rows % row_tile == 0
    tail_period = seq_rows // row_tile
    n = rows // row_tile
    n_pos = cs.shape[0] // row_tile
    n_sub = row_tile // SUB_ROWS
    row = lambda w: pl.BlockSpec((row_tile, w), lambda i: (i, 0))
    tail = pl.BlockSpec((tail_rows, A_HEADS, A_HEAD_DIM), lambda i: (i // tail_period, 0, 0))
    pos = pl.BlockSpec((row_tile, R_DIM), lambda i: (i % n_pos, 0))
    bf = lambda w: jax.ShapeDtypeStruct((rows, w), BF16)
    tail_shape = jax.ShapeDtypeStruct((rows // seq_rows * tail_rows, A_HEADS, A_HEAD_DIM), F32)
    if v_transposed:
        v_spec = pl.BlockSpec((n_sub, A_WIDTH, SUB_ROWS), lambda i: (i, 0, 0))
        v_shape = jax.ShapeDtypeStruct((rows // SUB_ROWS, A_WIDTH, SUB_ROWS), BF16)
    else:
        v_spec, v_shape = row(A_WIDTH), bf(A_WIDTH)
    cast_specs, cast_shapes = [], []
    for w in cast:
        assert w.shape[0] % n == 0 and (w.shape[0] // n) % BF16_SUBLANES == 0
        cast_specs.append(pl.BlockSpec((w.shape[0] // n, w.shape[1]), lambda i: (i, 0)))
        cast_shapes.append(jax.ShapeDtypeStruct(w.shape, BF16))
    return pl.pallas_call(
        functools.partial(_inproj_kernel, v_transposed=v_transposed, n_cast=len(cast)),
        grid=(n,),
        in_specs=[row(D_MODEL), _resident((1, D_MODEL)), _resident((D_MODEL, IN_COLS)),
                  _resident((A_WIDTH, D_MODEL)), pos, pos] + cast_specs,
        out_specs=[row(A_WIDTH), row(A_WIDTH), v_spec] + [row(R_WIDTH)] * 4 + [row(D_MODEL)] * 2
        + [tail, tail] + cast_specs,
        out_shape=[bf(A_WIDTH), bf(A_WIDTH), v_shape] + [bf(R_WIDTH)] * 4 + [bf(D_MODEL)] * 2
        + [tail_shape, tail_shape] + cast_shapes,
        compiler_params=pltpu.CompilerParams(
            dimension_semantics=("arbitrary",), vmem_limit_bytes=VMEM_LIMIT_IN_PROJ),
        name="in_proj",
    )(x2, g, w_bf, wvt_bf, cs, sn, *cast)


def _attn_sample_kernel(q_ref, kn_ref, vn_ref, kct_ref, vct_ref, bias_ref, o_ref):
    tq = q_ref.shape[1]
    lc = kct_ref.shape[3]
    n_pad = bias_ref.shape[2] - lc
    q_all = q_ref[0]
    zrows = jnp.zeros((n_pad - tq, A_WIDTH), BF16)
    kn_pad = jnp.concatenate([kn_ref[0], zrows], axis=0)
    vn_pad = jnp.concatenate([vn_ref[0], zrows], axis=0)
    lane_q = lax.broadcasted_iota(jnp.int32, (tq, LANES), 1)
    lane_o = lax.broadcasted_iota(jnp.int32, (tq, MXU_N), 1)
    heads_per_group = MXU_N // A_HEAD_DIM
    pairs_per_group = heads_per_group // 2
    nt = (((1,), (1,)), ((), ()))
    outs = []
    for grp in range(A_HEADS // heads_per_group):
        e_parts, d_parts = [], []
        for pg in range(pairs_per_group):
            pair = grp * pairs_per_group + pg
            q2 = q_all[:, pair * LANES:(pair + 1) * LANES]
            zero = jnp.zeros_like(q2)
            q_st = jnp.concatenate([jnp.where(lane_q < A_HEAD_DIM, q2, zero),
                                    jnp.where(lane_q >= A_HEAD_DIM, q2, zero)], axis=0)
            kt_pair = kct_ref[0, 2 * pair:2 * pair + 2].reshape(LANES, lc).astype(BF16)
            s_c = jnp.dot(q_st, kt_pair, preferred_element_type=F32)
            s_n = lax.dot_general(q_st, kn_pad[:, pair * LANES:(pair + 1) * LANES], nt,
                                  preferred_element_type=F32)
            s = jnp.concatenate([s_c, s_n], axis=1)
            s = s + jnp.concatenate([bias_ref[2 * pair], bias_ref[2 * pair + 1]], axis=0)
            m = jnp.max(s, axis=-1, keepdims=True)
            e = jnp.exp2(s - m)
            d_parts.append(jnp.sum(e, axis=-1, keepdims=True))
            e_parts.append(e.astype(BF16))
        p_st = jnp.concatenate(e_parts, axis=0)
        vt_grp = vct_ref[0, heads_per_group * grp:heads_per_group * (grp + 1)].reshape(MXU_N, lc).astype(BF16)
        o = lax.dot_general(p_st[:, 0:lc], vt_grp, nt, preferred_element_type=F32)
        o = o + jnp.dot(p_st[:, lc:], vn_pad[:, grp * MXU_N:(grp + 1) * MXU_N], preferred_element_type=F32)
        o = o * (1.0 / jnp.concatenate(d_parts, axis=0))
        acc = jnp.zeros((tq, MXU_N), F32)
        for hg in range(heads_per_group):
            olo = hg * A_HEAD_DIM
            acc = jnp.where((lane_o >= olo) & (lane_o < olo + A_HEAD_DIM), o[hg * tq:(hg + 1) * tq], acc)
        outs.append(acc.astype(BF16))
    o_ref[0] = jnp.concatenate(outs, axis=1)


def _reduce_rows(x, op):
    parts = [x[r:r + 8] for r in range(0, x.shape[0], 8)]
    while len(parts) > 1:
        paired = [op(parts[i], parts[i + 1]) for i in range(0, len(parts) - 1, 2)]
        parts = paired + parts[len(parts) - len(parts) % 2:]
    red = jnp.max if op is jnp.maximum else jnp.sum
    return red(parts[0], axis=0, keepdims=True)


def _attn_prompt_kernel(q_ref, k_ref, vt_ref, bmt_ref, o_ref):
    n_c = ATT_KEYS // ATT_Q_TILE
    n_left = n_c - 1
    lane_q = lax.broadcasted_iota(jnp.int32, (ATT_Q_TILE, LANES), 1)
    ones_rows = jnp.ones((PV_EXTRA_ROWS, ATT_Q_TILE), BF16)
    band_keys = (LEFT_CHUNKS + 2) * CHUNK
    n_lt = ATT_Q_TILE // LANES

    def pieces(t):
        lo, hi = LANES * t, LANES * t + band_keys
        out = []
        for c in range(n_c):
            a, b = max(lo, c * ATT_Q_TILE), min(hi, (c + 1) * ATT_Q_TILE)
            if a < b:
                out.append((c, a - c * ATT_Q_TILE, b - c * ATT_Q_TILE))
        return out

    def window(qb):
        j = pl.program_id(1) * ATT_QB + qb
        kc = [jnp.maximum(j - n_left + c, 0) for c in range(n_c)]
        bc = [jnp.where(j - n_left + c >= 0, c, n_c) for c in range(n_c)]
        return kc, bc

    def scores(qb, hd):
        kc, bc = window(qb)
        pair = hd // 2
        q2 = q_ref[0, qb * ATT_Q_TILE:(qb + 1) * ATT_Q_TILE, pair * LANES:(pair + 1) * LANES]
        k2 = jnp.concatenate(
            [k_ref[0, pl.ds(pl.multiple_of(kc[c] * ATT_Q_TILE, ATT_Q_TILE), ATT_Q_TILE),
                   pair * LANES:(pair + 1) * LANES] for c in range(n_c)], axis=0)
        lo = (hd % 2) * A_HEAD_DIM
        qm = jnp.where((lane_q >= lo) & (lane_q < lo + A_HEAD_DIM), q2, jnp.zeros_like(q2))
        st = lax.dot_general(k2, qm, (((1,), (1,)), ((), ())), preferred_element_type=F32)
        tiles = []
        for t in range(n_lt):
            lanes = slice(t * LANES, (t + 1) * LANES)
            tiles.append([st[c * ATT_Q_TILE + a:c * ATT_Q_TILE + b, lanes] + bmt_ref[hd, bc[c], a:b, lanes]
                          for c, a, b in pieces(t)])
        return tiles

    def softmax(tiles):
        cols = []
        for t in range(n_lt):
            st = jnp.concatenate(tiles[t], axis=0)
            m = _reduce_rows(st, jnp.maximum)
            e = jnp.exp2(st - m).astype(BF16)
            zero = lambda n: [jnp.zeros((n, LANES), BF16)] if n else []
            cols.append(jnp.concatenate(zero(LANES * t) + [e] + zero(ATT_KEYS - band_keys - LANES * t), axis=0))
        return jnp.concatenate(cols, axis=1)

    def pv(qb, hd, pt):
        kc, _ = window(qb)
        ot = jnp.zeros((A_HEAD_DIM + PV_EXTRA_ROWS, ATT_Q_TILE), F32)
        for c in range(n_c):
            vt_c = vt_ref[0, kc[c], hd * A_HEAD_DIM:(hd + 1) * A_HEAD_DIM, :]
            ot = ot + jnp.dot(jnp.concatenate([vt_c, ones_rows], axis=0),
                              pt[c * ATT_Q_TILE:(c + 1) * ATT_Q_TILE, :], preferred_element_type=F32)
        return ot[0:A_HEAD_DIM] * (1.0 / ot[A_HEAD_DIM:A_HEAD_DIM + 1])

    items = [(qb, hd) for qb in range(ATT_QB) for hd in range(A_HEADS)]
    tiles = {it: scores(*it) for it in items[:ATT_AHEAD]}
    prev = None
    for n, (qb, hd) in enumerate(items):
        pt = softmax(tiles.pop((qb, hd)))
        if n + ATT_AHEAD < len(items):
            nxt = items[n + ATT_AHEAD]
            tiles[nxt] = scores(*nxt)
        ot = pv(qb, hd, pt)
        if hd % 2 == 0:
            prev = ot
        else:
            pair_t = jnp.concatenate([prev, ot], axis=0).T.astype(BF16)
            o_ref[0, qb * ATT_Q_TILE:(qb + 1) * ATT_Q_TILE, (hd // 2) * LANES:(hd // 2 + 1) * LANES] = pair_t


def _attn_prompt(q, k, vt, bmt):
    b, s, _ = q.shape
    assert vt.shape == (b, s // ATT_Q_TILE, A_WIDTH, ATT_Q_TILE)
    blk = pl.BlockSpec((1, ATT_QB * ATT_Q_TILE, A_WIDTH), lambda bi, j: (bi, j, 0))
    full = pl.BlockSpec((1, s, A_WIDTH), lambda bi, j: (bi, 0, 0))
    full_t = pl.BlockSpec((1,) + vt.shape[1:], lambda bi, j: (bi, 0, 0, 0))
    return pl.pallas_call(
        _attn_prompt_kernel,
        grid=(b, s // (ATT_QB * ATT_Q_TILE)),
        in_specs=[blk, full, full_t, _resident(bmt.shape)],
        out_specs=blk,
        out_shape=jax.ShapeDtypeStruct((b, s, A_WIDTH), BF16),
        compiler_params=pltpu.CompilerParams(
            dimension_semantics=("arbitrary", "arbitrary"), vmem_limit_bytes=VMEM_LIMIT),
        name="attn_prompt",
    )(q, k, vt, bmt)


def _attn_sample(q, kn, vn, kct, vct, bias):
    b, t, _ = q.shape
    new = pl.BlockSpec((1, t, A_WIDTH), lambda bi: (bi, 0, 0))
    cache = pl.BlockSpec((1,) + kct.shape[1:], lambda bi: (bi, 0, 0, 0))
    return pl.pallas_call(
        _attn_sample_kernel,
        grid=(b,),
        in_specs=[new, new, new, cache, cache, _resident(bias.shape)],
        out_specs=new,
        out_shape=jax.ShapeDtypeStruct((b, t, A_WIDTH), BF16),
        compiler_params=pltpu.CompilerParams(
            dimension_semantics=("arbitrary",), vmem_limit_bytes=VMEM_LIMIT),
        name="attn_sample",
    )(q, kn, vn, kct, vct, bias)


def _retention_kernel(q_ref, k_ref, v_ref, sg_ref, s0_ref, rn_ref, sout_ref,
                      state, dmat, rdec, kdec):
    bi = pl.program_id(0)
    j = pl.program_id(1)
    t = dmat.shape[1]
    n_seq = q_ref.shape[0]
    n_blk = q_ref.shape[1] // t

    @pl.when((bi == 0) & (j == 0))
    def _():
        row = lax.broadcasted_iota(jnp.int32, (t, t), 0)
        col = lax.broadcasted_iota(jnp.int32, (t, t), 1)
        diff = (row - col).astype(F32)
        rowl = lax.broadcasted_iota(jnp.int32, (t, R_DIM), 0).astype(F32)
        for hh in range(R_HEADS):
            lg = LOG_DECAY[hh]
            dmat[hh] = jnp.where(diff >= 0, jnp.exp(jnp.maximum(diff, 0.0) * lg), 0.0)
            rdec[hh] = jnp.exp((rowl + 1.0) * lg)
            kdec[hh] = jnp.exp((t - 1.0 - rowl) * lg)

    @pl.when(j == 0)
    def _():
        state[...] = s0_ref[...]

    s_cur = {(sq, hh): state[sq, hh] for sq in range(n_seq) for hh in range(R_HEADS)}

    def recur(sq, blk, hh):
        rows, sl = slice(blk * t, (blk + 1) * t), slice(hh * R_DIM, (hh + 1) * R_DIM)
        q = q_ref[sq, rows, sl]
        k = k_ref[sq, rows, sl]
        v = v_ref[sq, rows, sl]
        s_prev = s_cur[sq, hh]
        a = lax.dot_general(q, k, (((1,), (1,)), ((), ())), preferred_element_type=F32) * dmat[hh]
        intra = jnp.dot(a.astype(BF16), v, preferred_element_type=F32)
        cross = jnp.dot(q, s_prev.astype(BF16), preferred_element_type=F32) * rdec[hh]
        kd = (k.astype(F32) * kdec[hh]).astype(BF16)
        upd = lax.dot_general(kd, v, (((0,), (0,)), ((), ())), preferred_element_type=F32)
        s_cur[sq, hh] = s_prev * math.exp(t * LOG_DECAY[hh]) + upd
        return intra + cross

    def norm_gate(sq, blk, hh, o):
        rows, sl = slice(blk * t, (blk + 1) * t), slice(hh * R_DIM, (hh + 1) * R_DIM)
        mu = jnp.mean(o, axis=-1, keepdims=True)
        d = o - mu
        var = jnp.mean(d * d, axis=-1, keepdims=True)
        rn = d * lax.rsqrt(var + GN_EPS) * sg_ref[sq, rows, sl].astype(F32)
        rn_ref[sq, rows, sl] = rn.astype(BF16)

    items = [(sq, blk, hh) for sq in range(n_seq) for blk in range(n_blk) for hh in range(R_HEADS)]
    o_next = recur(*items[0])
    for n, it in enumerate(items):
        o = o_next
        if n + 1 < len(items):
            o_next = recur(*items[n + 1])
        norm_gate(*it, o)

    for (sq, hh), s_new in s_cur.items():
        state[sq, hh] = s_new
        sout_ref[sq, hh] = s_new


def _retention(q, k, v, sg, s0, tile, n_blk, n_seq):
    b, s, _ = q.shape
    blk = pl.BlockSpec((n_seq, n_blk * tile, R_WIDTH), lambda bi, j: (bi, j, 0))
    st = pl.BlockSpec((n_seq, R_HEADS, R_DIM, R_DIM), lambda bi, j: (bi, 0, 0, 0))
    return pl.pallas_call(
        _retention_kernel,
        grid=(b // n_seq, s // (n_blk * tile)),
        in_specs=[blk, blk, blk, blk, st],
        out_specs=[blk, st],
        out_shape=[jax.ShapeDtypeStruct((b, s, R_WIDTH), BF16),
                   jax.ShapeDtypeStruct((b, R_HEADS, R_DIM, R_DIM), F32)],
        scratch_shapes=[pltpu.VMEM((n_seq, R_HEADS, R_DIM, R_DIM), F32),
                        pltpu.VMEM((R_HEADS, tile, tile), F32),
                        pltpu.VMEM((R_HEADS, tile, R_DIM), F32),
                        pltpu.VMEM((R_HEADS, tile, R_DIM), F32)],
        compiler_params=pltpu.CompilerParams(
            dimension_semantics=("arbitrary", "arbitrary"), vmem_limit_bytes=VMEM_LIMIT),
        name="retention",
    )(q, k, v, sg, s0)


def _tail_kernel(x_ref, att_ref, rn_ref, sga_ref, sgb_ref, wao_ref, wro_ref, wo_ref,
                 g1_ref, g2_ref, g3_ref, wup_ref, wdn_ref, y_ref):
    n_sub = x_ref.shape[0] // SUB_ROWS
    rows = [slice(r * SUB_ROWS, (r + 1) * SUB_ROWS) for r in range(n_sub)]
    n_ff = D_FF // FF_CHUNK
    assert n_ff == 4
    m, mix, x1, h2, f = ([None] * n_sub for _ in range(5))

    def merge(s):
        r = rows[s]
        ya = jnp.dot(att_ref[r, :], wao_ref[...], preferred_element_type=F32)
        yr = jnp.dot(rn_ref[r, :], wro_ref[...], preferred_element_type=F32)
        m[s] = (sga_ref[r, :].astype(F32) * ya + sgb_ref[r, :].astype(F32) * yr).astype(BF16)

    def project(s):
        mix[s] = jnp.dot(m[s], wo_ref[...], preferred_element_type=F32)

    def norms(s):
        x1[s] = x_ref[rows[s], :] + _rms(mix[s], g1_ref[...])
        h2[s] = _rms(x1[s], g2_ref[...]).astype(BF16)
        f[s] = jnp.zeros_like(x1[s])

    def ffn(s, c):
        u = jnp.dot(h2[s], wup_ref[:, c * FF_CHUNK:(c + 1) * FF_CHUNK], preferred_element_type=F32)
        u = jnp.square(jnp.maximum(u, 0.0)).astype(BF16)
        f[s] = f[s] + jnp.dot(u, wdn_ref[c * FF_CHUNK:(c + 1) * FF_CHUNK, :], preferred_element_type=F32)

    def finish(s):
        y_ref[rows[s], :] = x1[s] + _rms(f[s], g3_ref[...])

    def when(s, fn):
        if s < n_sub:
            fn(s)

    merge(0)
    when(1, merge)
    project(0)
    norms(0)
    when(1, project)
    ffn(0, 0)
    when(1, norms)
    for s in range(n_sub):
        when(s + 2, merge)
        ffn(s, 1)
        when(s + 2, project)
        ffn(s, 2)
        when(s + 2, norms)
        ffn(s, 3)
        finish(s)
        when(s + 1, lambda t: ffn(t, 0))


def _tail(x2, att, rn, sga, sgb, wao, wro, wo, g1, g2, g3, wup, wdn):
    rows = x2.shape[0]
    row_tile = min(rows, ROW_TILE)
    row = lambda w: pl.BlockSpec((row_tile, w), lambda i: (i, 0))
    gspec = _resident((1, D_MODEL))
    return pl.pallas_call(
        _tail_kernel,
        grid=(rows // row_tile,),
        in_specs=[row(D_MODEL), row(A_WIDTH), row(R_WIDTH), row(D_MODEL), row(D_MODEL),
                  _resident(wao.shape), _resident(wro.shape), _resident(wo.shape),
                  gspec, gspec, gspec, _resident(wup.shape), _resident(wdn.shape)],
        out_specs=row(D_MODEL),
        out_shape=jax.ShapeDtypeStruct((rows, D_MODEL), F32),
        compiler_params=pltpu.CompilerParams(
            dimension_semantics=("arbitrary",), vmem_limit_bytes=VMEM_LIMIT),
        name="tail",
    )(x2, att, rn, sga, sgb, wao, wro, wo, g1, g2, g3, wup, wdn)


def _rope_tables(pos):
    half = R_DIM // 2
    inv = ROPE_BASE ** (-jnp.arange(half, dtype=F32) / half)
    ang = pos[:, None] * inv[None, :]
    cos, sin = jnp.cos(ang), jnp.sin(ang)
    return jnp.concatenate([cos, cos], axis=1), jnp.concatenate([-sin, sin], axis=1)


BIAS_ROW = 1024
BIAS_TPAD = 384


def _bias_kernel(t_ref, bpt_ref, bs_ref, *, t_s, nk_valid_s):
    tq = ATT_Q_TILE
    n_heads = t_ref.shape[0]
    t8 = t_ref[:, 0, :]
    t1 = t8.astype(BF16)
    r1 = t8 - t1.astype(F32)
    t2 = r1.astype(BF16)
    t3 = (r1 - t2.astype(F32)).astype(BF16)

    def generator(width, offset, sign):
        k_idx = lax.broadcasted_iota(jnp.int32, (BIAS_TPAD, width), 0)
        x_idx = lax.broadcasted_iota(jnp.int32, (BIAS_TPAD, width), 1)
        sel = jnp.where(k_idx == jnp.clip(offset + sign * x_idx, 0, 2 * REL_CLIP), 1.0, 0.0).astype(BF16)
        gen = (jnp.dot(t1, sel, preferred_element_type=F32) + jnp.dot(t2, sel, preferred_element_type=F32)
               + jnp.dot(t3, sel, preferred_element_type=F32))
        return gen * LOG2E

    def toeplitz(gen_row, rows):
        width = gen_row.shape[1]
        return pltpu.roll(jnp.broadcast_to(gen_row, (rows, width)), width - tq, 1, stride=1, stride_axis=0)

    qc = lax.shift_right_logical(lax.broadcasted_iota(jnp.int32, (tq, tq), 1), 6)
    key = lax.broadcasted_iota(jnp.int32, (tq, tq), 0)
    for c in range(bpt_ref.shape[1]):
        if c * tq < ATT_KEYS:
            gen = generator(2 * tq, ATT_WINDOW + REL_CLIP - tq - tq * c, 1)
            kc = lax.shift_right_logical(key + c * tq, 6)
            valid = (kc >= qc) & (kc <= qc + LEFT_CHUNKS)
            for hh in range(n_heads):
                bpt_ref[hh, c] = jnp.where(valid, toeplitz(gen[hh:hh + 1], tq)[:, 0:tq], NEG_BIG)
        else:
            for hh in range(n_heads):
                bpt_ref[hh, c] = jnp.full((tq, tq), NEG_BIG, F32)
    nk_s = bs_ref.shape[2]
    gen_s = generator(BIAS_ROW, ATT_WINDOW + tq + REL_CLIP, -1)
    lane = lax.broadcasted_iota(jnp.int32, (t_s, nk_s), 1)
    for hh in range(n_heads):
        bs_ref[hh] = jnp.where(lane < nk_valid_s, toeplitz(gen_s[hh:hh + 1], t_s)[:, 0:nk_s], NEG_BIG)


def _build_bias(table, t_s, l_s, nk_s):
    assert CHUNK == 64 and l_s == ATT_WINDOW and t_s <= ATT_Q_TILE and nk_s <= ATT_KEYS
    hn, tl = table.shape
    tpad = jnp.pad(table.astype(F32), ((0, 0), (0, BIAS_TPAD - tl))).reshape(hn, 1, BIAS_TPAD)
    n_c = ATT_KEYS // ATT_Q_TILE + 1
    return pl.pallas_call(
        functools.partial(_bias_kernel, t_s=t_s, nk_valid_s=l_s + t_s),
        grid=(1,),
        in_specs=[pl.BlockSpec((hn, 1, BIAS_TPAD), lambda i: (0, 0, 0))],
        out_specs=[pl.BlockSpec((hn, n_c, ATT_Q_TILE, ATT_Q_TILE), lambda i: (0, 0, 0, 0)),
                   pl.BlockSpec((hn, t_s, nk_s), lambda i: (0, 0, 0))],
        out_shape=[jax.ShapeDtypeStruct((hn, n_c, ATT_Q_TILE, ATT_Q_TILE), F32),
                   jax.ShapeDtypeStruct((hn, t_s, nk_s), F32)],
        compiler_params=pltpu.CompilerParams(
            dimension_semantics=("arbitrary",), vmem_limit_bytes=VMEM_LIMIT),
        name="rel_bias",
    )(tpad)


def kernel(x_prompt, x_sample, cache_attn_k, cache_attn_v, state_retention, norm_mix_pre, w_in,
           rel_bias_table, w_attn_out, w_ret_out, w_o, norm_mix_post, norm_ffn_pre, w_ff_up,
           w_ff_down, norm_ffn_post):
    depth = w_in.shape[0]
    bp, sp, _ = x_prompt.shape
    bs, ts, _ = x_sample.shape
    lc = cache_attn_k.shape[2]
    n_keys_s = -(-(lc + ts) // LANES) * LANES

    cs_p, sn_p = _rope_tables(jnp.arange(sp, dtype=F32))
    pos_s = PAST_LEN + jnp.arange(ts, dtype=F32)
    cs_s, sn_s = _rope_tables(jnp.tile(pos_s, bs))

    xp = x_prompt.reshape(bp * sp, D_MODEL)
    xs = x_sample.reshape(bs * ts, D_MODEL)
    outs = [[] for _ in range(6)]
    for l in range(depth):
        w_in_b = w_in[l].astype(BF16)
        w_v_t = _w_v_transposed(w_in, l)
        g0 = norm_mix_pre[l].reshape(1, D_MODEL)
        g1 = norm_mix_post[l].reshape(1, D_MODEL)
        g2 = norm_ffn_pre[l].reshape(1, D_MODEL)
        g3 = norm_ffn_post[l].reshape(1, D_MODEL)
        bias_p, bias_s = _build_bias(rel_bias_table[l], ts, lc, n_keys_s)

        qa, ka, va_t, qr, kr, vr, sgr, sga, sgb, kt, vt, wao, wro, wo, wup, wdn = _in_proj(
            xp, g0, w_in_b, w_v_t, cs_p, sn_p, seq_rows=sp, tail_rows=ATT_WINDOW, v_transposed=True,
            cast=(w_attn_out[l], w_ret_out[l], w_o[l], w_ff_up[l], w_ff_down[l]))
        tail_w = (wao, wro, wo, g1, g2, g3, wup, wdn)
        r3 = lambda a: a.reshape(bp, sp, a.shape[-1])
        att = _attn_prompt(r3(qa), r3(ka), va_t.reshape(bp, sp // SUB_ROWS, A_WIDTH, SUB_ROWS), bias_p)
        rn, s_p = _retention(r3(qr), r3(kr), r3(vr), r3(sgr),
                             jnp.zeros((bp, R_HEADS, R_DIM, R_DIM), F32), RET_TILE, RET_BLOCKS, 1)
        xp = _tail(xp, att.reshape(bp * sp, A_WIDTH), rn.reshape(bp * sp, R_WIDTH), sga, sgb, *tail_w)
        outs[0].append(kt.reshape(bp, ATT_WINDOW, A_HEADS, A_HEAD_DIM))
        outs[1].append(vt.reshape(bp, ATT_WINDOW, A_HEADS, A_HEAD_DIM))
        outs[2].append(s_p)

        qa, ka, va, qr, kr, vr, sgr, sga, sgb, kt, vt = _in_proj(
            xs, g0, w_in_b, w_v_t, cs_s, sn_s, seq_rows=bs * ts, tail_rows=bs * ts, v_transposed=False)
        r3 = lambda a: a.reshape(bs, ts, a.shape[-1])
        att = _attn_sample(r3(qa), r3(ka), r3(va),
                           cache_attn_k[l].transpose(0, 2, 3, 1), cache_attn_v[l].transpose(0, 2, 3, 1),
                           bias_s)
        rn, s_s = _retention(r3(qr), r3(kr), r3(vr), r3(sgr),
                             state_retention[l].astype(F32), ts, 1, RET_SAMPLE_SEQS)
        xs = _tail(xs, att.reshape(bs * ts, A_WIDTH), rn.reshape(bs * ts, R_WIDTH), sga, sgb, *tail_w)
        outs[3].append(kt.reshape(bs, ts, A_HEADS, A_HEAD_DIM).astype(cache_attn_k.dtype))
        outs[4].append(vt.reshape(bs, ts, A_HEADS, A_HEAD_DIM).astype(cache_attn_v.dtype))
        outs[5].append(s_s.astype(state_retention.dtype))

    return (xp.reshape(bp, sp, D_MODEL), xs.reshape(bs, ts, D_MODEL),
            jnp.stack(outs[0]), jnp.stack(outs[1]), jnp.stack(outs[2]),
            jnp.stack(outs[3]), jnp.stack(outs[4]), jnp.stack(outs[5]))
```

```python
import functools
import math

import jax
import jax.numpy as jnp
import numpy as np
from jax import lax
from jax.experimental import pallas as pl
from jax.experimental.pallas import tpu as pltpu

F32 = jnp.float32
BF16 = jnp.bfloat16

D_MODEL = 1024
CHUNK = 64
LEFT_CHUNKS = 8
ATT_WINDOW = LEFT_CHUNKS * CHUNK
A_HEADS = 8
A_HEAD_DIM = 64
A_WIDTH = A_HEADS * A_HEAD_DIM
REL_CLIP = 128
R_HEADS = 4
R_DIM = 128
R_WIDTH = R_HEADS * R_DIM
ROPE_BASE = 10000.0
D_FF = 4 * D_MODEL
NORM_EPS = 1e-6
GN_EPS = 1e-6
PAST_LEN = 4096

IN_SPLITS = (A_WIDTH, A_WIDTH, A_WIDTH, R_WIDTH, R_WIDTH, R_WIDTH, R_WIDTH, D_MODEL, D_MODEL)
IN_COLS = sum(IN_SPLITS)
IN_OFFS = tuple(int(o) for o in np.cumsum((0,) + IN_SPLITS))

LANES = 128
BF16_SUBLANES = 16
MXU_N = 256
ROW_TILE = 1024
SUB_ROWS = 256
ATT_Q_TILE = 256
ATT_KEYS = ATT_WINDOW + ATT_Q_TILE
ATT_QB = 8
ATT_AHEAD = 3
PV_EXTRA_ROWS = 16
RET_TILE = 256
RET_BLOCKS = 4
RET_SAMPLE_SEQS = 8
FF_CHUNK = 1024
VMEM_LIMIT = 56 * 1024 * 1024
VMEM_LIMIT_IN_PROJ = 60 * 1024 * 1024
NEG_BIG = -1e30
LOG2E = math.log2(math.e)
ATT_Q_SCALE = (A_HEAD_DIM ** -0.5) * LOG2E
LOG_DECAY = tuple(math.log1p(-(2.0 ** (-5 - h))) for h in range(R_HEADS))


def _resident(shape):
    nd = len(shape)
    return pl.BlockSpec(shape, lambda *_: (0,) * nd, pipeline_mode=pl.Buffered(1))


def _rms(x, g):
    ms = jnp.mean(x * x, axis=-1, keepdims=True)
    return x * lax.rsqrt(ms + NORM_EPS) * g


def _inproj_kernel(*refs, v_transposed, n_cast):
    x_ref, g_ref, w_ref, cs_ref, sn_ref = refs[:5]
    cast_in = refs[5:5 + n_cast]
    (qa_ref, ka_ref, va_ref, qr_ref, kr_ref, vr_ref, sgr_ref, sga_ref, sgb_ref,
     kt_ref, vt_ref) = refs[5 + n_cast:16 + n_cast]
    cast_out = refs[16 + n_cast:]
    for src, dst in zip(cast_in, cast_out):
        dst[...] = src[...].astype(BF16)
    row_tile, tail_rows = x_ref.shape[0], kt_ref.shape[0]
    tail_start = row_tile - tail_rows
    for r0 in range(0, row_tile, SUB_ROWS):
        rows = slice(r0, r0 + SUB_ROWS)
        in_tail = r0 >= tail_start
        trows = slice(r0 - tail_start, r0 - tail_start + SUB_ROWS)
        h = _rms(x_ref[rows, :], g_ref[...]).astype(BF16)

        def seg(s):
            return jnp.dot(h, w_ref[:, IN_OFFS[s]:IN_OFFS[s + 1]], preferred_element_type=F32)

        cs = cs_ref[rows, :]
        sn = sn_ref[rows, :]

        def rope(z, scale):
            parts = []
            for hh in range(R_HEADS):
                zh = z[:, hh * R_DIM:(hh + 1) * R_DIM]
                r = zh * cs + pltpu.roll(zh, R_DIM // 2, 1) * sn
                if scale != 1.0:
                    r = r * scale
                parts.append(r.astype(BF16))
            return jnp.concatenate(parts, axis=1)

        qa_ref[rows, :] = (seg(0) * ATT_Q_SCALE).astype(BF16)
        ka = seg(1)
        ka_ref[rows, :] = ka.astype(BF16)
        if in_tail:
            kt_ref[trows] = ka.reshape(SUB_ROWS, A_HEADS, A_HEAD_DIM)
        va = seg(2)
        if v_transposed:
            va_ref[r0 // SUB_ROWS] = va.T.astype(BF16)
        else:
            va_ref[rows, :] = va.astype(BF16)
        if in_tail:
            vt_ref[trows] = va.reshape(SUB_ROWS, A_HEADS, A_HEAD_DIM)
        qr_ref[rows, :] = rope(seg(3), 1.0)
        kr_ref[rows, :] = rope(seg(4), R_DIM ** -0.5)
        vr_ref[rows, :] = seg(5).astype(BF16)
        gr = seg(6)
        sgr_ref[rows, :] = (gr * jax.nn.sigmoid(gr)).astype(BF16)
        sga_ref[rows, :] = jax.nn.sigmoid(seg(7)).astype(BF16)
        sgb_ref[rows, :] = jax.nn.sigmoid(seg(8)).astype(BF16)


def _in_proj(x2, g, w_bf, cs, sn, seq_rows, tail_rows, v_transposed, cast=()):
    rows = x2.shape[0]
    row_tile = min(seq_rows, ROW_TILE)
    assert tail_rows <= row_tile and tail_rows % SUB_ROWS == 0 and seq_rows % row_tile == 0
    tail_period = seq_rows // row_tile
    n = rows // row_tile
    n_pos = cs.shape[0] // row_tile
    n_sub = row_tile // SUB_ROWS
    row = lambda w: pl.BlockSpec((row_tile, w), lambda i: (i, 0))
    tail = pl.BlockSpec((tail_rows, A_HEADS, A_HEAD_DIM), lambda i: (i // tail_period, 0, 0))
    pos = pl.BlockSpec((row_tile, R_DIM), lambda i: (i % n_pos, 0))
    bf = lambda w: jax.ShapeDtypeStruct((rows, w), BF16)
    tail_shape = jax.ShapeDtypeStruct((rows // seq_rows * tail_rows, A_HEADS, A_HEAD_DIM), F32)
    if v_transposed:
        v_spec = pl.BlockSpec((n_sub, A_WIDTH, SUB_ROWS), lambda i: (i, 0, 0))
        v_shape = jax.ShapeDtypeStruct((rows // SUB_ROWS, A_WIDTH, SUB_ROWS), BF16)
    else:
        v_spec, v_shape = row(A_WIDTH), bf(A_WIDTH)
    cast_specs, cast_shapes = [], []
    for w in cast:
        assert w.shape[0] % n == 0 and (w.shape[0] // n) % BF16_SUBLANES == 0
        cast_specs.append(pl.BlockSpec((w.shape[0] // n, w.shape[1]), lambda i: (i, 0)))
        cast_shapes.append(jax.ShapeDtypeStruct(w.shape, BF16))
    return pl.pallas_call(
        functools.partial(_inproj_kernel, v_transposed=v_transposed, n_cast=len(cast)),
        grid=(n,),
        in_specs=[row(D_MODEL), _resident((1, D_MODEL)), _resident((D_MODEL, IN_COLS)), pos, pos]
        + cast_specs,
        out_specs=[row(A_WIDTH), row(A_WIDTH), v_spec] + [row(R_WIDTH)] * 4 + [row(D_MODEL)] * 2
        + [tail, tail] + cast_specs,
        out_shape=[bf(A_WIDTH), bf(A_WIDTH), v_shape] + [bf(R_WIDTH)] * 4 + [bf(D_MODEL)] * 2
        + [tail_shape, tail_shape] + cast_shapes,
        compiler_params=pltpu.CompilerParams(
            dimension_semantics=("arbitrary",), vmem_limit_bytes=VMEM_LIMIT_IN_PROJ),
        name="in_proj",
    )(x2, g, w_bf, cs, sn, *cast)


def _attn_sample_kernel(q_ref, kn_ref, vn_ref, kct_ref, vct_ref, bias_ref, o_ref):
    tq = q_ref.shape[1]
    lc = kct_ref.shape[3]
    n_pad = bias_ref.shape[2] - lc
    q_all = q_ref[0]
    zrows = jnp.zeros((n_pad - tq, A_WIDTH), BF16)
    kn_pad = jnp.concatenate([kn_ref[0], zrows], axis=0)
    vn_pad = jnp.concatenate([vn_ref[0], zrows], axis=0)
    lane_q = lax.broadcasted_iota(jnp.int32, (tq, LANES), 1)
    lane_o = lax.broadcasted_iota(jnp.int32, (tq, MXU_N), 1)
    heads_per_group = MXU_N // A_HEAD_DIM
    pairs_per_group = heads_per_group // 2
    nt = (((1,), (1,)), ((), ()))
    outs = []
    for grp in range(A_HEADS // heads_per_group):
        e_parts, d_parts = [], []
        for pg in range(pairs_per_group):
            pair = grp * pairs_per_group + pg
            q2 = q_all[:, pair * LANES:(pair + 1) * LANES]
            zero = jnp.zeros_like(q2)
            q_st = jnp.concatenate([jnp.where(lane_q < A_HEAD_DIM, q2, zero),
                                    jnp.where(lane_q >= A_HEAD_DIM, q2, zero)], axis=0)
            kt_pair = kct_ref[0, 2 * pair:2 * pair + 2].reshape(LANES, lc).astype(BF16)
            s_c = jnp.dot(q_st, kt_pair, preferred_element_type=F32)
            s_n = lax.dot_general(q_st, kn_pad[:, pair * LANES:(pair + 1) * LANES], nt,
                                  preferred_element_type=F32)
            s = jnp.concatenate([s_c, s_n], axis=1)
            s = s + jnp.concatenate([bias_ref[2 * pair], bias_ref[2 * pair + 1]], axis=0)
            m = jnp.max(s, axis=-1, keepdims=True)
            e = jnp.exp2(s - m)
            d_parts.append(jnp.sum(e, axis=-1, keepdims=True))
            e_parts.append(e.astype(BF16))
        p_st = jnp.concatenate(e_parts, axis=0)
        vt_grp = vct_ref[0, heads_per_group * grp:heads_per_group * (grp + 1)].reshape(MXU_N, lc).astype(BF16)
        o = lax.dot_general(p_st[:, 0:lc], vt_grp, nt, preferred_element_type=F32)
        o = o + jnp.dot(p_st[:, lc:], vn_pad[:, grp * MXU_N:(grp + 1) * MXU_N], preferred_element_type=F32)
        o = o * (1.0 / jnp.concatenate(d_parts, axis=0))
        acc = jnp.zeros((tq, MXU_N), F32)
        for hg in range(heads_per_group):
            olo = hg * A_HEAD_DIM
            acc = jnp.where((lane_o >= olo) & (lane_o < olo + A_HEAD_DIM), o[hg * tq:(hg + 1) * tq], acc)
        outs.append(acc.astype(BF16))
    o_ref[0] = jnp.concatenate(outs, axis=1)


def _reduce_rows(x, op):
    parts = [x[r:r + 8] for r in range(0, x.shape[0], 8)]
    while len(parts) > 1:
        paired = [op(parts[i], parts[i + 1]) for i in range(0, len(parts) - 1, 2)]
        parts = paired + parts[len(parts) - len(parts) % 2:]
    red = jnp.max if op is jnp.maximum else jnp.sum
    return red(parts[0], axis=0, keepdims=True)


def _attn_prompt_kernel(q_ref, k_ref, vt_ref, bmt_ref, o_ref):
    n_c = ATT_KEYS // ATT_Q_TILE
    n_left = n_c - 1
    lane_q = lax.broadcasted_iota(jnp.int32, (ATT_Q_TILE, LANES), 1)
    ones_rows = jnp.ones((PV_EXTRA_ROWS, ATT_Q_TILE), BF16)
    band_keys = (LEFT_CHUNKS + 2) * CHUNK
    n_lt = ATT_Q_TILE // LANES

    def pieces(t):
        lo, hi = LANES * t, LANES * t + band_keys
        out = []
        for c in range(n_c):
            a, b = max(lo, c * ATT_Q_TILE), min(hi, (c + 1) * ATT_Q_TILE)
            if a < b:
                out.append((c, a - c * ATT_Q_TILE, b - c * ATT_Q_TILE))
        return out

    def window(qb):
        j = pl.program_id(1) * ATT_QB + qb
        kc = [jnp.maximum(j - n_left + c, 0) for c in range(n_c)]
        bc = [jnp.where(j - n_left + c >= 0, c, n_c) for c in range(n_c)]
        return kc, bc

    def scores(qb, hd):
        kc, bc = window(qb)
        pair = hd // 2
        q2 = q_ref[0, qb * ATT_Q_TILE:(qb + 1) * ATT_Q_TILE, pair * LANES:(pair + 1) * LANES]
        k2 = jnp.concatenate(
            [k_ref[0, pl.ds(pl.multiple_of(kc[c] * ATT_Q_TILE, ATT_Q_TILE), ATT_Q_TILE),
                   pair * LANES:(pair + 1) * LANES] for c in range(n_c)], axis=0)
        lo = (hd % 2) * A_HEAD_DIM
        qm = jnp.where((lane_q >= lo) & (lane_q < lo + A_HEAD_DIM), q2, jnp.zeros_like(q2))
        st = lax.dot_general(k2, qm, (((1,), (1,)), ((), ())), preferred_element_type=F32)
        tiles = []
        for t in range(n_lt):
            lanes = slice(t * LANES, (t + 1) * LANES)
            tiles.append([st[c * ATT_Q_TILE + a:c * ATT_Q_TILE + b, lanes] + bmt_ref[hd, bc[c], a:b, lanes]
                          for c, a, b in pieces(t)])
        return tiles

    def softmax(tiles):
        cols = []
        for t in range(n_lt):
            st = jnp.concatenate(tiles[t], axis=0)
            m = _reduce_rows(st, jnp.maximum)
            e = jnp.exp2(st - m).astype(BF16)
            zero = lambda n: [jnp.zeros((n, LANES), BF16)] if n else []
            cols.append(jnp.concatenate(zero(LANES * t) + [e] + zero(ATT_KEYS - band_keys - LANES * t), axis=0))
        return jnp.concatenate(cols, axis=1)

    def pv(qb, hd, pt):
        kc, _ = window(qb)
        ot = jnp.zeros((A_HEAD_DIM + PV_EXTRA_ROWS, ATT_Q_TILE), F32)
        for c in range(n_c):
            vt_c = vt_ref[0, kc[c], hd * A_HEAD_DIM:(hd + 1) * A_HEAD_DIM, :]
            ot = ot + jnp.dot(jnp.concatenate([vt_c, ones_rows], axis=0),
                              pt[c * ATT_Q_TILE:(c + 1) * ATT_Q_TILE, :], preferred_element_type=F32)
        return ot[0:A_HEAD_DIM] * (1.0 / ot[A_HEAD_DIM:A_HEAD_DIM + 1])

    items = [(qb, hd) for qb in range(ATT_QB) for hd in range(A_HEADS)]
    tiles = {it: scores(*it) for it in items[:ATT_AHEAD]}
    prev = None
    for n, (qb, hd) in enumerate(items):
        pt = softmax(tiles.pop((qb, hd)))
        if n + ATT_AHEAD < len(items):
            nxt = items[n + ATT_AHEAD]
            tiles[nxt] = scores(*nxt)
        ot = pv(qb, hd, pt)
        if hd % 2 == 0:
            prev = ot
        else:
            pair_t = jnp.concatenate([prev, ot], axis=0).T.astype(BF16)
            o_ref[0, qb * ATT_Q_TILE:(qb + 1) * ATT_Q_TILE, (hd // 2) * LANES:(hd // 2 + 1) * LANES] = pair_t


def _attn_prompt(q, k, vt, bmt):
    b, s, _ = q.shape
    assert vt.shape == (b, s // ATT_Q_TILE, A_WIDTH, ATT_Q_TILE)
    blk = pl.BlockSpec((1, ATT_QB * ATT_Q_TILE, A_WIDTH), lambda bi, j: (bi, j, 0))
    full = pl.BlockSpec((1, s, A_WIDTH), lambda bi, j: (bi, 0, 0))
    full_t = pl.BlockSpec((1,) + vt.shape[1:], lambda bi, j: (bi, 0, 0, 0))
    return pl.pallas_call(
        _attn_prompt_kernel,
        grid=(b, s // (ATT_QB * ATT_Q_TILE)),
        in_specs=[blk, full, full_t, _resident(bmt.shape)],
        out_specs=blk,
        out_shape=jax.ShapeDtypeStruct((b, s, A_WIDTH), BF16),
        compiler_params=pltpu.CompilerParams(
            dimension_semantics=("arbitrary", "arbitrary"), vmem_limit_bytes=VMEM_LIMIT),
        name="attn_prompt",
    )(q, k, vt, bmt)


def _attn_sample(q, kn, vn, kct, vct, bias):
    b, t, _ = q.shape
    new = pl.BlockSpec((1, t, A_WIDTH), lambda bi: (bi, 0, 0))
    cache = pl.BlockSpec((1,) + kct.shape[1:], lambda bi: (bi, 0, 0, 0))
    return pl.pallas_call(
        _attn_sample_kernel,
        grid=(b,),
        in_specs=[new, new, new, cache, cache, _resident(bias.shape)],
        out_specs=new,
        out_shape=jax.ShapeDtypeStruct((b, t, A_WIDTH), BF16),
        compiler_params=pltpu.CompilerParams(
            dimension_semantics=("arbitrary",), vmem_limit_bytes=VMEM_LIMIT),
        name="attn_sample",
    )(q, kn, vn, kct, vct, bias)


def _retention_kernel(q_ref, k_ref, v_ref, sg_ref, s0_ref, rn_ref, sout_ref,
                      state, dmat, rdec, kdec):
    bi = pl.program_id(0)
    j = pl.program_id(1)
    t = dmat.shape[1]
    n_seq = q_ref.shape[0]
    n_blk = q_ref.shape[1] // t

    @pl.when((bi == 0) & (j == 0))
    def _():
        row = lax.broadcasted_iota(jnp.int32, (t, t), 0)
        col = lax.broadcasted_iota(jnp.int32, (t, t), 1)
        diff = (row - col).astype(F32)
        rowl = lax.broadcasted_iota(jnp.int32, (t, R_DIM), 0).astype(F32)
        for hh in range(R_HEADS):
            lg = LOG_DECAY[hh]
            dmat[hh] = jnp.where(diff >= 0, jnp.exp(jnp.maximum(diff, 0.0) * lg), 0.0)
            rdec[hh] = jnp.exp((rowl + 1.0) * lg)
            kdec[hh] = jnp.exp((t - 1.0 - rowl) * lg)

    @pl.when(j == 0)
    def _():
        state[...] = s0_ref[...]

    s_cur = {(sq, hh): state[sq, hh] for sq in range(n_seq) for hh in range(R_HEADS)}

    def recur(sq, blk, hh):
        rows, sl = slice(blk * t, (blk + 1) * t), slice(hh * R_DIM, (hh + 1) * R_DIM)
        q = q_ref[sq, rows, sl]
        k = k_ref[sq, rows, sl]
        v = v_ref[sq, rows, sl]
        s_prev = s_cur[sq, hh]
        a = lax.dot_general(q, k, (((1,), (1,)), ((), ())), preferred_element_type=F32) * dmat[hh]
        intra = jnp.dot(a.astype(BF16), v, preferred_element_type=F32)
        cross = jnp.dot(q, s_prev.astype(BF16), preferred_element_type=F32) * rdec[hh]
        kd = (k.astype(F32) * kdec[hh]).astype(BF16)
        upd = lax.dot_general(kd, v, (((0,), (0,)), ((), ())), preferred_element_type=F32)
        s_cur[sq, hh] = s_prev * math.exp(t * LOG_DECAY[hh]) + upd
        return intra + cross

    def norm_gate(sq, blk, hh, o):
        rows, sl = slice(blk * t, (blk + 1) * t), slice(hh * R_DIM, (hh + 1) * R_DIM)
        mu = jnp.mean(o, axis=-1, keepdims=True)
        d = o - mu
        var = jnp.mean(d * d, axis=-1, keepdims=True)
        rn = d * lax.rsqrt(var + GN_EPS) * sg_ref[sq, rows, sl].astype(F32)
        rn_ref[sq, rows, sl] = rn.astype(BF16)

    items = [(sq, blk, hh) for sq in range(n_seq) for blk in range(n_blk) for hh in range(R_HEADS)]
    o_next = recur(*items[0])
    for n, it in enumerate(items):
        o = o_next
        if n + 1 < len(items):
            o_next = recur(*items[n + 1])
        norm_gate(*it, o)

    for (sq, hh), s_new in s_cur.items():
        state[sq, hh] = s_new
        sout_ref[sq, hh] = s_new


def _retention(q, k, v, sg, s0, tile, n_blk, n_seq):
    b, s, _ = q.shape
    blk = pl.BlockSpec((n_seq, n_blk * tile, R_WIDTH), lambda bi, j: (bi, j, 0))
    st = pl.BlockSpec((n_seq, R_HEADS, R_DIM, R_DIM), lambda bi, j: (bi, 0, 0, 0))
    return pl.pallas_call(
        _retention_kernel,
        grid=(b // n_seq, s // (n_blk * tile)),
        in_specs=[blk, blk, blk, blk, st],
        out_specs=[blk, st],
        out_shape=[jax.ShapeDtypeStruct((b, s, R_WIDTH), BF16),
                   jax.ShapeDtypeStruct((b, R_HEADS, R_DIM, R_DIM), F32)],
        scratch_shapes=[pltpu.VMEM((n_seq, R_HEADS, R_DIM, R_DIM), F32),
                        pltpu.VMEM((R_HEADS, tile, tile), F32),
                        pltpu.VMEM((R_HEADS, tile, R_DIM), F32),
                        pltpu.VMEM((R_HEADS, tile, R_DIM), F32)],
        compiler_params=pltpu.CompilerParams(
            dimension_semantics=("arbitrary", "arbitrary"), vmem_limit_bytes=VMEM_LIMIT),
        name="retention",
    )(q, k, v, sg, s0)


def _tail_kernel(x_ref, att_ref, rn_ref, sga_ref, sgb_ref, wao_ref, wro_ref, wo_ref,
                 g1_ref, g2_ref, g3_ref, wup_ref, wdn_ref, y_ref):
    n_sub = x_ref.shape[0] // SUB_ROWS
    rows = [slice(r * SUB_ROWS, (r + 1) * SUB_ROWS) for r in range(n_sub)]
    n_ff = D_FF // FF_CHUNK
    assert n_ff == 4
    m, mix, x1, h2, f = ([None] * n_sub for _ in range(5))

    def merge(s):
        r = rows[s]
        ya = jnp.dot(att_ref[r, :], wao_ref[...], preferred_element_type=F32)
        yr = jnp.dot(rn_ref[r, :], wro_ref[...], preferred_element_type=F32)
        m[s] = (sga_ref[r, :].astype(F32) * ya + sgb_ref[r, :].astype(F32) * yr).astype(BF16)

    def project(s):
        mix[s] = jnp.dot(m[s], wo_ref[...], preferred_element_type=F32)

    def norms(s):
        x1[s] = x_ref[rows[s], :] + _rms(mix[s], g1_ref[...])
        h2[s] = _rms(x1[s], g2_ref[...]).astype(BF16)
        f[s] = jnp.zeros_like(x1[s])

    def ffn(s, c):
        u = jnp.dot(h2[s], wup_ref[:, c * FF_CHUNK:(c + 1) * FF_CHUNK], preferred_element_type=F32)
        u = jnp.square(jnp.maximum(u, 0.0)).astype(BF16)
        f[s] = f[s] + jnp.dot(u, wdn_ref[c * FF_CHUNK:(c + 1) * FF_CHUNK, :], preferred_element_type=F32)

    def finish(s):
        y_ref[rows[s], :] = x1[s] + _rms(f[s], g3_ref[...])

    def when(s, fn):
        if s < n_sub:
            fn(s)

    merge(0)
    when(1, merge)
    project(0)
    norms(0)
    when(1, project)
    ffn(0, 0)
    when(1, norms)
    for s in range(n_sub):
        when(s + 2, merge)
        ffn(s, 1)
        when(s + 2, project)
        ffn(s, 2)
        when(s + 2, norms)
        ffn(s, 3)
        finish(s)
        when(s + 1, lambda t: ffn(t, 0))


def _tail(x2, att, rn, sga, sgb, wao, wro, wo, g1, g2, g3, wup, wdn):
    rows = x2.shape[0]
    row_tile = min(rows, ROW_TILE)
    row = lambda w: pl.BlockSpec((row_tile, w), lambda i: (i, 0))
    gspec = _resident((1, D_MODEL))
    return pl.pallas_call(
        _tail_kernel,
        grid=(rows // row_tile,),
        in_specs=[row(D_MODEL), row(A_WIDTH), row(R_WIDTH), row(D_MODEL), row(D_MODEL),
                  _resident(wao.shape), _resident(wro.shape), _resident(wo.shape),
                  gspec, gspec, gspec, _resident(wup.shape), _resident(wdn.shape)],
        out_specs=row(D_MODEL),
        out_shape=jax.ShapeDtypeStruct((rows, D_MODEL), F32),
        compiler_params=pltpu.CompilerParams(
            dimension_semantics=("arbitrary",), vmem_limit_bytes=VMEM_LIMIT),
        name="tail",
    )(x2, att, rn, sga, sgb, wao, wro, wo, g1, g2, g3, wup, wdn)


def _rope_tables(pos):
    half = R_DIM // 2
    inv = ROPE_BASE ** (-jnp.arange(half, dtype=F32) / half)
    ang = pos[:, None] * inv[None, :]
    cos, sin = jnp.cos(ang), jnp.sin(ang)
    return jnp.concatenate([cos, cos], axis=1), jnp.concatenate([-sin, sin], axis=1)


BIAS_ROW = 1024
BIAS_TPAD = 384


def _bias_kernel(t_ref, bpt_ref, bs_ref, *, t_s, nk_valid_s):
    tq = ATT_Q_TILE
    n_heads = t_ref.shape[0]
    t8 = t_ref[:, 0, :]
    t1 = t8.astype(BF16)
    r1 = t8 - t1.astype(F32)
    t2 = r1.astype(BF16)
    t3 = (r1 - t2.astype(F32)).astype(BF16)

    def generator(width, offset, sign):
        k_idx = lax.broadcasted_iota(jnp.int32, (BIAS_TPAD, width), 0)
        x_idx = lax.broadcasted_iota(jnp.int32, (BIAS_TPAD, width), 1)
        sel = jnp.where(k_idx == jnp.clip(offset + sign * x_idx, 0, 2 * REL_CLIP), 1.0, 0.0).astype(BF16)
        gen = (jnp.dot(t1, sel, preferred_element_type=F32) + jnp.dot(t2, sel, preferred_element_type=F32)
               + jnp.dot(t3, sel, preferred_element_type=F32))
        return gen * LOG2E

    def toeplitz(gen_row, rows):
        width = gen_row.shape[1]
        return pltpu.roll(jnp.broadcast_to(gen_row, (rows, width)), width - tq, 1, stride=1, stride_axis=0)

    qc = lax.shift_right_logical(lax.broadcasted_iota(jnp.int32, (tq, tq), 1), 6)
    key = lax.broadcasted_iota(jnp.int32, (tq, tq), 0)
    for c in range(bpt_ref.shape[1]):
        if c * tq < ATT_KEYS:
            gen = generator(2 * tq, ATT_WINDOW + REL_CLIP - tq - tq * c, 1)
            kc = lax.shift_right_logical(key + c * tq, 6)
            valid = (kc >= qc) & (kc <= qc + LEFT_CHUNKS)
            for hh in range(n_heads):
                bpt_ref[hh, c] = jnp.where(valid, toeplitz(gen[hh:hh + 1], tq)[:, 0:tq], NEG_BIG)
        else:
            for hh in range(n_heads):
                bpt_ref[hh, c] = jnp.full((tq, tq), NEG_BIG, F32)
    nk_s = bs_ref.shape[2]
    gen_s = generator(BIAS_ROW, ATT_WINDOW + tq + REL_CLIP, -1)
    lane = lax.broadcasted_iota(jnp.int32, (t_s, nk_s), 1)
    for hh in range(n_heads):
        bs_ref[hh] = jnp.where(lane < nk_valid_s, toeplitz(gen_s[hh:hh + 1], t_s)[:, 0:nk_s], NEG_BIG)


def _build_bias(table, t_s, l_s, nk_s):
    assert CHUNK == 64 and l_s == ATT_WINDOW and t_s <= ATT_Q_TILE and nk_s <= ATT_KEYS
    hn, tl = table.shape
    tpad = jnp.pad(table.astype(F32), ((0, 0), (0, BIAS_TPAD - tl))).reshape(hn, 1, BIAS_TPAD)
    n_c = ATT_KEYS // ATT_Q_TILE + 1
    return pl.pallas_call(
        functools.partial(_bias_kernel, t_s=t_s, nk_valid_s=l_s + t_s),
        grid=(1,),
        in_specs=[pl.BlockSpec((hn, 1, BIAS_TPAD), lambda i: (0, 0, 0))],
        out_specs=[pl.BlockSpec((hn, n_c, ATT_Q_TILE, ATT_Q_TILE), lambda i: (0, 0, 0, 0)),
                   pl.BlockSpec((hn, t_s, nk_s), lambda i: (0, 0, 0))],
        out_shape=[jax.ShapeDtypeStruct((hn, n_c, ATT_Q_TILE, ATT_Q_TILE), F32),
                   jax.ShapeDtypeStruct((hn, t_s, nk_s), F32)],
        compiler_params=pltpu.CompilerParams(
            dimension_semantics=("arbitrary",), vmem_limit_bytes=VMEM_LIMIT),
        name="rel_bias",
    )(tpad)


def kernel(x_prompt, x_sample, cache_attn_k, cache_attn_v, state_retention, norm_mix_pre, w_in,
           rel_bias_table, w_attn_out, w_ret_out, w_o, norm_mix_post, norm_ffn_pre, w_ff_up,
           w_ff_down, norm_ffn_post):
    depth = w_in.shape[0]
    bp, sp, _ = x_prompt.shape
    bs, ts, _ = x_sample.shape
    lc = cache_attn_k.shape[2]
    n_keys_s = -(-(lc + ts) // LANES) * LANES

    cs_p, sn_p = _rope_tables(jnp.arange(sp, dtype=F32))
    pos_s = PAST_LEN + jnp.arange(ts, dtype=F32)
    cs_s, sn_s = _rope_tables(jnp.tile(pos_s, bs))

    xp = x_prompt.reshape(bp * sp, D_MODEL)
    xs = x_sample.reshape(bs * ts, D_MODEL)
    outs = [[] for _ in range(6)]
    for l in range(depth):
        w_in_b = w_in[l].astype(BF16)
        g0 = norm_mix_pre[l].reshape(1, D_MODEL)
        g1 = norm_mix_post[l].reshape(1, D_MODEL)
        g2 = norm_ffn_pre[l].reshape(1, D_MODEL)
        g3 = norm_ffn_post[l].reshape(1, D_MODEL)
        bias_p, bias_s = _build_bias(rel_bias_table[l], ts, lc, n_keys_s)

        qa, ka, va_t, qr, kr, vr, sgr, sga, sgb, kt, vt, wao, wro, wo, wup, wdn = _in_proj(
            xp, g0, w_in_b, cs_p, sn_p, seq_rows=sp, tail_rows=ATT_WINDOW, v_transposed=True,
            cast=(w_attn_out[l], w_ret_out[l], w_o[l], w_ff_up[l], w_ff_down[l]))
        tail_w = (wao, wro, wo, g1, g2, g3, wup, wdn)
        r3 = lambda a: a.reshape(bp, sp, a.shape[-1])
        att = _attn_prompt(r3(qa), r3(ka), va_t.reshape(bp, sp // SUB_ROWS, A_WIDTH, SUB_ROWS), bias_p)
        rn, s_p = _retention(r3(qr), r3(kr), r3(vr), r3(sgr),
                             jnp.zeros((bp, R_HEADS, R_DIM, R_DIM), F32), RET_TILE, RET_BLOCKS, 1)
        xp = _tail(xp, att.reshape(bp * sp, A_WIDTH), rn.reshape(bp * sp, R_WIDTH), sga, sgb, *tail_w)
        outs[0].append(kt.reshape(bp, ATT_WINDOW, A_HEADS, A_HEAD_DIM))
        outs[1].append(vt.reshape(bp, ATT_WINDOW, A_HEADS, A_HEAD_DIM))
        outs[2].append(s_p)

        qa, ka, va, qr, kr, vr, sgr, sga, sgb, kt, vt = _in_proj(
            xs, g0, w_in_b, cs_s, sn_s, seq_rows=bs * ts, tail_rows=bs * ts, v_transposed=False)
        r3 = lambda a: a.reshape(bs, ts, a.shape[-1])
        att = _attn_sample(r3(qa), r3(ka), r3(va),
                           cache_attn_k[l].transpose(0, 2, 3, 1), cache_attn_v[l].transpose(0, 2, 3, 1),
                           bias_s)
        rn, s_s = _retention(r3(qr), r3(kr), r3(vr), r3(sgr),
                             state_retention[l].astype(F32), ts, 1, RET_SAMPLE_SEQS)
        xs = _tail(xs, att.reshape(bs * ts, A_WIDTH), rn.reshape(bs * ts, R_WIDTH), sga, sgb, *tail_w)
        outs[3].append(kt.reshape(bs, ts, A_HEADS, A_HEAD_DIM).astype(cache_attn_k.dtype))
        outs[4].append(vt.reshape(bs, ts, A_HEADS, A_HEAD_DIM).astype(cache_attn_v.dtype))
        outs[5].append(s_s.astype(state_retention.dtype))

    return (xp.reshape(bp, sp, D_MODEL), xs.reshape(bs, ts, D_MODEL),
            jnp.stack(outs[0]), jnp.stack(outs[1]), jnp.stack(outs[2]),
            jnp.stack(outs[3]), jnp.stack(outs[4]), jnp.stack(outs[5]))
```
